```python
import math
import jax
import jax.numpy as jnp
from jax import lax
import numpy as np

D_MODEL = 1024
BATCH = 16
SEQ = 2048
DEPTH = 4

CTX_LEN = 256
GRID_W = 64
N_EVEN = (DEPTH + 1) // 2
N_ODD = DEPTH // 2
D_FF = 4 * D_MODEL
EPS = 1e-6
ROPE_BASE = 10000.0
MIX_A = D_MODEL // 2
MLSTM_HEADS = 4
MLSTM_HD = MIX_A // MLSTM_HEADS
MLSTM_CHUNK = 64
N_GATES = 4 * MLSTM_HEADS
FORGET_BIAS_LO = 3.0
FORGET_BIAS_HI = 6.0
MIX_B = D_MODEL - MIX_A
HYENA_ORDER = 2
HYENA_SHORT = 3
HYENA_EMB = 33
HYENA_FFN = 64
HYENA_TARGET = 1e-2
HYENA_FAST = 0.3
HYENA_SLOW = 1.5
E_IN = 4 * MIX_A + N_GATES + (HYENA_ORDER + 1) * MIX_B
NA_HEADS = 16
NA_HD = D_MODEL // NA_HEADS
NA_WIN_ROWS = 8
NA_WIN_COLS = 16

kernel_name = "hybrid_mlstm_hyena_natten_dit_trunk"


def rms_norm(x, gain=None):
    x32 = x.astype(jnp.float32)
    y = x32 * lax.rsqrt(jnp.mean(x32 * x32, axis=-1, keepdims=True) + EPS)
    if gain is not None:
        y = y * gain.astype(jnp.float32)
    return y.astype(x.dtype)


def rope_1d(xa, pos):
    nf = xa.shape[-1] // 2
    inv = ROPE_BASE ** (-jnp.arange(nf, dtype=jnp.float32) / nf)
    ang = pos.astype(jnp.float32)[:, None] * inv[None, :]
    cos, sin = jnp.cos(ang), jnp.sin(ang)
    x1, x2 = xa[..., :nf], xa[..., nf:]
    return jnp.concatenate([x1 * cos - x2 * sin, x1 * sin + x2 * cos], axis=-1)


def axial_rope(x, rows, cols):
    half = x.shape[-1] // 2
    return jnp.concatenate([rope_1d(x[..., :half], rows), rope_1d(x[..., half:], cols)], axis=-1)


def mlstm_zero_state(b):
    f32 = jnp.float32
    return (jnp.zeros((b, MLSTM_HEADS, MLSTM_HD, MLSTM_HD), f32),
            jnp.zeros((b, MLSTM_HEADS, MLSTM_HD), f32),
            jnp.zeros((b, MLSTM_HEADS), f32))


def mlstm_chunkwise(q, k, v, ig, lf, state):
    B, H, L, dh = q.shape
    nc = L // MLSTM_CHUNK

    def to_chunks(a):
        a = a.reshape(a.shape[:2] + (nc, MLSTM_CHUNK) + a.shape[3:])
        return jnp.moveaxis(a, 2, 0)

    tri = jnp.tril(jnp.ones((MLSTM_CHUNK, MLSTM_CHUNK), dtype=bool))

    def step(carry, inp):
        C, n, m = carry
        qc, kc, vc, ic, fc = inp
        b = jnp.cumsum(fc, axis=-1)
        dmat = b[..., :, None] - b[..., None, :] + ic[..., None, :]
        dmat = jnp.where(tri, dmat, -jnp.inf)
        inter = b + m[..., None]
        m_t = jnp.maximum(inter, jnp.max(dmat, axis=-1))
        w = jnp.exp(dmat - m_t[..., None])
        sc = jnp.exp(inter - m_t)
        qk = jnp.einsum('bhtd,bhsd->bhts', qc, kc) * w
        num = sc[..., None] * jnp.einsum('bhvk,bhtk->bhtv', C, qc) + jnp.einsum('bhts,bhsv->bhtv', qk, vc)
        den = sc * jnp.einsum('bhk,bhtk->bht', n, qc) + jnp.sum(qk, axis=-1)
        h = num / jnp.maximum(jnp.abs(den), jnp.exp(-m_t))[..., None]
        bl = b[..., -1]
        g = bl[..., None] - b + ic
        m_new = jnp.maximum(bl + m, jnp.max(g, axis=-1))
        decay = jnp.exp(bl + m - m_new)
        wk = jnp.exp(g - m_new[..., None])
        C_new = decay[..., None, None] * C + jnp.einsum('bhs,bhsv,bhsk->bhvk', wk, vc, kc)
        n_new = decay[..., None] * n + jnp.einsum('bhs,bhsk->bhk', wk, kc)
        return (C_new, n_new, m_new), h

    state, hs = lax.scan(step, state, tuple(to_chunks(a) for a in (q, k, v, ig, lf)))
    h = jnp.moveaxis(hs, 0, 2).reshape(B, H, L, dh)
    return h, state


def mlstm_bidir(q, k, v, gpre, init):
    B, L, _ = gpre.shape
    g = gpre.reshape(B, L, 4, MLSTM_HEADS).transpose(2, 0, 3, 1)
    h_f, s_f = mlstm_chunkwise(q, k, v, g[0], jax.nn.log_sigmoid(g[1]), init[0])
    fl = lambda a: jnp.flip(a, axis=2)
    h_b, s_b = mlstm_chunkwise(fl(q), fl(k), fl(v), fl(g[2]), fl(jax.nn.log_sigmoid(g[3])), init[1])
    return h_f + fl(h_b), (s_f, s_b)


def short_conv(x, w, b):
    L = x.shape[1]
    pad = HYENA_SHORT // 2
    xp = jnp.pad(x, ((0, 0), (pad, pad), (0, 0)))
    y = b
    for j in range(HYENA_SHORT):
        y = y + xp[:, j:j + L] * w[j]
    return y


def hyena_filters(L, w1, b1, w2, b2, w3, freq):
    f32 = jnp.float32
    t = jnp.linspace(0.0, 1.0, L, dtype=f32)[:, None]
    bands = (HYENA_EMB - 1) // 2
    wpos = 2.0 * math.pi * jnp.arange(L, dtype=f32) / L
    fr = jnp.linspace(1e-4, bands - 1, bands, dtype=f32)
    ang = wpos[:, None] * fr[None, :]
    emb = jnp.concatenate([t, jnp.cos(ang), -jnp.sin(ang)], axis=-1)
    h = jnp.sin(freq * (emb @ w1 + b1))
    h = jnp.sin(freq * (h @ w2 + b2))
    h = (h @ w3).astype(f32).reshape(L, HYENA_ORDER, 2, MIX_B)
    deltas = jnp.abs(jnp.linspace(math.log(HYENA_TARGET) / HYENA_SLOW, math.log(HYENA_TARGET) / HYENA_FAST, MIX_B, dtype=f32))
    h = h * jnp.exp(-t * deltas)[:, None, None, :]
    return h * lax.rsqrt(jnp.sum(h * h, axis=0, keepdims=True) + EPS)


def fft_conv(z, h):
    L = z.shape[1]
    n = 2 * L
    zf = jnp.fft.rfft(z, n=n, axis=1)
    hf = jnp.fft.rfft(h, n=n, axis=0)
    return jnp.fft.irfft(zf * hf[None], n=n, axis=1)[:, :L]


def bidir_long_conv(z, h_fwd, h_bwd):
    return fft_conv(z, h_fwd) + jnp.flip(fft_conv(jnp.flip(z, axis=1), h_bwd), axis=1)


def hyena(u, x1, x2, filt, dskip):
    z = u.astype(jnp.float32)
    for o, gate in enumerate((x1, x2)):
        z = gate.astype(jnp.float32) * (bidir_long_conv(z, filt[:, o, 0], filt[:, o, 1]) + dskip[o] * z)
    return z


def even_mixer(hx, hc, rows, cols, w_in, gate_b, hnorm, conv_w, conv_b, f_w1, f_b1, f_w2, f_b2, f_w3, f_freq, hy_d, w_out, ctx_out):
    f32 = jnp.float32
    cuts = [MIX_A, 2 * MIX_A, 3 * MIX_A, 4 * MIX_A, 4 * MIX_A + N_GATES]

    def heads(a):
        b_, l_, _ = a.shape
        return a.reshape(b_, l_, MLSTM_HEADS, MLSTM_HD).transpose(0, 2, 1, 3).astype(f32)

    def project(h, rope):
        q, k, v, o, g, hy = jnp.split(h @ w_in, cuts, axis=-1)
        q, k, v = heads(q), heads(k) * MLSTM_HD ** -0.5, heads(v)
        if rope:
            q, k = axial_rope(q, rows, cols), axial_rope(k, rows, cols)
        return q, k, v, o, g.astype(f32) + gate_b, hy

    def combine(hm, o, hy, dtype):
        b_, h_, l_, d_ = hm.shape
        hm = rms_norm(hm, hnorm.reshape(MLSTM_HEADS, 1, MLSTM_HD))
        a_out = hm.transpose(0, 2, 1, 3).reshape(b_, l_, h_ * d_) * jax.nn.sigmoid(o.astype(f32))
        u, x1, x2 = jnp.split(short_conv(hy, conv_w, conv_b), 3, axis=-1)
        filt = hyena_filters(l_, f_w1, f_b1, f_w2, f_b2, f_w3, f_freq)
        b_out = hyena(u, x1, x2, filt, hy_d)
        return (jnp.concatenate([a_out, b_out], axis=-1) @ w_out).astype(dtype)

    qc, kc, vc, oc, gc, hyc = project(hc, False)
    hm_c, ctx_states = mlstm_bidir(qc, kc, vc, gc, (mlstm_zero_state(hc.shape[0]), mlstm_zero_state(hc.shape[0])))
    q, k, v, o, g, hy = project(hx, True)
    hm_x, _ = mlstm_bidir(q, k, v, g, ctx_states)
    out_x = combine(hm_x, o, hy, hx.dtype)
    out_c = combine(hm_c, oc, hyc, hc.dtype) if ctx_out else None
    return out_x, out_c


def odd_mixer(hx, hc, w_qkv, qn, kn, rpb, w_out, ctx_out):
    f32 = jnp.float32

    def project(h):
        b_, l_, _ = h.shape
        p = (h @ w_qkv).reshape(b_, l_, 3, NA_HEADS, NA_HD)
        q = rms_norm(p[:, :, 0], qn) * NA_HD ** -0.5
        k = rms_norm(p[:, :, 1], kn)
        v = p[:, :, 2]
        return tuple(a.transpose(0, 2, 1, 3) for a in (q, k, v))

    qc, kc, vc = project(hc)
    q, k, v = project(hx)
    B, H, S, d = q.shape
    rows_n = S // GRID_W
    wr = min(NA_WIN_ROWS, rows_n)
    nk = wr * GRID_W
    q_g = q.reshape(B, H, rows_n, GRID_W, d)
    k_g = k.reshape(B, H, rows_n, GRID_W, d)
    v_g = v.reshape(B, H, rows_n, GRID_W, d)
    cidx = jnp.arange(GRID_W)
    cstart = jnp.clip(cidx - NA_WIN_COLS // 2, 0, GRID_W - NA_WIN_COLS)
    cmask = (cidx[None, :] >= cstart[:, None]) & (cidx[None, :] < cstart[:, None] + NA_WIN_COLS)
    blk_mask = jnp.broadcast_to(cmask[:, None, :], (GRID_W, wr, GRID_W)).reshape(GRID_W, nk)
    dc_idx = jnp.clip(cidx[None, :] - cidx[:, None] + NA_WIN_COLS - 1, 0, 2 * NA_WIN_COLS - 2)

    def row_block(args):
        q_r, r = args
        rs = jnp.clip(r - wr // 2, 0, rows_n - wr)
        k_blk = lax.dynamic_slice_in_dim(k_g, rs, wr, axis=2).reshape(B, H, nk, d)
        v_blk = lax.dynamic_slice_in_dim(v_g, rs, wr, axis=2).reshape(B, H, nk, d)
        dr_idx = rs + jnp.arange(wr) - r + NA_WIN_ROWS - 1
        bias = rpb[:, dr_idx[None, :, None], dc_idx[:, None, :]].reshape(H, GRID_W, nk)
        s_lat = jnp.einsum('bhqd,bhkd->bhqk', q_r, k_blk).astype(f32) + bias.astype(f32)
        s_lat = jnp.where(blk_mask, s_lat, -jnp.inf)
        s_ctx = jnp.einsum('bhqd,bhkd->bhqk', q_r, kc).astype(f32)
        p = jax.nn.softmax(jnp.concatenate([s_lat, s_ctx], axis=-1), axis=-1).astype(v.dtype)
        return (jnp.einsum('bhqk,bhkd->bhqd', p[..., :nk], v_blk)
                + jnp.einsum('bhqk,bhkd->bhqd', p[..., nk:], vc))

    o = lax.map(row_block, (jnp.moveaxis(q_g, 2, 0), jnp.arange(rows_n)))
    o = o.transpose(1, 0, 3, 2, 4).reshape(B, S, H * d)
    out_x = (o @ w_out).astype(hx.dtype)
    out_c = None
    if ctx_out:
        s = jnp.einsum('bhqd,bhkd->bhqk', qc, kc).astype(f32)
        pc = jax.nn.softmax(s, axis=-1).astype(vc.dtype)
        oc = jnp.einsum('bhqk,bhkd->bhqd', pc, vc).transpose(0, 2, 1, 3).reshape(B, -1, H * d)
        out_c = (oc @ w_out).astype(hc.dtype)
    return out_x, out_c


def sq_relu_mlp(h, w1, w2):
    return jnp.square(jax.nn.relu(h @ w1)) @ w2


def setup_inputs(seed: int = 0) -> dict:
    key = jax.random.key(seed)
    ks = iter(jax.random.split(key, 32))
    nrm = lambda shape, s: jax.random.normal(next(ks), shape, jnp.float32) * s
    D = D_MODEL
    H = MLSTM_HEADS
    f_lin = jnp.linspace(FORGET_BIAS_LO, FORGET_BIAS_HI, H, dtype=jnp.float32)
    gate_base = jnp.concatenate([jnp.zeros((H,), jnp.float32), f_lin, jnp.zeros((H,), jnp.float32), f_lin])
    return {
        "x": nrm((BATCH, SEQ, D), 1.0),
        "c": nrm((BATCH, D), 1.0),
        "ctx": nrm((BATCH, CTX_LEN, D), 1.0),
        "c_ctx": nrm((D,), 1.0),
        "w_mod": nrm((DEPTH, D, 6 * D), 0.5 * D ** -0.5),
        "b_mod": nrm((DEPTH, 6 * D), 0.01),
        "w_mlp_in": nrm((DEPTH, D, D_FF), D ** -0.5),
        "w_mlp_out": nrm((DEPTH, D_FF, D), D_FF ** -0.5),
        "e_w_in": nrm((N_EVEN, D, E_IN), D ** -0.5),
        "e_gate_b": gate_base[None, :] + nrm((N_EVEN, N_GATES), 0.1),
        "e_hnorm": 1.0 + nrm((N_EVEN, MIX_A), 0.02),
        "e_conv_w": nrm((N_EVEN, HYENA_SHORT, (HYENA_ORDER + 1) * MIX_B), HYENA_SHORT ** -0.5),
        "e_conv_b": nrm((N_EVEN, (HYENA_ORDER + 1) * MIX_B), 0.01),
        "e_f_w1": nrm((N_EVEN, HYENA_EMB, HYENA_FFN), HYENA_EMB ** -0.5),
        "e_f_b1": nrm((N_EVEN, HYENA_FFN), 0.1),
        "e_f_w2": nrm((N_EVEN, HYENA_FFN, HYENA_FFN), HYENA_FFN ** -0.5),
        "e_f_b2": nrm((N_EVEN, HYENA_FFN), 0.1),
        "e_f_w3": nrm((N_EVEN, HYENA_FFN, HYENA_ORDER * 2 * MIX_B), HYENA_FFN ** -0.5),
        "e_f_freq": 1.0 + nrm((N_EVEN, HYENA_FFN), 0.02),
        "e_hy_d": nrm((N_EVEN, HYENA_ORDER, MIX_B), 0.5),
        "e_w_out": nrm((N_EVEN, MIX_A + MIX_B, D), (MIX_A + MIX_B) ** -0.5),
        "o_w_qkv": nrm((N_ODD, D, 3 * D), D ** -0.5),
        "o_qn": 1.0 + nrm((N_ODD, NA_HD), 0.02),
        "o_kn": 1.0 + nrm((N_ODD, NA_HD), 0.02),
        "o_rpb": nrm((N_ODD, NA_HEADS, 2 * NA_WIN_ROWS - 1, 2 * NA_WIN_COLS - 1), 0.02),
        "o_w_out": nrm((N_ODD, D, D), D ** -0.5),
    }


def reference(x, c, ctx, c_ctx, w_mod, b_mod, w_mlp_in, w_mlp_out, e_w_in, e_gate_b, e_hnorm, e_conv_w, e_conv_b, e_f_w1, e_f_b1, e_f_w2, e_f_b2, e_f_w3, e_f_freq, e_hy_d, e_w_out, o_w_qkv, o_qn, o_kn, o_rpb, o_w_out):
    S = x.shape[1]
    t = jnp.arange(S)
    rows, cols = t // GRID_W, t % GRID_W
    silu_c = jax.nn.silu(c)
    silu_cc = jax.nn.silu(c_ctx)
    xc = ctx
    for l in range(DEPTH):
        last = l == DEPTH - 1
        mod_x = (silu_c @ w_mod[l] + b_mod[l])[:, None, :]
        mod_c = silu_cc @ w_mod[l] + b_mod[l]
        sh1, sc1, g1, sh2, sc2, g2 = jnp.split(mod_x, 6, axis=-1)
        csh1, csc1, cg1, csh2, csc2, cg2 = jnp.split(mod_c, 6, axis=-1)
        hx = rms_norm(x) * (1.0 + sc1) + sh1
        hc = rms_norm(xc) * (1.0 + csc1) + csh1
        i = l // 2
        if l % 2 == 0:
            mx, mc = even_mixer(hx, hc, rows, cols, e_w_in[i], e_gate_b[i], e_hnorm[i], e_conv_w[i], e_conv_b[i],
                                e_f_w1[i], e_f_b1[i], e_f_w2[i], e_f_b2[i], e_f_w3[i], e_f_freq[i], e_hy_d[i],
                                e_w_out[i], not last)
        else:
            mx, mc = odd_mixer(hx, hc, o_w_qkv[i], o_qn[i], o_kn[i], o_rpb[i], o_w_out[i], not last)
        x = x + g1 * mx
        x = x + g2 * sq_relu_mlp(rms_norm(x) * (1.0 + sc2) + sh2, w_mlp_in[l], w_mlp_out[l])
        if not last:
            xc = xc + cg1 * mc
            xc = xc + cg2 * sq_relu_mlp(rms_norm(xc) * (1.0 + csc2) + csh2, w_mlp_in[l], w_mlp_out[l])
    return x
```

```python
import functools
import math

import numpy as np
import jax
import jax.numpy as jnp
from jax import lax
from jax.experimental import pallas as pl
from jax.experimental.pallas import tpu as pltpu

F32 = jnp.float32
BF16 = jnp.bfloat16

D_MODEL = 1024
D_FF = 4 * D_MODEL
EPS = 1e-6
ROPE_BASE = 10000.0
GRID_W = 64
MIX_A = D_MODEL // 2
MIX_B = D_MODEL - MIX_A
MLSTM_HEADS = 4
MLSTM_HD = MIX_A // MLSTM_HEADS
N_GATES = 4 * MLSTM_HEADS
HYENA_EMB = 33
HYENA_FFN = 64
HYENA_TARGET = 1e-2
HYENA_FAST = 0.3
HYENA_SLOW = 1.5
NA_HEADS = 16
NA_HD = D_MODEL // NA_HEADS
NA_WIN_ROWS = 8
NA_WIN_COLS = 16

LANES = 128
ROW_TILE = 256
CHUNK = 256
VMEM_LIMIT = 56 * 1024 * 1024
NEG_BIG = -1e30


def _cparams(*sem):
    return pltpu.CompilerParams(dimension_semantics=sem, vmem_limit_bytes=VMEM_LIMIT)


def _dot(a, b):
    return jnp.dot(a, b, preferred_element_type=F32)


def _dot_nt(a, b):
    return lax.dot_general(a, b, (((1,), (1,)), ((), ())), preferred_element_type=F32)


def _dot_tn(a, b):
    return lax.dot_general(a, b, (((0,), (0,)), ((), ())), preferred_element_type=F32)


def _resident(shape, index_map):
    return pl.BlockSpec(shape, index_map, pipeline_mode=pl.Buffered(1))


def _normmod(x, sh, sc):
    ms = jnp.mean(x * x, axis=-1, keepdims=True)
    return (x * lax.rsqrt(ms + EPS)) * (1.0 + sc) + sh


def _sigmoid(x):
    return 1.0 / (1.0 + jnp.exp(-x))


def _log_sigmoid(x):
    return jnp.minimum(x, 0.0) - jnp.log(1.0 + jnp.exp(-jnp.abs(x)))


def _mod_kernel(c_ref, w_ref, b_ref, o_ref):
    c = c_ref[...]
    s = (c * _sigmoid(c)).astype(BF16)
    o_ref[...] = _dot(s, w_ref[...].astype(BF16)) + b_ref[...]


def _modulation(cvec, w_mod, b_mod):
    depth, d, d6 = w_mod.shape
    ns = cvec.shape[0]
    nj = d6 // d
    out = pl.pallas_call(
        _mod_kernel,
        grid=(depth, nj),
        in_specs=[
            pl.BlockSpec((ns, d), lambda l, j: (0, 0)),
            pl.BlockSpec((None, d, d), lambda l, j: (l, 0, j)),
            pl.BlockSpec((None, 1, d), lambda l, j: (l, 0, j)),
        ],
        out_specs=pl.BlockSpec((None, ns, d), lambda l, j: (l, 0, j)),
        out_shape=jax.ShapeDtypeStruct((depth, ns, d6), F32),
        compiler_params=_cparams("arbitrary", "arbitrary"),
        name="modulation",
    )(cvec, w_mod, b_mod.reshape(depth, 1, d6))
    return out.reshape(depth, ns, nj, 1, d)


class _Tiles:
    def __init__(self, batch, tok, ctx):
        assert ctx == ROW_TILE and tok % ROW_TILE == 0
        self.batch, self.tok, self.ctx = batch, tok, ctx
        self.per_batch = tok // ROW_TILE
        self.n = batch * self.per_batch

    def sample(self, i):
        return jnp.where(i % self.per_batch == 0, self.batch, i // self.per_batch)

    def mod_spec(self, layer, which):
        return pl.BlockSpec((None, None, None, 1, D_MODEL),
                            lambda i: (layer, self.sample(i), which, 0, 0))

    def row_spec(self, width):
        return pl.BlockSpec((ROW_TILE, width), lambda i: (i, 0))

    def pos_spec(self, width):
        return pl.BlockSpec((ROW_TILE, width), lambda i: (i % self.per_batch, 0))


def _even_in_kernel(x_ref, sh_ref, sc_ref, w_ref, gb_ref, cos_ref, sin_ref,
                    q_ref, k_ref, v_ref, sg_ref, g_ref, hy_ref):
    h = _normmod(x_ref[...], sh_ref[...], sc_ref[...]).astype(BF16)
    cos = cos_ref[...]
    sin = sin_ref[...]
    lane = lax.broadcasted_iota(jnp.int32, cos.shape, 1)
    first = (lane % (MLSTM_HD // 2)) < (MLSTM_HD // 4)

    def rope(a):
        part = jnp.where(first, pltpu.roll(a, LANES - MLSTM_HD // 4, 1), pltpu.roll(a, MLSTM_HD // 4, 1))
        return a * cos + part * sin

    acc = _dot(h, w_ref[:, 0:MIX_A])
    for hd in range(MLSTM_HEADS):
        sl = slice(hd * MLSTM_HD, (hd + 1) * MLSTM_HD)
        q_ref[:, sl] = rope(acc[:, sl]).astype(BF16)
    acc = _dot(h, w_ref[:, MIX_A:2 * MIX_A]) * (MLSTM_HD ** -0.5)
    for hd in range(MLSTM_HEADS):
        sl = slice(hd * MLSTM_HD, (hd + 1) * MLSTM_HD)
        k_ref[:, sl] = rope(acc[:, sl]).astype(BF16)
    v_ref[...] = _dot(h, w_ref[:, 2 * MIX_A:3 * MIX_A]).astype(BF16)
    sg_ref[...] = _sigmoid(_dot(h, w_ref[:, 3 * MIX_A:4 * MIX_A])).astype(BF16)
    g0 = 4 * MIX_A
    g_ref[...] = _dot(h, w_ref[:, g0:g0 + LANES]) + gb_ref[...]
    h0 = g0 + LANES
    for j in range(3):
        hy_ref[:, j * MIX_B:(j + 1) * MIX_B] = _dot(h, w_ref[:, h0 + j * MIX_B:h0 + (j + 1) * MIX_B])


def _even_in_proj(x2, mod5, layer, tiles, w_pad, gate_b_pad, cos_t, sin_t):
    m = x2.shape[0]
    n_w = w_pad.shape[1]
    outs = (
        jax.ShapeDtypeStruct((m, MIX_A), BF16), jax.ShapeDtypeStruct((m, MIX_A), BF16),
        jax.ShapeDtypeStruct((m, MIX_A), BF16), jax.ShapeDtypeStruct((m, MIX_A), BF16),
        jax.ShapeDtypeStruct((m, LANES), F32), jax.ShapeDtypeStruct((m, 3 * MIX_B), F32),
    )
    return pl.pallas_call(
        _even_in_kernel,
        grid=(tiles.n,),
        in_specs=[
            tiles.row_spec(D_MODEL), tiles.mod_spec(layer, 0), tiles.mod_spec(layer, 1),
            _resident((D_MODEL, n_w), lambda i: (0, 0)),
            pl.BlockSpec((1, LANES), lambda i: (0, 0)),
            tiles.pos_spec(MLSTM_HD), tiles.pos_spec(MLSTM_HD),
        ],
        out_specs=[tiles.row_spec(MIX_A)] * 4 + [tiles.row_spec(LANES), tiles.row_spec(3 * MIX_B)],
        out_shape=outs,
        compiler_params=_cparams("parallel"),
        name="even_in_proj",
    )(x2, mod5, mod5, w_pad, gate_b_pad, cos_t, sin_t)


def _rope_tables(ctx, seq):
    half = MLSTM_HD // 2
    nf = half // 2
    inv = ROPE_BASE ** (-np.arange(nf, dtype=np.float64) / nf)
    t = np.arange(seq)
    rows, cols = t // GRID_W, t % GRID_W

    def one(pos):
        ang = pos[:, None].astype(np.float64) * inv[None, :]
        c = np.concatenate([np.cos(ang), np.cos(ang)], axis=-1)
        s = np.concatenate([-np.sin(ang), np.sin(ang)], axis=-1)
        return c, s

    cr, sr = one(rows)
    cc, sc = one(cols)
    cos = np.concatenate([cr, cc], axis=-1)
    sin = np.concatenate([sr, sc], axis=-1)
    cos = np.concatenate([np.ones((ctx, MLSTM_HD)), cos], axis=0)
    sin = np.concatenate([np.zeros((ctx, MLSTM_HD)), sin], axis=0)
    return jnp.asarray(cos, F32), jnp.asarray(sin, F32)


def _mlstm_chunk(qc, kc, vc, i_col, f_col, i_row, f_row, state, lower):
    ct, n, m = state
    t = qc.shape[0]
    r = lax.broadcasted_iota(jnp.int32, (t, t), 0)
    c = lax.broadcasted_iota(jnp.int32, (t, t), 1)
    incl = (c <= r) if lower else (c >= r)
    incl_t = (r <= c) if lower else (r >= c)
    lf_col = _log_sigmoid(f_col)
    lf_row = _log_sigmoid(f_row)
    b_col = jnp.sum(jnp.where(incl, lf_row, 0.0), axis=1, keepdims=True)
    b_row = jnp.sum(jnp.where(incl_t, lf_col, 0.0), axis=0, keepdims=True)
    a_row = i_row - b_row
    a_col = i_col - b_col
    dmat = jnp.where(incl, b_col + a_row, -jnp.inf)
    inter = b_col + m
    m_t = jnp.maximum(inter, jnp.max(dmat, axis=1, keepdims=True))
    w = jnp.exp(dmat - m_t)
    sc = jnp.exp(inter - m_t)
    qkw = _dot_nt(qc, kc) * w
    num = sc * _dot(qc, ct.astype(BF16)) + _dot(qkw.astype(BF16), vc)
    den = sc * jnp.sum(qc.astype(F32) * n, axis=1, keepdims=True) + jnp.sum(qkw, axis=1, keepdims=True)
    h = num / jnp.maximum(jnp.abs(den), jnp.exp(-m_t))
    bl = b_col[t - 1:t, :] if lower else b_col[0:1, :]
    m_new = bl + jnp.maximum(m, jnp.max(a_col, axis=0, keepdims=True))
    decay = jnp.exp(bl + m - m_new)
    wk = jnp.exp(bl + a_col - m_new)
    kw = kc.astype(F32) * wk
    ct_new = decay * ct + _dot_tn(kw.astype(BF16), vc)
    n_new = decay * n + jnp.sum(kw, axis=0, keepdims=True)
    return h, (ct_new, n_new, m_new)


def _mlstm_kernel(q_ref, k_ref, v_ref, sg_ref, gc_ref, gr_ref, hn_ref, o_ref, hf_ref, hb_ref, *, n_chunks):
    head = pl.program_id(1)
    lane = lax.broadcasted_iota(jnp.int32, (CHUNK, LANES), 1)
    sub = lax.broadcasted_iota(jnp.int32, (N_GATES, CHUNK), 0)

    def gates(off, kind):
        idx = kind * MLSTM_HEADS + head
        col = jnp.sum(jnp.where(lane == idx, gc_ref[pl.ds(off, CHUNK), :], 0.0), axis=1, keepdims=True)
        row = jnp.sum(jnp.where(sub == idx, gr_ref[:, pl.ds(off, CHUNK)], 0.0), axis=0, keepdims=True)
        return col, row

    def run(off, state, lower, dst_ref):
        i_col, i_row = gates(off, 0 if lower else 2)
        f_col, f_row = gates(off, 1 if lower else 3)
        rows = pl.ds(off, CHUNK)
        h, state = _mlstm_chunk(q_ref[rows, :], k_ref[rows, :], v_ref[rows, :],
                                i_col, f_col, i_row, f_row, state, lower)
        dst_ref[rows, :] = h
        return state

    def body(s, carry):
        sf, sb = carry
        off_f = pl.multiple_of(s * CHUNK, CHUNK)
        off_b = pl.multiple_of(jnp.where(s == 0, 0, n_chunks - s) * CHUNK, CHUNK)
        sf = run(off_f, sf, True, hf_ref)
        sb = run(off_b, sb, False, hb_ref)
        return sf, sb

    zero = (jnp.zeros((MLSTM_HD, MLSTM_HD), F32), jnp.zeros((1, MLSTM_HD), F32), jnp.zeros((1, 1), F32))
    lax.fori_loop(0, n_chunks, body, (zero, zero))

    hm = hf_ref[...] + hb_ref[...]
    ms = jnp.mean(hm * hm, axis=-1, keepdims=True)
    y = hm * lax.rsqrt(ms + EPS) * hn_ref[...]
    o_ref[...] = (y * sg_ref[...].astype(F32)).astype(BF16)


def _mlstm(q, k, v, sg, gcol, grow, hnorm):
    b, tok, _ = q.shape
    head_spec = pl.BlockSpec((None, tok, MLSTM_HD), lambda i, h: (i, 0, h))
    return pl.pallas_call(
        functools.partial(_mlstm_kernel, n_chunks=tok // CHUNK),
        grid=(b, MLSTM_HEADS),
        in_specs=[
            head_spec, head_spec, head_spec, head_spec,
            pl.BlockSpec((None, tok, LANES), lambda i, h: (i, 0, 0)),
            pl.BlockSpec((None, N_GATES, tok), lambda i, h: (i, 0, 0)),
            pl.BlockSpec((1, MLSTM_HD), lambda i, h: (0, h)),
        ],
        out_specs=head_spec,
        out_shape=jax.ShapeDtypeStruct((b, tok, MIX_A), BF16),
        scratch_shapes=[pltpu.VMEM((tok, MLSTM_HD), F32), pltpu.VMEM((tok, MLSTM_HD), F32)],
        compiler_params=_cparams("parallel", "arbitrary"),
        name="mlstm",
    )(q, k, v, sg, gcol, grow, hnorm)


def _short_conv_kernel(x_ref, w_ref, b_ref, o_ref, *, ctx):
    x = x_ref[...]
    tok = x.shape[0]
    t = lax.broadcasted_iota(jnp.int32, x.shape, 0)
    prev = jnp.where((t == 0) | (t == ctx), 0.0, pltpu.roll(x, 1, 0))
    nxt = jnp.where((t == ctx - 1) | (t == tok - 1), 0.0, pltpu.roll(x, tok - 1, 0))
    o_ref[...] = b_ref[...] + prev * w_ref[0:1, :] + x * w_ref[1:2, :] + nxt * w_ref[2:3, :]


def _short_conv(hy, conv_w, conv_b, ctx):
    b, tok, ch = hy.shape
    nblk = ch // MIX_B
    return pl.pallas_call(
        functools.partial(_short_conv_kernel, ctx=ctx),
        grid=(b, nblk),
        in_specs=[
            pl.BlockSpec((None, tok, MIX_B), lambda i, j: (i, 0, j)),
            pl.BlockSpec((3, MIX_B), lambda i, j: (0, j)),
            pl.BlockSpec((1, MIX_B), lambda i, j: (0, j)),
        ],
        out_specs=pl.BlockSpec((None, tok, MIX_B), lambda i, j: (i, 0, j)),
        out_shape=jax.ShapeDtypeStruct((b, tok, ch), F32),
        compiler_params=_cparams("parallel", "arbitrary"),
        name="hyena_short_conv",
    )(hy, conv_w, conv_b.reshape(1, ch))


def _filter_kernel(emb_ref, w1_ref, b1_ref, w2_ref, b2_ref, fr_ref, w3_ref, dec_ref, o_ref):
    fr = fr_ref[...]
    h = jnp.sin(fr * (_dot(emb_ref[...].astype(BF16), w1_ref[...]) + b1_ref[...]))
    h = jnp.sin(fr * (_dot(h.astype(BF16), w2_ref[...]) + b2_ref[...]))
    h = _dot(h.astype(BF16), w3_ref[...]) * dec_ref[...]
    o_ref[...] = h * lax.rsqrt(jnp.sum(h * h, axis=0, keepdims=True) + EPS)


def _hyena_filters(length, w1p, b1p, w2p, b2p, frp, w3p):
    bands = (HYENA_EMB - 1) // 2
    t = np.linspace(0.0, 1.0, length)[:, None]
    wpos = 2.0 * math.pi * np.arange(length) / length
    fr = np.linspace(1e-4, bands - 1, bands)
    ang = wpos[:, None] * fr[None, :]
    emb = np.concatenate([t, np.cos(ang), -np.sin(ang)], axis=-1)
    emb = np.pad(emb, ((0, 0), (0, LANES - HYENA_EMB)))
    deltas = np.abs(np.linspace(math.log(HYENA_TARGET) / HYENA_SLOW, math.log(HYENA_TARGET) / HYENA_FAST, MIX_B))
    decay = np.exp(-t * deltas[None, :])
    ngrp = w3p.shape[1] // MIX_B
    const = lambda g: (0, 0)
    return pl.pallas_call(
        _filter_kernel,
        grid=(ngrp,),
        in_specs=[
            pl.BlockSpec((length, LANES), const), pl.BlockSpec((LANES, LANES), const),
            pl.BlockSpec((1, LANES), const), pl.BlockSpec((LANES, LANES), const),
            pl.BlockSpec((1, LANES), const), pl.BlockSpec((1, LANES), const),
            pl.BlockSpec((LANES, MIX_B), lambda g: (0, g)),
            pl.BlockSpec((length, MIX_B), const),
        ],
        out_specs=pl.BlockSpec((length, MIX_B), lambda g: (0, g)),
        out_shape=jax.ShapeDtypeStruct((length, ngrp * MIX_B), F32),
        compiler_params=_cparams("arbitrary"),
        name="hyena_filters",
    )(jnp.asarray(emb, F32), w1p, b1p, w2p, b2p, frp, w3p, jnp.asarray(decay, F32))


def _dft_matrices(length):
    period = 2 * length
    k = jnp.arange(length, dtype=jnp.int32)[:, None]
    t = jnp.arange(length, dtype=jnp.int32)[None, :]
    ang = ((k * t) & (period - 1)).astype(F32) * (2.0 * math.pi / period)
    c = jnp.cos(ang)
    s = jnp.sin(ang)
    alt_t = jnp.where(t % 2 == 0, 1.0, -1.0).astype(F32)
    fwd = jnp.concatenate([c, jnp.where(k == 0, alt_t, -s)], axis=0)
    coef = jnp.where(t == 0, 1.0, 2.0).astype(F32) / period
    alt_k = jnp.where(k % 2 == 0, 1.0, -1.0).astype(F32)
    inv = jnp.concatenate([c * coef, jnp.where(t == 0, alt_k / period, -s * coef)], axis=1)
    return fwd.astype(BF16), inv.astype(BF16)


def _spectrum_kernel(a_ref, hf_ref, hb_ref, o_ref):
    half = pl.program_id(0)
    hf = hf_ref[...]
    hb = hb_ref[...]
    comb = jnp.where(half == 0, hf + hb, hf - hb).astype(BF16)
    o_ref[...] = _dot(a_ref[...], comb)

    @pl.when(half == 1)
    def _():
        t = lax.broadcasted_iota(jnp.int32, hf.shape, 0)
        o_ref[0:1, :] = jnp.sum(jnp.where(t % 2 == 0, hf + hb, -(hf + hb)), axis=0, keepdims=True)


def _filter_spectrum(fwd, filt):
    period, length = fwd.shape
    orders = filt.shape[1] // (2 * MIX_B)
    return pl.pallas_call(
        _spectrum_kernel,
        grid=(2, orders),
        in_specs=[
            pl.BlockSpec((length, length), lambda hlf, o: (hlf, 0)),
            pl.BlockSpec((length, MIX_B), lambda hlf, o: (0, 2 * o)),
            pl.BlockSpec((length, MIX_B), lambda hlf, o: (0, 2 * o + 1)),
        ],
        out_specs=pl.BlockSpec((length, MIX_B), lambda hlf, o: (hlf, o)),
        out_shape=jax.ShapeDtypeStruct((period, orders * MIX_B), F32),
        compiler_params=_cparams("arbitrary", "arbitrary"),
        name="hyena_filter_spectrum",
    )(fwd, filt, filt)


FREQ_CHUNK = 512


def _spectral_product(a_ref, g_ref, z, y_ref, y_off, length):
    ck = min(FREQ_CHUNK, length)
    for c0 in range(0, length, ck):
        zr = _dot(a_ref[c0:c0 + ck, :], z)
        zi = _dot(a_ref[length + c0:length + c0 + ck, :], z)
        gr = g_ref[c0:c0 + ck, :]
        gi = g_ref[length + c0:length + c0 + ck, :]
        yr = zr * gr - zi * gi
        yi = zr * gi + zi * gr
        if c0 == 0:
            first = lax.broadcasted_iota(jnp.int32, yr.shape, 0) == 0
            yr = jnp.where(first, zr * gr, yr)
            yi = jnp.where(first, zi * gi, yi)
        y_ref[y_off + c0:y_off + c0 + ck, :] = yr.astype(BF16)
        y_ref[y_off + length + c0:y_off + length + c0 + ck, :] = yi.astype(BF16)


def _hy_fwd_kernel(z_ref, al_ref, ac_ref, gl_ref, gc_ref, y_ref, *, ctx, seq):
    _spectral_product(ac_ref, gc_ref, z_ref[0:ctx, :].astype(BF16), y_ref, 0, ctx)
    _spectral_product(al_ref, gl_ref, z_ref[ctx:ctx + seq, :].astype(BF16), y_ref, 2 * ctx, seq)


def _hy_fwd(z, zblk, a_lat, a_ctx, g_lat, g_ctx, order, ctx, seq):
    b, tok, _ = z.shape
    rows = 2 * tok
    return pl.pallas_call(
        functools.partial(_hy_fwd_kernel, ctx=ctx, seq=seq),
        grid=(b,),
        in_specs=[
            pl.BlockSpec((None, tok, MIX_B), lambda i: (i, 0, zblk)),
            _resident(a_lat.shape, lambda i: (0, 0)),
            _resident(a_ctx.shape, lambda i: (0, 0)),
            _resident((2 * seq, MIX_B), lambda i: (0, order)),
            _resident((2 * ctx, MIX_B), lambda i: (0, order)),
        ],
        out_specs=pl.BlockSpec((None, rows, MIX_B), lambda i: (i, 0, 0)),
        out_shape=jax.ShapeDtypeStruct((b, rows, MIX_B), BF16),
        compiler_params=_cparams("parallel"),
        name="hyena_dft_mul",
    )(z, a_lat, a_ctx, g_lat, g_ctx)


TIME_CHUNK = 512


def _hy_inv_kernel(y_ref, bl_ref, bc_ref, z_ref, x_ref, d_ref, o_ref, *, ctx, seq):
    d = d_ref[...]

    def part(b_ref, y, t0, length):
        ck = min(TIME_CHUNK, length)
        for c0 in range(0, length, ck):
            rows = slice(t0 + c0, t0 + c0 + ck)
            conv = _dot(b_ref[c0:c0 + ck, :], y)
            o_ref[rows, :] = x_ref[rows, :] * (conv + d * z_ref[rows, :])

    part(bc_ref, y_ref[0:2 * ctx, :], 0, ctx)
    part(bl_ref, y_ref[2 * ctx:2 * (ctx + seq), :], ctx, seq)


def _hy_inv(y, b_lat, b_ctx, z, zblk, uxx, xblk, dskip, order, ctx, seq):
    b, rows, _ = y.shape
    tok = ctx + seq
    return pl.pallas_call(
        functools.partial(_hy_inv_kernel, ctx=ctx, seq=seq),
        grid=(b,),
        in_specs=[
            pl.BlockSpec((None, rows, MIX_B), lambda i: (i, 0, 0)),
            _resident(b_lat.shape, lambda i: (0, 0)),
            _resident(b_ctx.shape, lambda i: (0, 0)),
            pl.BlockSpec((None, tok, MIX_B), lambda i: (i, 0, zblk)),
            pl.BlockSpec((None, tok, MIX_B), lambda i: (i, 0, xblk)),
            pl.BlockSpec((None, 1, MIX_B), lambda i: (order, 0, 0)),
        ],
        out_specs=pl.BlockSpec((None, tok, MIX_B), lambda i: (i, 0, 0)),
        out_shape=jax.ShapeDtypeStruct((b, tok, MIX_B), F32),
        compiler_params=_cparams("parallel"),
        name="hyena_idft_gate",
    )(y, b_lat, b_ctx, z, uxx, dskip)


def _odd_in_kernel(x_ref, sh_ref, sc_ref, w_ref, qn_ref, kn_ref, q_ref, k_ref, v_ref):
    h = _normmod(x_ref[...], sh_ref[...], sc_ref[...]).astype(BF16)
    lane = lax.broadcasted_iota(jnp.int32, (x_ref.shape[0], LANES), 1)
    lo = lane < NA_HD
    half = D_MODEL // 2
    for part, gain_ref, scale, dst in ((0, qn_ref, NA_HD ** -0.5, q_ref), (1, kn_ref, 1.0, k_ref)):
        gain = gain_ref[...]
        for c in range(2):
            base = part * D_MODEL + c * half
            acc = _dot(h, w_ref[:, base:base + half])
            for g in range(half // LANES):
                a = acc[:, g * LANES:(g + 1) * LANES]
                sq = a * a
                s_lo = jnp.sum(jnp.where(lo, sq, 0.0), axis=-1, keepdims=True)
                s_hi = jnp.sum(jnp.where(lo, 0.0, sq), axis=-1, keepdims=True)
                inv = lax.rsqrt(jnp.where(lo, s_lo, s_hi) * (1.0 / NA_HD) + EPS)
                y = a * inv * gain
                if scale != 1.0:
                    y = y * scale
                dst[:, c * half + g * LANES:c * half + (g + 1) * LANES] = y.astype(BF16)
    for c in range(2):
        base = 2 * D_MODEL + c * half
        v_ref[:, c * half:(c + 1) * half] = _dot(h, w_ref[:, base:base + half]).astype(BF16)


def _odd_in_proj(x2, mod5, layer, tiles, w_qkv, qn2, kn2):
    m = x2.shape[0]
    out = jax.ShapeDtypeStruct((m, D_MODEL), BF16)
    return pl.pallas_call(
        _odd_in_kernel,
        grid=(tiles.n,),
        in_specs=[
            tiles.row_spec(D_MODEL), tiles.mod_spec(layer, 0), tiles.mod_spec(layer, 1),
            _resident((D_MODEL, 3 * D_MODEL), lambda i: (0, 0)),
            pl.BlockSpec((1, LANES), lambda i: (0, 0)), pl.BlockSpec((1, LANES), lambda i: (0, 0)),
        ],
        out_specs=[tiles.row_spec(D_MODEL)] * 3,
        out_shape=(out, out, out),
        compiler_params=_cparams("parallel"),
        name="odd_qkv_proj",
    )(x2, mod5, mod5, w_qkv, qn2, kn2)


def _pair_queries(q):
    lane = lax.broadcasted_iota(jnp.int32, q.shape, 1)
    zero = jnp.zeros_like(q)
    return jnp.concatenate([jnp.where(lane < NA_HD, q, zero), jnp.where(lane < NA_HD, zero, q)], axis=0)


def _unpair(res, n):
    lane = lax.broadcasted_iota(jnp.int32, (n, LANES), 1)
    return jnp.where(lane < NA_HD, res[0:n, :], res[n:2 * n, :])


def _natten_kernel(q_ref, k_ref, v_ref, t_ref, o_ref, *, ctx, rows_n, win_rows):
    kc = k_ref[0:ctx, :]
    vc = v_ref[0:ctx, :]

    s = _dot_nt(kc, _pair_queries(q_ref[0:ctx, :]))
    p = jnp.exp(s - jnp.max(s, axis=0, keepdims=True))
    p = p * (1.0 / jnp.sum(p, axis=0, keepdims=True))
    o_ref[0:ctx, :] = _unpair(_dot_tn(p.astype(BF16), vc), ctx).astype(BF16)

    nk = win_rows * GRID_W

    def body(r, carry):
        rs = jnp.clip(r - win_rows // 2, 0, rows_n - win_rows)
        q_off = pl.multiple_of(ctx + r * GRID_W, GRID_W)
        k_off = pl.multiple_of(ctx + rs * GRID_W, GRID_W)
        w = _pair_queries(q_ref[pl.ds(q_off, GRID_W), :])
        kb = k_ref[pl.ds(k_off, nk), :]
        vb = v_ref[pl.ds(k_off, nk), :]
        s_lat = _dot_nt(kb, w) + t_ref[rs - r + NA_WIN_ROWS - 1]
        s_ctx = _dot_nt(kc, w)
        m = jnp.maximum(jnp.max(s_lat, axis=0, keepdims=True), jnp.max(s_ctx, axis=0, keepdims=True))
        p_lat = jnp.exp(s_lat - m)
        p_ctx = jnp.exp(s_ctx - m)
        inv = 1.0 / (jnp.sum(p_lat, axis=0, keepdims=True) + jnp.sum(p_ctx, axis=0, keepdims=True))
        res = _dot_tn((p_lat * inv).astype(BF16), vb) + _dot_tn((p_ctx * inv).astype(BF16), vc)
        o_ref[pl.ds(q_off, GRID_W), :] = _unpair(res, GRID_W).astype(BF16)
        return carry

    lax.fori_loop(0, rows_n, body, 0)


def _natten(q, k, v, table, ctx, seq):
    b, tok, _ = q.shape
    rows_n = seq // GRID_W
    win_rows = min(NA_WIN_ROWS, rows_n)
    pairs = NA_HEADS // 2
    pair_spec = pl.BlockSpec((None, tok, LANES), lambda i, p: (i, 0, p))
    return pl.pallas_call(
        functools.partial(_natten_kernel, ctx=ctx, rows_n=rows_n, win_rows=win_rows),
        grid=(b, pairs),
        in_specs=[
            pair_spec, pair_spec, pair_spec,
            pl.BlockSpec((None,) + table.shape[1:], lambda i, p: (p, 0, 0, 0)),
        ],
        out_specs=pair_spec,
        out_shape=jax.ShapeDtypeStruct((b, tok, D_MODEL), BF16),
        compiler_params=_cparams("parallel", "arbitrary"),
        name="natten",
    )(q, k, v, table)


def _natten_bias_table(rpb, rows_n):
    wr = min(NA_WIN_ROWS, rows_n)
    n_start = 2 * NA_WIN_ROWS - 1 - (wr - 1)
    cidx = np.arange(GRID_W)
    cstart = np.clip(cidx - NA_WIN_COLS // 2, 0, GRID_W - NA_WIN_COLS)
    cmask = (cidx[None, :] >= cstart[:, None]) & (cidx[None, :] < cstart[:, None] + NA_WIN_COLS)
    dc = np.clip(cidx[None, :] - cidx[:, None] + NA_WIN_COLS - 1, 0, 2 * NA_WIN_COLS - 2)
    dr = np.arange(n_start)[:, None] + np.arange(wr)[None, :]
    tb = rpb[:, dr[:, :, None, None], dc.T[None, None, :, :]]
    tb = jnp.where(jnp.asarray(cmask.T)[None, None, None], tb, NEG_BIG)
    tb = tb.reshape(NA_HEADS // 2, 2, n_start, wr * GRID_W, GRID_W)
    tb = tb.transpose(0, 2, 3, 1, 4)
    return tb.reshape(NA_HEADS // 2, n_start, wr * GRID_W, 2 * GRID_W).astype(F32)


def _out_proj_kernel(*refs, widths):
    x_ref, g_ref = refs[0], refs[1]
    in_refs = refs[2:2 + len(widths)]
    w_ref, o_ref = refs[2 + len(widths)], refs[3 + len(widths)]
    acc = None
    off = 0
    for r, wd in zip(in_refs, widths):
        part = _dot(r[...].astype(BF16), w_ref[off:off + wd, :])
        acc = part if acc is None else acc + part
        off += wd
    o_ref[...] = x_ref[...] + g_ref[...] * acc


def _out_proj(x2, mod5, layer, tiles, parts, w):
    widths = tuple(p.shape[1] for p in parts)
    return pl.pallas_call(
        functools.partial(_out_proj_kernel, widths=widths),
        grid=(tiles.n,),
        in_specs=[tiles.row_spec(D_MODEL), tiles.mod_spec(layer, 2)]
        + [tiles.row_spec(wd) for wd in widths]
        + [_resident(w.shape, lambda i: (0, 0))],
        out_specs=tiles.row_spec(D_MODEL),
        out_shape=jax.ShapeDtypeStruct(x2.shape, F32),
        input_output_aliases={0: 0},
        compiler_params=_cparams("parallel"),
        name="mixer_out_proj",
    )(x2, mod5, *parts, w)


FF_CHUNK = 1024


def _mlp_kernel(x_ref, sh_ref, sc_ref, g_ref, w1_ref, w2_ref, o_ref):
    x = x_ref[...]
    h = _normmod(x, sh_ref[...], sc_ref[...]).astype(BF16)
    acc = None
    for c0 in range(0, D_FF, FF_CHUNK):
        a = jnp.maximum(_dot(h, w1_ref[:, c0:c0 + FF_CHUNK]), 0.0)
        part = _dot((a * a).astype(BF16), w2_ref[c0:c0 + FF_CHUNK, :])
        acc = part if acc is None else acc + part
    o_ref[...] = x + g_ref[...] * acc


def _mlp(x2, mod5, layer, tiles, w1, w2):
    return pl.pallas_call(
        _mlp_kernel,
        grid=(tiles.n,),
        in_specs=[
            tiles.row_spec(D_MODEL), tiles.mod_spec(layer, 3), tiles.mod_spec(layer, 4), tiles.mod_spec(layer, 5),
            _resident(w1.shape, lambda i: (0, 0)), _resident(w2.shape, lambda i: (0, 0)),
        ],
        out_specs=tiles.row_spec(D_MODEL),
        out_shape=jax.ShapeDtypeStruct(x2.shape, F32),
        input_output_aliases={0: 0},
        compiler_params=_cparams("parallel"),
        name="sq_relu_mlp",
    )(x2, mod5, mod5, mod5, w1, w2)


def _pad_cols(a, width):
    return jnp.pad(a, ((0, 0), (0, width - a.shape[1])))


def kernel(x, c, ctx, c_ctx, w_mod, b_mod, w_mlp_in, w_mlp_out, e_w_in, e_gate_b, e_hnorm, e_conv_w, e_conv_b,
           e_f_w1, e_f_b1, e_f_w2, e_f_b2, e_f_w3, e_f_freq, e_hy_d, e_w_out, o_w_qkv, o_qn, o_kn, o_rpb, o_w_out):
    batch, seq, d = x.shape
    n_ctx = ctx.shape[1]
    depth = w_mod.shape[0]
    tok = n_ctx + seq
    assert d == D_MODEL and seq % CHUNK == 0 and n_ctx == CHUNK and seq % GRID_W == 0
    tiles = _Tiles(batch, tok, n_ctx)

    n_samp = -(-(batch + 1) // 8) * 8
    cvec = jnp.concatenate([c, c_ctx[None, :], jnp.zeros((n_samp - batch - 1, d), F32)], axis=0)
    mod5 = _modulation(cvec, w_mod, b_mod)

    cos_t, sin_t = _rope_tables(n_ctx, seq)
    a_lat, b_lat = _dft_matrices(seq)
    a_ctx, b_ctx = _dft_matrices(n_ctx)

    xs = jnp.concatenate([ctx, x], axis=1).reshape(batch * tok, d)

    for l in range(depth):
        i = l // 2
        if l % 2 == 0:
            w_in = e_w_in[i]
            g0 = 4 * MIX_A
            w_pad = jnp.concatenate(
                [w_in[:, :g0], _pad_cols(w_in[:, g0:g0 + N_GATES], LANES), w_in[:, g0 + N_GATES:]], axis=1).astype(BF16)
            gate_b = _pad_cols(e_gate_b[i][None, :], LANES)
            q, k, v, sg, g, hy = _even_in_proj(xs, mod5, l, tiles, w_pad, gate_b, cos_t, sin_t)
            as3 = lambda a: a.reshape(batch, tok, a.shape[-1])
            gcol = as3(g)
            grow = gcol[:, :, :N_GATES].transpose(0, 2, 1)
            a_out = _mlstm(as3(q), as3(k), as3(v), as3(sg), gcol, grow, e_hnorm[i][None, :])

            uxx = _short_conv(as3(hy), e_conv_w[i], e_conv_b[i], n_ctx)
            pad2 = lambda a: jnp.pad(a, ((0, LANES - a.shape[0]), (0, LANES - a.shape[1])))
            w1p = pad2(e_f_w1[i]).astype(BF16)
            w2p = pad2(e_f_w2[i]).astype(BF16)
            w3p = jnp.pad(e_f_w3[i], ((0, LANES - HYENA_FFN), (0, 0))).astype(BF16)
            b1p = _pad_cols(e_f_b1[i][None, :], LANES)
            b2p = _pad_cols(e_f_b2[i][None, :], LANES)
            frp = _pad_cols(e_f_freq[i][None, :], LANES)
            g_lat = _filter_spectrum(a_lat, _hyena_filters(seq, w1p, b1p, w2p, b2p, frp, w3p))
            g_ctx = _filter_spectrum(a_ctx, _hyena_filters(n_ctx, w1p, b1p, w2p, b2p, frp, w3p))
            dskip = e_hy_d[i][:, None, :]
            z, zblk = uxx, 0
            for o in range(e_hy_d.shape[1]):
                y = _hy_fwd(z, zblk, a_lat, a_ctx, g_lat, g_ctx, o, n_ctx, seq)
                z = _hy_inv(y, b_lat, b_ctx, z, zblk, uxx, 1 + o, dskip, o, n_ctx, seq)
                zblk = 0
            xs = _out_proj(xs, mod5, l, tiles, [a_out.reshape(batch * tok, MIX_A), z.reshape(batch * tok, MIX_B)],
                           e_w_out[i].astype(BF16))
        else:
            rep = LANES // NA_HD
            qn2 = jnp.tile(o_qn[i], rep)[None, :]
            kn2 = jnp.tile(o_kn[i], rep)[None, :]
            q, k, v = _odd_in_proj(xs, mod5, l, tiles, o_w_qkv[i].astype(BF16), qn2, kn2)
            as3 = lambda a: a.reshape(batch, tok, D_MODEL)
            table = _natten_bias_table(o_rpb[i], seq // GRID_W)
            att = _natten(as3(q), as3(k), as3(v), table, n_ctx, seq)
            xs = _out_proj(xs, mod5, l, tiles, [att.reshape(batch * tok, D_MODEL)], o_w_out[i].astype(BF16))
        xs = _mlp(xs, mod5, l, tiles, w_mlp_in[l].astype(BF16), w_mlp_out[l].astype(BF16))

    return xs.reshape(batch, tok, d)[:, n_ctx:, :]
```

```python
import functools
import math

import numpy as np
import jax
import jax.numpy as jnp
from jax import lax
from jax.experimental import pallas as pl
from jax.experimental.pallas import tpu as pltpu

F32 = jnp.float32
BF16 = jnp.bfloat16

D_MODEL = 1024
D_FF = 4 * D_MODEL
EPS = 1e-6
ROPE_BASE = 10000.0
GRID_W = 64
MIX_A = D_MODEL // 2
MIX_B = D_MODEL - MIX_A
MLSTM_HEADS = 4
MLSTM_HD = MIX_A // MLSTM_HEADS
N_GATES = 4 * MLSTM_HEADS
HYENA_EMB = 33
HYENA_FFN = 64
HYENA_TARGET = 1e-2
HYENA_FAST = 0.3
HYENA_SLOW = 1.5
NA_HEADS = 16
NA_HD = D_MODEL // NA_HEADS
NA_WIN_ROWS = 8
NA_WIN_COLS = 16

LANES = 128
ROW_TILE = 256
CHUNK = 256
VMEM_LIMIT = 56 * 1024 * 1024
NEG_BIG = -1e30
LOG2E = math.log2(math.e)
NA_UNROLL = 4


def _cparams(*sem):
    return pltpu.CompilerParams(dimension_semantics=sem, vmem_limit_bytes=VMEM_LIMIT)


def _dot(a, b):
    return jnp.dot(a, b, preferred_element_type=F32)


def _dot_nt(a, b):
    return lax.dot_general(a, b, (((1,), (1,)), ((), ())), preferred_element_type=F32)


def _dot_tn(a, b):
    return lax.dot_general(a, b, (((0,), (0,)), ((), ())), preferred_element_type=F32)


def _resident(shape, index_map):
    return pl.BlockSpec(shape, index_map, pipeline_mode=pl.Buffered(1))


def _normmod(x, sh, sc):
    ms = jnp.mean(x * x, axis=-1, keepdims=True)
    return (x * lax.rsqrt(ms + EPS)) * (1.0 + sc) + sh


def _sigmoid(x):
    return 1.0 / (1.0 + jnp.exp(-x))


def _log_sigmoid(x):
    return jnp.minimum(x, 0.0) - jnp.log(1.0 + jnp.exp(-jnp.abs(x)))


def _mod_kernel(c_ref, w_ref, b_ref, o_ref):
    c = c_ref[...]
    s = (c * _sigmoid(c)).astype(BF16)
    o_ref[...] = _dot(s, w_ref[...].astype(BF16)) + b_ref[...]


def _modulation(cvec, w_mod, b_mod):
    depth, d, d6 = w_mod.shape
    ns = cvec.shape[0]
    nj = d6 // d
    out = pl.pallas_call(
        _mod_kernel,
        grid=(depth, nj),
        in_specs=[
            pl.BlockSpec((ns, d), lambda l, j: (0, 0)),
            pl.BlockSpec((None, d, d), lambda l, j: (l, 0, j)),
            pl.BlockSpec((None, 1, d), lambda l, j: (l, 0, j)),
        ],
        out_specs=pl.BlockSpec((None, ns, d), lambda l, j: (l, 0, j)),
        out_shape=jax.ShapeDtypeStruct((depth, ns, d6), F32),
        compiler_params=_cparams("arbitrary", "arbitrary"),
        name="modulation",
    )(cvec, w_mod, b_mod.reshape(depth, 1, d6))
    return out.reshape(depth, ns, nj, 1, d)


class _Tiles:
    def __init__(self, batch, tok, ctx):
        assert ctx == ROW_TILE and tok % ROW_TILE == 0
        self.batch, self.tok, self.ctx = batch, tok, ctx
        self.per_batch = tok // ROW_TILE
        self.n = batch * self.per_batch

    def sample(self, i):
        return jnp.where(i % self.per_batch == 0, self.batch, i // self.per_batch)

    def mod_spec(self, layer, which):
        return pl.BlockSpec((None, None, None, 1, D_MODEL),
                            lambda i: (layer, self.sample(i), which, 0, 0))

    def row_spec(self, width):
        return pl.BlockSpec((ROW_TILE, width), lambda i: (i, 0))

    def pos_spec(self, width):
        return pl.BlockSpec((ROW_TILE, width), lambda i: (i % self.per_batch, 0))


def _even_in_kernel(x_ref, sh_ref, sc_ref, w_ref, gb_ref, cos_ref, sin_ref,
                    q_ref, k_ref, v_ref, sg_ref, g_ref, hy_ref):
    h = _normmod(x_ref[...], sh_ref[...], sc_ref[...]).astype(BF16)
    cos = cos_ref[...]
    sin = sin_ref[...]
    lane = lax.broadcasted_iota(jnp.int32, cos.shape, 1)
    first = (lane % (MLSTM_HD // 2)) < (MLSTM_HD // 4)

    def rope(a):
        part = jnp.where(first, pltpu.roll(a, LANES - MLSTM_HD // 4, 1), pltpu.roll(a, MLSTM_HD // 4, 1))
        return a * cos + part * sin

    acc = _dot(h, w_ref[:, 0:MIX_A])
    for hd in range(MLSTM_HEADS):
        sl = slice(hd * MLSTM_HD, (hd + 1) * MLSTM_HD)
        q_ref[:, sl] = rope(acc[:, sl]).astype(BF16)
    acc = _dot(h, w_ref[:, MIX_A:2 * MIX_A]) * (MLSTM_HD ** -0.5)
    for hd in range(MLSTM_HEADS):
        sl = slice(hd * MLSTM_HD, (hd + 1) * MLSTM_HD)
        k_ref[:, sl] = rope(acc[:, sl]).astype(BF16)
    v_ref[...] = _dot(h, w_ref[:, 2 * MIX_A:3 * MIX_A]).astype(BF16)
    sg_ref[...] = _sigmoid(_dot(h, w_ref[:, 3 * MIX_A:4 * MIX_A])).astype(BF16)
    g0 = 4 * MIX_A
    g_ref[...] = _dot(h, w_ref[:, g0:g0 + LANES]) + gb_ref[...]
    h0 = g0 + LANES
    for j in range(3):
        hy_ref[:, j * MIX_B:(j + 1) * MIX_B] = _dot(h, w_ref[:, h0 + j * MIX_B:h0 + (j + 1) * MIX_B])


def _even_in_proj(x2, mod5, layer, tiles, w_pad, gate_b_pad, cos_t, sin_t):
    m = x2.shape[0]
    n_w = w_pad.shape[1]
    outs = (
        jax.ShapeDtypeStruct((m, MIX_A), BF16), jax.ShapeDtypeStruct((m, MIX_A), BF16),
        jax.ShapeDtypeStruct((m, MIX_A), BF16), jax.ShapeDtypeStruct((m, MIX_A), BF16),
        jax.ShapeDtypeStruct((m, LANES), F32), jax.ShapeDtypeStruct((m, 3 * MIX_B), F32),
    )
    return pl.pallas_call(
        _even_in_kernel,
        grid=(tiles.n,),
        in_specs=[
            tiles.row_spec(D_MODEL), tiles.mod_spec(layer, 0), tiles.mod_spec(layer, 1),
            _resident((D_MODEL, n_w), lambda i: (0, 0)),
            pl.BlockSpec((1, LANES), lambda i: (0, 0)),
            tiles.pos_spec(MLSTM_HD), tiles.pos_spec(MLSTM_HD),
        ],
        out_specs=[tiles.row_spec(MIX_A)] * 4 + [tiles.row_spec(LANES), tiles.row_spec(3 * MIX_B)],
        out_shape=outs,
        compiler_params=_cparams("parallel"),
        name="even_in_proj",
    )(x2, mod5, mod5, w_pad, gate_b_pad, cos_t, sin_t)


def _rope_tables(ctx, seq):
    half = MLSTM_HD // 2
    nf = half // 2
    inv = ROPE_BASE ** (-np.arange(nf, dtype=np.float64) / nf)
    t = np.arange(seq)
    rows, cols = t // GRID_W, t % GRID_W

    def one(pos):
        ang = pos[:, None].astype(np.float64) * inv[None, :]
        c = np.concatenate([np.cos(ang), np.cos(ang)], axis=-1)
        s = np.concatenate([-np.sin(ang), np.sin(ang)], axis=-1)
        return c, s

    cr, sr = one(rows)
    cc, sc = one(cols)
    cos = np.concatenate([cr, cc], axis=-1)
    sin = np.concatenate([sr, sc], axis=-1)
    cos = np.concatenate([np.ones((ctx, MLSTM_HD)), cos], axis=0)
    sin = np.concatenate([np.zeros((ctx, MLSTM_HD)), sin], axis=0)
    return jnp.asarray(cos, F32), jnp.asarray(sin, F32)


def _mlstm_chunk(qc, kc, vc, i_col, f_col, i_row, f_row, state, lower):
    ct, n, m = state
    t = qc.shape[0]
    r = lax.broadcasted_iota(jnp.int32, (t, t), 0)
    c = lax.broadcasted_iota(jnp.int32, (t, t), 1)
    incl = (c <= r) if lower else (c >= r)
    incl_t = (r <= c) if lower else (r >= c)
    lf_col = _log_sigmoid(f_col)
    lf_row = _log_sigmoid(f_row)
    b_col = jnp.sum(jnp.where(incl, lf_row, 0.0), axis=1, keepdims=True)
    b_row = jnp.sum(jnp.where(incl_t, lf_col, 0.0), axis=0, keepdims=True)
    a_row = i_row - b_row
    a_col = i_col - b_col
    dmat = jnp.where(incl, b_col + a_row, -jnp.inf)
    inter = b_col + m
    m_t = jnp.maximum(inter, jnp.max(dmat, axis=1, keepdims=True))
    w = jnp.exp(dmat - m_t)
    sc = jnp.exp(inter - m_t)
    qkw = _dot_nt(qc, kc) * w
    num = sc * _dot(qc, ct.astype(BF16)) + _dot(qkw.astype(BF16), vc)
    den = sc * jnp.sum(qc.astype(F32) * n, axis=1, keepdims=True) + jnp.sum(qkw, axis=1, keepdims=True)
    h = num / jnp.maximum(jnp.abs(den), jnp.exp(-m_t))
    bl = b_col[t - 1:t, :] if lower else b_col[0:1, :]
    m_new = bl + jnp.maximum(m, jnp.max(a_col, axis=0, keepdims=True))
    decay = jnp.exp(bl + m - m_new)
    wk = jnp.exp(bl + a_col - m_new)
    kw = kc.astype(F32) * wk
    ct_new = decay * ct + _dot_tn(kw.astype(BF16), vc)
    n_new = decay * n + jnp.sum(kw, axis=0, keepdims=True)
    return h, (ct_new, n_new, m_new)


def _mlstm_kernel(q_ref, k_ref, v_ref, sg_ref, gc_ref, gr_ref, hn_ref, o_ref, hf_ref, hb_ref, *, n_chunks):
    head = pl.program_id(1)
    lane = lax.broadcasted_iota(jnp.int32, (CHUNK, LANES), 1)
    sub = lax.broadcasted_iota(jnp.int32, (N_GATES, CHUNK), 0)

    def gates(off, kind):
        idx = kind * MLSTM_HEADS + head
        col = jnp.sum(jnp.where(lane == idx, gc_ref[pl.ds(off, CHUNK), :], 0.0), axis=1, keepdims=True)
        row = jnp.sum(jnp.where(sub == idx, gr_ref[:, pl.ds(off, CHUNK)], 0.0), axis=0, keepdims=True)
        return col, row

    def run(off, state, lower, dst_ref):
        i_col, i_row = gates(off, 0 if lower else 2)
        f_col, f_row = gates(off, 1 if lower else 3)
        rows = pl.ds(off, CHUNK)
        h, state = _mlstm_chunk(q_ref[rows, :], k_ref[rows, :], v_ref[rows, :],
                                i_col, f_col, i_row, f_row, state, lower)
        dst_ref[rows, :] = h
        return state

    def body(s, carry):
        sf, sb = carry
        off_f = pl.multiple_of(s * CHUNK, CHUNK)
        off_b = pl.multiple_of(jnp.where(s == 0, 0, n_chunks - s) * CHUNK, CHUNK)
        sf = run(off_f, sf, True, hf_ref)
        sb = run(off_b, sb, False, hb_ref)
        return sf, sb

    zero = (jnp.zeros((MLSTM_HD, MLSTM_HD), F32), jnp.zeros((1, MLSTM_HD), F32), jnp.zeros((1, 1), F32))
    lax.fori_loop(0, n_chunks, body, (zero, zero))

    hm = hf_ref[...] + hb_ref[...]
    ms = jnp.mean(hm * hm, axis=-1, keepdims=True)
    y = hm * lax.rsqrt(ms + EPS) * hn_ref[...]
    o_ref[...] = (y * sg_ref[...].astype(F32)).astype(BF16)


def _mlstm(q, k, v, sg, gcol, grow, hnorm):
    b, tok, _ = q.shape
    head_spec = pl.BlockSpec((None, tok, MLSTM_HD), lambda i, h: (i, 0, h))
    return pl.pallas_call(
        functools.partial(_mlstm_kernel, n_chunks=tok // CHUNK),
        grid=(b, MLSTM_HEADS),
        in_specs=[
            head_spec, head_spec, head_spec, head_spec,
            pl.BlockSpec((None, tok, LANES), lambda i, h: (i, 0, 0)),
            pl.BlockSpec((None, N_GATES, tok), lambda i, h: (i, 0, 0)),
            pl.BlockSpec((1, MLSTM_HD), lambda i, h: (0, h)),
        ],
        out_specs=head_spec,
        out_shape=jax.ShapeDtypeStruct((b, tok, MIX_A), BF16),
        scratch_shapes=[pltpu.VMEM((tok, MLSTM_HD), F32), pltpu.VMEM((tok, MLSTM_HD), F32)],
        compiler_params=_cparams("parallel", "arbitrary"),
        name="mlstm",
    )(q, k, v, sg, gcol, grow, hnorm)


def _short_conv_kernel(x_ref, w_ref, b_ref, o_ref, *, ctx):
    x = x_ref[...]
    tok = x.shape[0]
    t = lax.broadcasted_iota(jnp.int32, x.shape, 0)
    prev = jnp.where((t == 0) | (t == ctx), 0.0, pltpu.roll(x, 1, 0))
    nxt = jnp.where((t == ctx - 1) | (t == tok - 1), 0.0, pltpu.roll(x, tok - 1, 0))
    o_ref[...] = b_ref[...] + prev * w_ref[0:1, :] + x * w_ref[1:2, :] + nxt * w_ref[2:3, :]


def _short_conv(hy, conv_w, conv_b, ctx):
    b, tok, ch = hy.shape
    nblk = ch // MIX_B
    return pl.pallas_call(
        functools.partial(_short_conv_kernel, ctx=ctx),
        grid=(b, nblk),
        in_specs=[
            pl.BlockSpec((None, tok, MIX_B), lambda i, j: (i, 0, j)),
            pl.BlockSpec((3, MIX_B), lambda i, j: (0, j)),
            pl.BlockSpec((1, MIX_B), lambda i, j: (0, j)),
        ],
        out_specs=pl.BlockSpec((None, tok, MIX_B), lambda i, j: (i, 0, j)),
        out_shape=jax.ShapeDtypeStruct((b, tok, ch), F32),
        compiler_params=_cparams("parallel", "arbitrary"),
        name="hyena_short_conv",
    )(hy, conv_w, conv_b.reshape(1, ch))


def _filter_kernel(emb_ref, w1_ref, b1_ref, w2_ref, b2_ref, fr_ref, w3_ref, dec_ref, o_ref):
    fr = fr_ref[...]
    h = jnp.sin(fr * (_dot(emb_ref[...].astype(BF16), w1_ref[...]) + b1_ref[...]))
    h = jnp.sin(fr * (_dot(h.astype(BF16), w2_ref[...]) + b2_ref[...]))
    h = _dot(h.astype(BF16), w3_ref[...]) * dec_ref[...]
    o_ref[...] = h * lax.rsqrt(jnp.sum(h * h, axis=0, keepdims=True) + EPS)


def _hyena_filters(length, w1p, b1p, w2p, b2p, frp, w3p):
    bands = (HYENA_EMB - 1) // 2
    t = np.linspace(0.0, 1.0, length)[:, None]
    wpos = 2.0 * math.pi * np.arange(length) / length
    fr = np.linspace(1e-4, bands - 1, bands)
    ang = wpos[:, None] * fr[None, :]
    emb = np.concatenate([t, np.cos(ang), -np.sin(ang)], axis=-1)
    emb = np.pad(emb, ((0, 0), (0, LANES - HYENA_EMB)))
    deltas = np.abs(np.linspace(math.log(HYENA_TARGET) / HYENA_SLOW, math.log(HYENA_TARGET) / HYENA_FAST, MIX_B))
    decay = np.exp(-t * deltas[None, :])
    ngrp = w3p.shape[1] // MIX_B
    const = lambda g: (0, 0)
    return pl.pallas_call(
        _filter_kernel,
        grid=(ngrp,),
        in_specs=[
            pl.BlockSpec((length, LANES), const), pl.BlockSpec((LANES, LANES), const),
            pl.BlockSpec((1, LANES), const), pl.BlockSpec((LANES, LANES), const),
            pl.BlockSpec((1, LANES), const), pl.BlockSpec((1, LANES), const),
            pl.BlockSpec((LANES, MIX_B), lambda g: (0, g)),
            pl.BlockSpec((length, MIX_B), const),
        ],
        out_specs=pl.BlockSpec((length, MIX_B), lambda g: (0, g)),
        out_shape=jax.ShapeDtypeStruct((length, ngrp * MIX_B), F32),
        compiler_params=_cparams("arbitrary"),
        name="hyena_filters",
    )(jnp.asarray(emb, F32), w1p, b1p, w2p, b2p, frp, w3p, jnp.asarray(decay, F32))


def _dft_matrices(length):
    period = 2 * length
    k = jnp.arange(length, dtype=jnp.int32)[:, None]
    t = jnp.arange(length, dtype=jnp.int32)[None, :]
    ang = ((k * t) & (period - 1)).astype(F32) * (2.0 * math.pi / period)
    c = jnp.cos(ang)
    s = jnp.sin(ang)
    alt_t = jnp.where(t % 2 == 0, 1.0, -1.0).astype(F32)
    fwd = jnp.concatenate([c, jnp.where(k == 0, alt_t, -s)], axis=0)
    coef = jnp.where(t == 0, 1.0, 2.0).astype(F32) / period
    alt_k = jnp.where(k % 2 == 0, 1.0, -1.0).astype(F32)
    inv = jnp.concatenate([c * coef, jnp.where(t == 0, alt_k / period, -s * coef)], axis=1)
    return fwd.astype(BF16), inv.astype(BF16)


def _spectrum_kernel(a_ref, hf_ref, hb_ref, o_ref):
    half = pl.program_id(0)
    hf = hf_ref[...]
    hb = hb_ref[...]
    comb = jnp.where(half == 0, hf + hb, hf - hb).astype(BF16)
    o_ref[...] = _dot(a_ref[...], comb)

    @pl.when(half == 1)
    def _():
        t = lax.broadcasted_iota(jnp.int32, hf.shape, 0)
        o_ref[0:1, :] = jnp.sum(jnp.where(t % 2 == 0, hf + hb, -(hf + hb)), axis=0, keepdims=True)


def _filter_spectrum(fwd, filt):
    period, length = fwd.shape
    orders = filt.shape[1] // (2 * MIX_B)
    return pl.pallas_call(
        _spectrum_kernel,
        grid=(2, orders),
        in_specs=[
            pl.BlockSpec((length, length), lambda hlf, o: (hlf, 0)),
            pl.BlockSpec((length, MIX_B), lambda hlf, o: (0, 2 * o)),
            pl.BlockSpec((length, MIX_B), lambda hlf, o: (0, 2 * o + 1)),
        ],
        out_specs=pl.BlockSpec((length, MIX_B), lambda hlf, o: (hlf, o)),
        out_shape=jax.ShapeDtypeStruct((period, orders * MIX_B), F32),
        compiler_params=_cparams("arbitrary", "arbitrary"),
        name="hyena_filter_spectrum",
    )(fwd, filt, filt)


FREQ_CHUNK = 512


def _spectral_product(a_ref, g_ref, z, y_ref, y_off, length):
    ck = min(FREQ_CHUNK, length)
    for c0 in range(0, length, ck):
        zr = _dot(a_ref[c0:c0 + ck, :], z)
        zi = _dot(a_ref[length + c0:length + c0 + ck, :], z)
        gr = g_ref[c0:c0 + ck, :]
        gi = g_ref[length + c0:length + c0 + ck, :]
        yr = zr * gr - zi * gi
        yi = zr * gi + zi * gr
        if c0 == 0:
            first = lax.broadcasted_iota(jnp.int32, yr.shape, 0) == 0
            yr = jnp.where(first, zr * gr, yr)
            yi = jnp.where(first, zi * gi, yi)
        y_ref[y_off + c0:y_off + c0 + ck, :] = yr.astype(BF16)
        y_ref[y_off + length + c0:y_off + length + c0 + ck, :] = yi.astype(BF16)


def _hy_fwd_kernel(z_ref, al_ref, ac_ref, gl_ref, gc_ref, y_ref, *, ctx, seq):
    _spectral_product(ac_ref, gc_ref, z_ref[0:ctx, :].astype(BF16), y_ref, 0, ctx)
    _spectral_product(al_ref, gl_ref, z_ref[ctx:ctx + seq, :].astype(BF16), y_ref, 2 * ctx, seq)


def _hy_fwd(z, zblk, a_lat, a_ctx, g_lat, g_ctx, order, ctx, seq):
    b, tok, _ = z.shape
    rows = 2 * tok
    return pl.pallas_call(
        functools.partial(_hy_fwd_kernel, ctx=ctx, seq=seq),
        grid=(b,),
        in_specs=[
            pl.BlockSpec((None, tok, MIX_B), lambda i: (i, 0, zblk)),
            _resident(a_lat.shape, lambda i: (0, 0)),
            _resident(a_ctx.shape, lambda i: (0, 0)),
            _resident((2 * seq, MIX_B), lambda i: (0, order)),
            _resident((2 * ctx, MIX_B), lambda i: (0, order)),
        ],
        out_specs=pl.BlockSpec((None, rows, MIX_B), lambda i: (i, 0, 0)),
        out_shape=jax.ShapeDtypeStruct((b, rows, MIX_B), BF16),
        compiler_params=_cparams("parallel"),
        name="hyena_dft_mul",
    )(z, a_lat, a_ctx, g_lat, g_ctx)


TIME_CHUNK = 512


def _hy_inv_kernel(y_ref, bl_ref, bc_ref, z_ref, x_ref, d_ref, o_ref, *, ctx, seq):
    d = d_ref[...]

    def part(b_ref, y, t0, length):
        ck = min(TIME_CHUNK, length)
        for c0 in range(0, length, ck):
            rows = slice(t0 + c0, t0 + c0 + ck)
            conv = _dot(b_ref[c0:c0 + ck, :], y)
            o_ref[rows, :] = x_ref[rows, :] * (conv + d * z_ref[rows, :])

    part(bc_ref, y_ref[0:2 * ctx, :], 0, ctx)
    part(bl_ref, y_ref[2 * ctx:2 * (ctx + seq), :], ctx, seq)


def _hy_inv(y, b_lat, b_ctx, z, zblk, uxx, xblk, dskip, order, ctx, seq):
    b, rows, _ = y.shape
    tok = ctx + seq
    return pl.pallas_call(
        functools.partial(_hy_inv_kernel, ctx=ctx, seq=seq),
        grid=(b,),
        in_specs=[
            pl.BlockSpec((None, rows, MIX_B), lambda i: (i, 0, 0)),
            _resident(b_lat.shape, lambda i: (0, 0)),
            _resident(b_ctx.shape, lambda i: (0, 0)),
            pl.BlockSpec((None, tok, MIX_B), lambda i: (i, 0, zblk)),
            pl.BlockSpec((None, tok, MIX_B), lambda i: (i, 0, xblk)),
            pl.BlockSpec((None, 1, MIX_B), lambda i: (order, 0, 0)),
        ],
        out_specs=pl.BlockSpec((None, tok, MIX_B), lambda i: (i, 0, 0)),
        out_shape=jax.ShapeDtypeStruct((b, tok, MIX_B), F32),
        compiler_params=_cparams("parallel"),
        name="hyena_idft_gate",
    )(y, b_lat, b_ctx, z, uxx, dskip)


def _odd_in_kernel(x_ref, sh_ref, sc_ref, w_ref, qn_ref, kn_ref, q_ref, k_ref, v_ref):
    h = _normmod(x_ref[...], sh_ref[...], sc_ref[...]).astype(BF16)
    lane = lax.broadcasted_iota(jnp.int32, (x_ref.shape[0], LANES), 1)
    lo = lane < NA_HD
    half = D_MODEL // 2
    for part, gain_ref, scale, dst in ((0, qn_ref, NA_HD ** -0.5 * LOG2E, q_ref), (1, kn_ref, 1.0, k_ref)):
        gain = gain_ref[...]
        for c in range(2):
            base = part * D_MODEL + c * half
            acc = _dot(h, w_ref[:, base:base + half])
            for g in range(half // LANES):
                a = acc[:, g * LANES:(g + 1) * LANES]
                sq = a * a
                s_lo = jnp.sum(jnp.where(lo, sq, 0.0), axis=-1, keepdims=True)
                s_hi = jnp.sum(jnp.where(lo, 0.0, sq), axis=-1, keepdims=True)
                inv = lax.rsqrt(jnp.where(lo, s_lo, s_hi) * (1.0 / NA_HD) + EPS)
                y = a * inv * gain
                if scale != 1.0:
                    y = y * scale
                dst[:, c * half + g * LANES:c * half + (g + 1) * LANES] = y.astype(BF16)
    for c in range(2):
        base = 2 * D_MODEL + c * half
        v_ref[:, c * half:(c + 1) * half] = _dot(h, w_ref[:, base:base + half]).astype(BF16)


def _odd_in_proj(x2, mod5, layer, tiles, w_qkv, qn2, kn2):
    m = x2.shape[0]
    out = jax.ShapeDtypeStruct((m, D_MODEL), BF16)
    return pl.pallas_call(
        _odd_in_kernel,
        grid=(tiles.n,),
        in_specs=[
            tiles.row_spec(D_MODEL), tiles.mod_spec(layer, 0), tiles.mod_spec(layer, 1),
            _resident((D_MODEL, 3 * D_MODEL), lambda i: (0, 0)),
            pl.BlockSpec((1, LANES), lambda i: (0, 0)), pl.BlockSpec((1, LANES), lambda i: (0, 0)),
        ],
        out_specs=[tiles.row_spec(D_MODEL)] * 3,
        out_shape=(out, out, out),
        compiler_params=_cparams("parallel"),
        name="odd_qkv_proj",
    )(x2, mod5, mod5, w_qkv, qn2, kn2)


def _pair_queries(q):
    lane = lax.broadcasted_iota(jnp.int32, q.shape, 1)
    zero = jnp.zeros_like(q)
    return jnp.concatenate([jnp.where(lane < NA_HD, q, zero), jnp.where(lane < NA_HD, zero, q)], axis=0)


def _unpair(res, n):
    lane = lax.broadcasted_iota(jnp.int32, (n, LANES), 1)
    return jnp.where(lane < NA_HD, res[0:n, :], res[n:2 * n, :])


def _natten_kernel(q_ref, k_ref, v_ref, t_ref, o_ref, va_ref, *, ctx, rows_n, win_rows):
    tok = v_ref.shape[0]
    va_ref[:, 0:LANES] = v_ref[...]
    va_ref[:, LANES:2 * LANES] = jnp.ones((tok, LANES), BF16)
    kc = k_ref[0:ctx, :]
    vc = va_ref[0:ctx, :]

    def softmax_pv(scores, values):
        m = functools.reduce(jnp.maximum, [jnp.max(s, axis=1, keepdims=True) for s in scores])
        res = None
        for s, vblk in zip(scores, values):
            part = _dot(jnp.exp2(s - m).astype(BF16), vblk)
            res = part if res is None else res + part
        return res[:, 0:LANES] * (1.0 / res[:, LANES:2 * LANES])

    w = _pair_queries(q_ref[0:ctx, :])
    o_ref[0:ctx, :] = _unpair(softmax_pv([_dot_nt(w, kc)], [vc]), ctx).astype(BF16)

    nk = win_rows * GRID_W

    def body(r, carry):
        rs = jnp.clip(r - win_rows // 2, 0, rows_n - win_rows)
        dr0 = rs - r + NA_WIN_ROWS - 1
        q_off = pl.multiple_of(ctx + r * GRID_W, GRID_W)
        k_off = pl.multiple_of(ctx + rs * GRID_W, GRID_W)
        w = _pair_queries(q_ref[pl.ds(q_off, GRID_W), :])
        bias = t_ref[dr0 & 1, :, pl.ds(pl.multiple_of((dr0 >> 1) * LANES, LANES), nk)]
        s_lat = _dot_nt(w, k_ref[pl.ds(k_off, nk), :]) + bias
        s_ctx = _dot_nt(w, kc)
        out = softmax_pv([s_lat, s_ctx], [va_ref[pl.ds(k_off, nk), :], vc])
        o_ref[pl.ds(q_off, GRID_W), :] = _unpair(out, GRID_W).astype(BF16)
        return carry

    lax.fori_loop(0, rows_n, body, 0, unroll=NA_UNROLL)


def _natten(q, k, v, table, ctx, seq):
    b, tok, _ = q.shape
    rows_n = seq // GRID_W
    assert rows_n >= NA_WIN_ROWS and rows_n % NA_UNROLL == 0
    pairs = NA_HEADS // 2
    pair_spec = pl.BlockSpec((None, tok, LANES), lambda i, p: (i, 0, p))
    return pl.pallas_call(
        functools.partial(_natten_kernel, ctx=ctx, rows_n=rows_n, win_rows=NA_WIN_ROWS),
        grid=(b, pairs),
        in_specs=[
            pair_spec, pair_spec, pair_spec,
            pl.BlockSpec((None,) + table.shape[1:], lambda i, p: (p, 0, 0, 0)),
        ],
        out_specs=pair_spec,
        out_shape=jax.ShapeDtypeStruct((b, tok, D_MODEL), BF16),
        scratch_shapes=[pltpu.VMEM((tok, 2 * LANES), BF16)],
        compiler_params=_cparams("parallel", "arbitrary"),
        name="natten",
    )(q, k, v, table)


def _natten_bias_table(rpb):
    n_dr = 2 * NA_WIN_ROWS - 1
    n_dc = 2 * NA_WIN_COLS - 1
    cidx = np.arange(GRID_W)
    cstart = np.clip(cidx - NA_WIN_COLS // 2, 0, GRID_W - NA_WIN_COLS)
    cmask = (cidx[None, :] >= cstart[:, None]) & (cidx[None, :] < cstart[:, None] + NA_WIN_COLS)
    dc = np.clip(cidx[None, :] - cidx[:, None] + NA_WIN_COLS - 1, 0, n_dc - 1)
    onehot = ((dc[None] == np.arange(n_dc)[:, None, None]) & cmask[None]).astype(np.float32)
    tb = jnp.einsum("hrc,cqk->hrqk", rpb.astype(F32), jnp.asarray(onehot), precision=lax.Precision.HIGHEST)
    tb = tb * LOG2E + jnp.asarray(np.where(cmask, 0.0, NEG_BIG), F32)
    tb = tb.reshape(NA_HEADS // 2, 2, n_dr, GRID_W, GRID_W).transpose(0, 1, 3, 2, 4)
    tb = tb.reshape(NA_HEADS // 2, 2 * GRID_W, n_dr * GRID_W)
    width = (n_dr + 1) * GRID_W
    tb = jnp.pad(tb, ((0, 0), (0, 0), (0, width + GRID_W - n_dr * GRID_W)))
    return jnp.stack([tb[:, :, 0:width], tb[:, :, GRID_W:GRID_W + width]], axis=1)


FF_CHUNK = 1024


def _mix_mlp_kernel(*refs, widths, per_batch, latent_only):
    x_ref, g1_ref, sh_ref, sc_ref, g2_ref = refs[:5]
    in_refs = refs[5:5 + len(widths)]
    wo_ref, w1_ref, w2_ref, o_ref = refs[5 + len(widths):]

    def compute():
        mix = None
        off = 0
        for r, wd in zip(in_refs, widths):
            part = _dot(r[...].astype(BF16), wo_ref[off:off + wd, :])
            mix = part if mix is None else mix + part
            off += wd
        x = x_ref[...] + g1_ref[...] * mix
        h = _normmod(x, sh_ref[...], sc_ref[...]).astype(BF16)
        acc = None
        for c0 in range(0, D_FF, FF_CHUNK):
            a = jnp.maximum(_dot(h, w1_ref[:, c0:c0 + FF_CHUNK]), 0.0)
            part = _dot((a * a).astype(BF16), w2_ref[c0:c0 + FF_CHUNK, :])
            acc = part if acc is None else acc + part
        o_ref[...] = x + g2_ref[...] * acc

    if latent_only:
        pl.when(pl.program_id(0) % per_batch != 0)(compute)
    else:
        compute()


def _mix_mlp(x2, mod5, layer, tiles, parts, w_out, w1, w2, latent_only):
    widths = tuple(p.shape[1] for p in parts)
    pb = tiles.per_batch
    if latent_only:
        out_rows = tiles.batch * (tiles.tok - tiles.ctx)
        out_spec = pl.BlockSpec((ROW_TILE, D_MODEL), lambda i: (i - i // pb - jnp.where(i % pb == 0, 0, 1), 0))
    else:
        out_rows = x2.shape[0]
        out_spec = tiles.row_spec(D_MODEL)
    return pl.pallas_call(
        functools.partial(_mix_mlp_kernel, widths=widths, per_batch=pb, latent_only=latent_only),
        grid=(tiles.n,),
        in_specs=[tiles.row_spec(D_MODEL), tiles.mod_spec(layer, 2),
                  tiles.mod_spec(layer, 3), tiles.mod_spec(layer, 4), tiles.mod_spec(layer, 5)]
        + [tiles.row_spec(wd) for wd in widths]
        + [_resident(w_out.shape, lambda i: (0, 0)), _resident(w1.shape, lambda i: (0, 0)),
           _resident(w2.shape, lambda i: (0, 0))],
        out_specs=out_spec,
        out_shape=jax.ShapeDtypeStruct((out_rows, D_MODEL), F32),
        input_output_aliases={} if latent_only else {0: 0},
        compiler_params=_cparams("arbitrary" if latent_only else "parallel"),
        name="mix_out_mlp",
    )(x2, mod5, mod5, mod5, mod5, *parts, w_out, w1, w2)


def _pad_cols(a, width):
    return jnp.pad(a, ((0, 0), (0, width - a.shape[1])))


def kernel(x, c, ctx, c_ctx, w_mod, b_mod, w_mlp_in, w_mlp_out, e_w_in, e_gate_b, e_hnorm, e_conv_w, e_conv_b,
           e_f_w1, e_f_b1, e_f_w2, e_f_b2, e_f_w3, e_f_freq, e_hy_d, e_w_out, o_w_qkv, o_qn, o_kn, o_rpb, o_w_out):
    batch, seq, d = x.shape
    n_ctx = ctx.shape[1]
    depth = w_mod.shape[0]
    tok = n_ctx + seq
    assert d == D_MODEL and seq % CHUNK == 0 and n_ctx == CHUNK and seq % GRID_W == 0
    tiles = _Tiles(batch, tok, n_ctx)

    n_samp = -(-(batch + 1) // 8) * 8
    cvec = jnp.concatenate([c, c_ctx[None, :], jnp.zeros((n_samp - batch - 1, d), F32)], axis=0)
    mod5 = _modulation(cvec, w_mod, b_mod)

    cos_t, sin_t = _rope_tables(n_ctx, seq)
    a_lat, b_lat = _dft_matrices(seq)
    a_ctx, b_ctx = _dft_matrices(n_ctx)

    xs = jnp.concatenate([ctx, x], axis=1).reshape(batch * tok, d)

    for l in range(depth):
        i = l // 2
        if l % 2 == 0:
            w_in = e_w_in[i]
            g0 = 4 * MIX_A
            w_pad = jnp.concatenate(
                [w_in[:, :g0], _pad_cols(w_in[:, g0:g0 + N_GATES], LANES), w_in[:, g0 + N_GATES:]], axis=1).astype(BF16)
            gate_b = _pad_cols(e_gate_b[i][None, :], LANES)
            q, k, v, sg, g, hy = _even_in_proj(xs, mod5, l, tiles, w_pad, gate_b, cos_t, sin_t)
            as3 = lambda a: a.reshape(batch, tok, a.shape[-1])
            gcol = as3(g)
            grow = gcol[:, :, :N_GATES].transpose(0, 2, 1)
            a_out = _mlstm(as3(q), as3(k), as3(v), as3(sg), gcol, grow, e_hnorm[i][None, :])

            uxx = _short_conv(as3(hy), e_conv_w[i], e_conv_b[i], n_ctx)
            pad2 = lambda a: jnp.pad(a, ((0, LANES - a.shape[0]), (0, LANES - a.shape[1])))
            w1p = pad2(e_f_w1[i]).astype(BF16)
            w2p = pad2(e_f_w2[i]).astype(BF16)
            w3p = jnp.pad(e_f_w3[i], ((0, LANES - HYENA_FFN), (0, 0))).astype(BF16)
            b1p = _pad_cols(e_f_b1[i][None, :], LANES)
            b2p = _pad_cols(e_f_b2[i][None, :], LANES)
            frp = _pad_cols(e_f_freq[i][None, :], LANES)
            g_lat = _filter_spectrum(a_lat, _hyena_filters(seq, w1p, b1p, w2p, b2p, frp, w3p))
            g_ctx = _filter_spectrum(a_ctx, _hyena_filters(n_ctx, w1p, b1p, w2p, b2p, frp, w3p))
            dskip = e_hy_d[i][:, None, :]
            z, zblk = uxx, 0
            for o in range(e_hy_d.shape[1]):
                y = _hy_fwd(z, zblk, a_lat, a_ctx, g_lat, g_ctx, o, n_ctx, seq)
                z = _hy_inv(y, b_lat, b_ctx, z, zblk, uxx, 1 + o, dskip, o, n_ctx, seq)
                zblk = 0
            parts = [a_out.reshape(batch * tok, MIX_A), z.reshape(batch * tok, MIX_B)]
            w_out = e_w_out[i].astype(BF16)
        else:
            rep = LANES // NA_HD
            qn2 = jnp.tile(o_qn[i], rep)[None, :]
            kn2 = jnp.tile(o_kn[i], rep)[None, :]
            q, k, v = _odd_in_proj(xs, mod5, l, tiles, o_w_qkv[i].astype(BF16), qn2, kn2)
            as3 = lambda a: a.reshape(batch, tok, D_MODEL)
            table = _natten_bias_table(o_rpb[i])
            att = _natten(as3(q), as3(k), as3(v), table, n_ctx, seq)
            parts = [att.reshape(batch * tok, D_MODEL)]
            w_out = o_w_out[i].astype(BF16)
        xs = _mix_mlp(xs, mod5, l, tiles, parts, w_out, w_mlp_in[l].astype(BF16), w_mlp_out[l].astype(BF16),
                      latent_only=(l == depth - 1))

    return xs.reshape(batch, seq, d)
```

```python
import functools
import math

import numpy as np
import jax
import jax.numpy as jnp
from jax import lax
from jax.experimental import pallas as pl
from jax.experimental.pallas import tpu as pltpu

F32 = jnp.float32
BF16 = jnp.bfloat16

D_MODEL = 1024
D_FF = 4 * D_MODEL
EPS = 1e-6
ROPE_BASE = 10000.0
GRID_W = 64
MIX_A = D_MODEL // 2
MIX_B = D_MODEL - MIX_A
MLSTM_HEADS = 4
MLSTM_HD = MIX_A // MLSTM_HEADS
N_GATES = 4 * MLSTM_HEADS
HYENA_EMB = 33
HYENA_FFN = 64
HYENA_TARGET = 1e-2
HYENA_FAST = 0.3
HYENA_SLOW = 1.5
NA_HEADS = 16
NA_HD = D_MODEL // NA_HEADS
NA_WIN_ROWS = 8
NA_WIN_COLS = 16

LANES = 128
ROW_TILE = 256
CHUNK = 256
VMEM_LIMIT = 56 * 1024 * 1024
NEG_BIG = -1e30
LOG2E = math.log2(math.e)
DFT_SPLIT = 64
NA_UNROLL = 16


def _cparams(*sem):
    return pltpu.CompilerParams(dimension_semantics=sem, vmem_limit_bytes=VMEM_LIMIT)


def _dot(a, b):
    return jnp.dot(a, b, preferred_element_type=F32)


def _dot_nt(a, b):
    return lax.dot_general(a, b, (((1,), (1,)), ((), ())), preferred_element_type=F32)


def _dot_tn(a, b):
    return lax.dot_general(a, b, (((0,), (0,)), ((), ())), preferred_element_type=F32)


def _resident(shape, index_map):
    return pl.BlockSpec(shape, index_map, pipeline_mode=pl.Buffered(1))


def _normmod(x, sh, sc):
    ms = jnp.mean(x * x, axis=-1, keepdims=True)
    return (x * lax.rsqrt(ms + EPS)) * (1.0 + sc) + sh


def _sigmoid(x):
    return 1.0 / (1.0 + jnp.exp(-x))


def _log_sigmoid(x):
    return jnp.minimum(x, 0.0) - jnp.log(1.0 + jnp.exp(-jnp.abs(x)))


def _mod_kernel(c_ref, w_ref, b_ref, o_ref):
    c = c_ref[...]
    s = (c * _sigmoid(c)).astype(BF16)
    o_ref[...] = _dot(s, w_ref[...].astype(BF16)) + b_ref[...]


def _modulation(cvec, w_mod, b_mod):
    depth, d, d6 = w_mod.shape
    ns = cvec.shape[0]
    nj = d6 // d
    out = pl.pallas_call(
        _mod_kernel,
        grid=(depth, nj),
        in_specs=[
            pl.BlockSpec((ns, d), lambda l, j: (0, 0)),
            pl.BlockSpec((None, d, d), lambda l, j: (l, 0, j)),
            pl.BlockSpec((None, 1, d), lambda l, j: (l, 0, j)),
        ],
        out_specs=pl.BlockSpec((None, ns, d), lambda l, j: (l, 0, j)),
        out_shape=jax.ShapeDtypeStruct((depth, ns, d6), F32),
        compiler_params=_cparams("arbitrary", "arbitrary"),
        name="modulation",
    )(cvec, w_mod, b_mod.reshape(depth, 1, d6))
    return out.reshape(depth, ns, nj, 1, d)


class _Tiles:
    def __init__(self, batch, tok, ctx):
        assert ctx == ROW_TILE and tok % ROW_TILE == 0
        self.batch, self.tok, self.ctx = batch, tok, ctx
        self.per_batch = tok // ROW_TILE
        self.n = batch * self.per_batch

    def sample(self, i):
        return jnp.where(i % self.per_batch == 0, self.batch, i // self.per_batch)

    def mod_spec(self, layer, which):
        return pl.BlockSpec((None, None, None, 1, D_MODEL),
                            lambda i: (layer, self.sample(i), which, 0, 0))

    def row_spec(self, width):
        return pl.BlockSpec((ROW_TILE, width), lambda i: (i, 0))

    def pos_spec(self, width):
        return pl.BlockSpec((ROW_TILE, width), lambda i: (i % self.per_batch, 0))


def _even_in_kernel(x_ref, sh_ref, sc_ref, w_ref, gb_ref, cos_ref, sin_ref,
                    q_ref, k_ref, v_ref, sg_ref, g_ref, hy_ref):
    h = _normmod(x_ref[...], sh_ref[...], sc_ref[...]).astype(BF16)
    cos = cos_ref[...]
    sin = sin_ref[...]
    lane = lax.broadcasted_iota(jnp.int32, cos.shape, 1)
    first = (lane % (MLSTM_HD // 2)) < (MLSTM_HD // 4)

    def rope(a):
        part = jnp.where(first, pltpu.roll(a, LANES - MLSTM_HD // 4, 1), pltpu.roll(a, MLSTM_HD // 4, 1))
        return a * cos + part * sin

    acc = _dot(h, w_ref[:, 0:MIX_A])
    for hd in range(MLSTM_HEADS):
        sl = slice(hd * MLSTM_HD, (hd + 1) * MLSTM_HD)
        q_ref[:, sl] = rope(acc[:, sl]).astype(BF16)
    acc = _dot(h, w_ref[:, MIX_A:2 * MIX_A]) * (MLSTM_HD ** -0.5)
    for hd in range(MLSTM_HEADS):
        sl = slice(hd * MLSTM_HD, (hd + 1) * MLSTM_HD)
        k_ref[:, sl] = rope(acc[:, sl]).astype(BF16)
    v_ref[...] = _dot(h, w_ref[:, 2 * MIX_A:3 * MIX_A]).astype(BF16)
    sg_ref[...] = _sigmoid(_dot(h, w_ref[:, 3 * MIX_A:4 * MIX_A])).astype(BF16)
    g0 = 4 * MIX_A
    g_ref[...] = _dot(h, w_ref[:, g0:g0 + LANES]) + gb_ref[...]
    h0 = g0 + LANES
    for j in range(3):
        hy_ref[:, j * MIX_B:(j + 1) * MIX_B] = _dot(h, w_ref[:, h0 + j * MIX_B:h0 + (j + 1) * MIX_B])


def _even_in_proj(x2, mod5, layer, tiles, w_pad, gate_b_pad, cos_t, sin_t):
    m = x2.shape[0]
    n_w = w_pad.shape[1]
    outs = (
        jax.ShapeDtypeStruct((m, MIX_A), BF16), jax.ShapeDtypeStruct((m, MIX_A), BF16),
        jax.ShapeDtypeStruct((m, MIX_A), BF16), jax.ShapeDtypeStruct((m, MIX_A), BF16),
        jax.ShapeDtypeStruct((m, LANES), F32), jax.ShapeDtypeStruct((m, 3 * MIX_B), F32),
    )
    return pl.pallas_call(
        _even_in_kernel,
        grid=(tiles.n,),
        in_specs=[
            tiles.row_spec(D_MODEL), tiles.mod_spec(layer, 0), tiles.mod_spec(layer, 1),
            _resident((D_MODEL, n_w), lambda i: (0, 0)),
            pl.BlockSpec((1, LANES), lambda i: (0, 0)),
            tiles.pos_spec(MLSTM_HD), tiles.pos_spec(MLSTM_HD),
        ],
        out_specs=[tiles.row_spec(MIX_A)] * 4 + [tiles.row_spec(LANES), tiles.row_spec(3 * MIX_B)],
        out_shape=outs,
        compiler_params=_cparams("parallel"),
        name="even_in_proj",
    )(x2, mod5, mod5, w_pad, gate_b_pad, cos_t, sin_t)


def _rope_tables(ctx, seq):
    half = MLSTM_HD // 2
    nf = half // 2
    inv = ROPE_BASE ** (-np.arange(nf, dtype=np.float64) / nf)
    t = np.arange(seq)
    rows, cols = t // GRID_W, t % GRID_W

    def one(pos):
        ang = pos[:, None].astype(np.float64) * inv[None, :]
        c = np.concatenate([np.cos(ang), np.cos(ang)], axis=-1)
        s = np.concatenate([-np.sin(ang), np.sin(ang)], axis=-1)
        return c, s

    cr, sr = one(rows)
    cc, sc = one(cols)
    cos = np.concatenate([cr, cc], axis=-1)
    sin = np.concatenate([sr, sc], axis=-1)
    cos = np.concatenate([np.ones((ctx, MLSTM_HD)), cos], axis=0)
    sin = np.concatenate([np.zeros((ctx, MLSTM_HD)), sin], axis=0)
    return jnp.asarray(cos, F32), jnp.asarray(sin, F32)


def _seg_scan(y, pos, op, reverse, axis):
    n = y.shape[axis]
    k = 1
    while k < CHUNK:
        if reverse:
            y = jnp.where(pos < CHUNK - k, op(y, pltpu.roll(y, n - k, axis)), y)
        else:
            y = jnp.where(pos >= k, op(y, pltpu.roll(y, k, axis)), y)
        k *= 2
    return y


def _mlstm_gate_scans(gc_ref, gi_ref, gf_ref, bq_ref, aq_ref, mq_ref, ar_ref):
    g = gc_ref[...]
    lane = lax.broadcasted_iota(jnp.int32, g.shape, 1)
    pos = lax.broadcasted_iota(jnp.int32, g.shape, 0) % CHUNK
    fwd = lane < 2 * MLSTM_HEADS
    lf = pltpu.roll(_log_sigmoid(g) * LOG2E, LANES - MLSTM_HEADS, 1)
    b = jnp.where(fwd, _seg_scan(lf, pos, jnp.add, False, 0), _seg_scan(lf, pos, jnp.add, True, 0))
    a = g * LOG2E - b
    bq_ref[...] = b
    aq_ref[...] = a
    mq_ref[...] = jnp.where(fwd, _seg_scan(a, pos, jnp.maximum, False, 0), _seg_scan(a, pos, jnp.maximum, True, 0))

    lfr = _log_sigmoid(gf_ref[...]) * LOG2E
    sub = lax.broadcasted_iota(jnp.int32, lfr.shape, 0)
    posr = lax.broadcasted_iota(jnp.int32, lfr.shape, 1) % CHUNK
    br = jnp.where(sub < MLSTM_HEADS, _seg_scan(lfr, posr, jnp.add, False, 1), _seg_scan(lfr, posr, jnp.add, True, 1))
    ar_ref[...] = gi_ref[...] * LOG2E - br


def _mlstm_chunk(qc, kc, vc, b_col, a_col, amax_col, a_row, state, m, lower):
    t, dv = vc.shape
    r = lax.broadcasted_iota(jnp.int32, (t, t), 0)
    c = lax.broadcasted_iota(jnp.int32, (t, t), 1)
    incl = (c <= r) if lower else (c >= r)
    mm = jnp.maximum(m, amax_col)
    m_t = b_col + mm
    w = jnp.exp2(jnp.where(incl, a_row - mm, -jnp.inf))
    sc = jnp.exp2(m - mm)
    qkw = _dot_nt(qc, kc) * w
    v_aug = jnp.concatenate([vc, jnp.ones_like(vc)], axis=1)
    res = sc * _dot(qc, state.astype(BF16)) + _dot(qkw.astype(BF16), v_aug)
    h = res[:, :dv] / jnp.maximum(jnp.abs(res[:, dv:]), jnp.exp2(-m_t))
    e = t - 1 if lower else 0
    bl = b_col[e:e + 1, :]
    m_new = bl + jnp.maximum(m, amax_col[e:e + 1, :])
    decay = jnp.exp2(bl + m - m_new)
    kw = (kc.astype(F32) * jnp.exp2(bl + a_col - m_new)).astype(BF16)
    return h, decay * state + _dot_tn(kw, v_aug), m_new


def _mlstm_kernel(q_ref, k_ref, v_ref, sg_ref, gc_ref, gi_ref, gf_ref, hn_ref, o_ref,
                  hf_ref, hb_ref, bq_ref, aq_ref, mq_ref, ar_ref, st_ref, *, n_chunks):
    _mlstm_gate_scans(gc_ref, gi_ref, gf_ref, bq_ref, aq_ref, mq_ref, ar_ref)
    st_ref[...] = jnp.zeros(st_ref.shape, F32)

    def run(off, hd, m, lower, dst_ref):
        lane = hd if lower else 2 * MLSTM_HEADS + hd
        row = hd if lower else MLSTM_HEADS + hd
        slot = 2 * hd + (0 if lower else 1)
        rows = pl.ds(off, CHUNK)
        cols = slice(hd * MLSTM_HD, (hd + 1) * MLSTM_HD)
        h, state, m = _mlstm_chunk(
            q_ref[rows, cols], k_ref[rows, cols], v_ref[rows, cols],
            bq_ref[rows, lane:lane + 1], aq_ref[rows, lane:lane + 1], mq_ref[rows, lane:lane + 1],
            ar_ref[row:row + 1, rows], st_ref[slot], m, lower)
        st_ref[slot] = state
        dst_ref[rows, cols] = h
        return m

    def body(s, ms):
        off_f = pl.multiple_of(s * CHUNK, CHUNK)
        off_b = pl.multiple_of(jnp.where(s == 0, 0, n_chunks - s) * CHUNK, CHUNK)
        out = []
        for hd in range(MLSTM_HEADS):
            out.append(run(off_f, hd, ms[2 * hd], True, hf_ref))
            out.append(run(off_b, hd, ms[2 * hd + 1], False, hb_ref))
        return tuple(out)

    lax.fori_loop(0, n_chunks, body, tuple(jnp.zeros((1, 1), F32) for _ in range(2 * MLSTM_HEADS)))

    for hd in range(MLSTM_HEADS):
        cols = slice(hd * MLSTM_HD, (hd + 1) * MLSTM_HD)
        hm = hf_ref[:, cols] + hb_ref[:, cols]
        ms = jnp.mean(hm * hm, axis=-1, keepdims=True)
        y = hm * lax.rsqrt(ms + EPS) * hn_ref[:, cols]
        o_ref[:, cols] = (y * sg_ref[:, cols].astype(F32)).astype(BF16)


def _mlstm(q, k, v, sg, gcol, gi_rows, gf_rows, hnorm):
    b, tok, _ = q.shape
    full = pl.BlockSpec((None, tok, MIX_A), lambda i: (i, 0, 0))
    rows = pl.BlockSpec((None, 2 * MLSTM_HEADS, tok), lambda i: (i, 0, 0))
    return pl.pallas_call(
        functools.partial(_mlstm_kernel, n_chunks=tok // CHUNK),
        grid=(b,),
        in_specs=[
            full, full, full, full,
            pl.BlockSpec((None, tok, LANES), lambda i: (i, 0, 0)),
            rows, rows,
            pl.BlockSpec((1, MIX_A), lambda i: (0, 0)),
        ],
        out_specs=full,
        out_shape=jax.ShapeDtypeStruct((b, tok, MIX_A), BF16),
        scratch_shapes=[
            pltpu.VMEM((tok, MIX_A), F32), pltpu.VMEM((tok, MIX_A), F32),
            pltpu.VMEM((tok, LANES), F32), pltpu.VMEM((tok, LANES), F32), pltpu.VMEM((tok, LANES), F32),
            pltpu.VMEM((2 * MLSTM_HEADS, tok), F32),
            pltpu.VMEM((2 * MLSTM_HEADS, MLSTM_HD, 2 * MLSTM_HD), F32),
        ],
        compiler_params=_cparams("parallel"),
        name="mlstm",
    )(q, k, v, sg, gcol, gi_rows, gf_rows, hnorm)


def _short_conv_rows(x, w, b, ctx):
    tok = x.shape[0]
    t = lax.broadcasted_iota(jnp.int32, x.shape, 0)
    prev = jnp.where((t == 0) | (t == ctx), 0.0, pltpu.roll(x, 1, 0))
    nxt = jnp.where((t == ctx - 1) | (t == tok - 1), 0.0, pltpu.roll(x, tok - 1, 0))
    return b + prev * w[0:1, :] + x * w[1:2, :] + nxt * w[2:3, :]


def _filter_kernel(emb_ref, w1_ref, b1_ref, w2_ref, b2_ref, fr_ref, w3_ref, dec_ref, o_ref):
    fr = fr_ref[...]
    h = jnp.sin(fr * (_dot(emb_ref[...].astype(BF16), w1_ref[...]) + b1_ref[...]))
    h = jnp.sin(fr * (_dot(h.astype(BF16), w2_ref[...]) + b2_ref[...]))
    h = _dot(h.astype(BF16), w3_ref[...]) * dec_ref[...]
    o_ref[...] = h * lax.rsqrt(jnp.sum(h * h, axis=0, keepdims=True) + EPS)


def _hyena_filters(length, w1p, b1p, w2p, b2p, frp, w3p):
    bands = (HYENA_EMB - 1) // 2
    t = np.linspace(0.0, 1.0, length)[:, None]
    wpos = 2.0 * math.pi * np.arange(length) / length
    fr = np.linspace(1e-4, bands - 1, bands)
    ang = wpos[:, None] * fr[None, :]
    emb = np.concatenate([t, np.cos(ang), -np.sin(ang)], axis=-1)
    emb = np.pad(emb, ((0, 0), (0, LANES - HYENA_EMB)))
    deltas = np.abs(np.linspace(math.log(HYENA_TARGET) / HYENA_SLOW, math.log(HYENA_TARGET) / HYENA_FAST, MIX_B))
    decay = np.exp(-t * deltas[None, :])
    ngrp = w3p.shape[1] // MIX_B
    const = lambda g: (0, 0)
    return pl.pallas_call(
        _filter_kernel,
        grid=(ngrp,),
        in_specs=[
            pl.BlockSpec((length, LANES), const), pl.BlockSpec((LANES, LANES), const),
            pl.BlockSpec((1, LANES), const), pl.BlockSpec((LANES, LANES), const),
            pl.BlockSpec((1, LANES), const), pl.BlockSpec((1, LANES), const),
            pl.BlockSpec((LANES, MIX_B), lambda g: (0, g)),
            pl.BlockSpec((length, MIX_B), const),
        ],
        out_specs=pl.BlockSpec((length, MIX_B), lambda g: (0, g)),
        out_shape=jax.ShapeDtypeStruct((length, ngrp * MIX_B), F32),
        compiler_params=_cparams("arbitrary"),
        name="hyena_filters",
    )(jnp.asarray(emb, F32), w1p, b1p, w2p, b2p, frp, w3p, jnp.asarray(decay, F32))


def _dft_matrices(length):
    period = 2 * length
    k = jnp.arange(length, dtype=jnp.int32)[:, None]
    t = jnp.arange(length, dtype=jnp.int32)[None, :]
    unit = 2.0 * math.pi / period

    def table(freqs):
        ang = ((freqs[:, None] * t) & (period - 1)).astype(F32) * unit
        return jnp.cos(ang), jnp.sin(ang)

    c1, s1 = table(jnp.arange(length // DFT_SPLIT, dtype=jnp.int32) * DFT_SPLIT)
    c0, s0 = table(jnp.arange(DFT_SPLIT, dtype=jnp.int32))
    c = (c1[:, None, :] * c0[None, :, :] - s1[:, None, :] * s0[None, :, :]).reshape(length, length)
    s = (s1[:, None, :] * c0[None, :, :] + c1[:, None, :] * s0[None, :, :]).reshape(length, length)
    alt_t = jnp.where(t % 2 == 0, 1.0, -1.0).astype(F32)
    fwd = jnp.concatenate([c, jnp.where(k == 0, alt_t, -s)], axis=0)
    coef = jnp.where(t == 0, 1.0, 2.0).astype(F32) / period
    alt_k = jnp.where(k % 2 == 0, 1.0, -1.0).astype(F32)
    inv = jnp.concatenate([c * coef, jnp.where(t == 0, alt_k / period, -s * coef)], axis=1)
    return fwd.astype(BF16), inv.astype(BF16)


def _spectrum_kernel(a_ref, hf_ref, hb_ref, o_ref):
    half = pl.program_id(0)
    hf = hf_ref[...]
    hb = hb_ref[...]
    comb = jnp.where(half == 0, hf + hb, hf - hb).astype(BF16)
    o_ref[...] = _dot(a_ref[...], comb)

    @pl.when(half == 1)
    def _():
        t = lax.broadcasted_iota(jnp.int32, hf.shape, 0)
        o_ref[0:1, :] = jnp.sum(jnp.where(t % 2 == 0, hf + hb, -(hf + hb)), axis=0, keepdims=True)


def _filter_spectrum(fwd, filt):
    period, length = fwd.shape
    orders = filt.shape[1] // (2 * MIX_B)
    return pl.pallas_call(
        _spectrum_kernel,
        grid=(2, orders),
        in_specs=[
            pl.BlockSpec((length, length), lambda hlf, o: (hlf, 0)),
            pl.BlockSpec((length, MIX_B), lambda hlf, o: (0, 2 * o)),
            pl.BlockSpec((length, MIX_B), lambda hlf, o: (0, 2 * o + 1)),
        ],
        out_specs=pl.BlockSpec((length, MIX_B), lambda hlf, o: (hlf, o)),
        out_shape=jax.ShapeDtypeStruct((period, orders * MIX_B), F32),
        compiler_params=_cparams("arbitrary", "arbitrary"),
        name="hyena_filter_spectrum",
    )(fwd, filt, filt)


FREQ_CHUNK = 512


def _spectral_product(a_ref, g_ref, z, y_ref, y_off, length):
    ck = min(FREQ_CHUNK, length)
    for c0 in range(0, length, ck):
        zr = _dot(a_ref[c0:c0 + ck, :], z)
        zi = _dot(a_ref[length + c0:length + c0 + ck, :], z)
        gr = g_ref[c0:c0 + ck, :]
        gi = g_ref[length + c0:length + c0 + ck, :]
        yr = zr * gr - zi * gi
        yi = zr * gi + zi * gr
        if c0 == 0:
            first = lax.broadcasted_iota(jnp.int32, yr.shape, 0) == 0
            yr = jnp.where(first, zr * gr, yr)
            yi = jnp.where(first, zi * gi, yi)
        y_ref[y_off + c0:y_off + c0 + ck, :] = yr.astype(BF16)
        y_ref[y_off + length + c0:y_off + length + c0 + ck, :] = yi.astype(BF16)


def _hy_fwd_kernel(*refs, ctx, seq, conv_in):
    if conv_in:
        z_ref, cw_ref, cb_ref, al_ref, ac_ref, gl_ref, gc_ref, y_ref = refs
        z = _short_conv_rows(z_ref[...], cw_ref[...], cb_ref[...], ctx).astype(BF16)
    else:
        z_ref, al_ref, ac_ref, gl_ref, gc_ref, y_ref = refs
        z = z_ref[...].astype(BF16)
    _spectral_product(ac_ref, gc_ref, z[0:ctx, :], y_ref, 0, ctx)
    _spectral_product(al_ref, gl_ref, z[ctx:ctx + seq, :], y_ref, 2 * ctx, seq)


def _conv_specs(blk):
    return [pl.BlockSpec((3, MIX_B), lambda i: (0, blk)), pl.BlockSpec((1, MIX_B), lambda i: (0, blk))]


def _hy_fwd(z, conv, a_lat, a_ctx, g_lat, g_ctx, order, ctx, seq):
    b, tok, _ = z.shape
    rows = 2 * tok
    conv_in = conv is not None
    return pl.pallas_call(
        functools.partial(_hy_fwd_kernel, ctx=ctx, seq=seq, conv_in=conv_in),
        grid=(b,),
        in_specs=[pl.BlockSpec((None, tok, MIX_B), lambda i: (i, 0, 0))]
        + (_conv_specs(0) if conv_in else [])
        + [
            _resident(a_lat.shape, lambda i: (0, 0)),
            _resident(a_ctx.shape, lambda i: (0, 0)),
            _resident((2 * seq, MIX_B), lambda i: (0, order)),
            _resident((2 * ctx, MIX_B), lambda i: (0, order)),
        ],
        out_specs=pl.BlockSpec((None, rows, MIX_B), lambda i: (i, 0, 0)),
        out_shape=jax.ShapeDtypeStruct((b, rows, MIX_B), BF16),
        compiler_params=_cparams("parallel"),
        name="hyena_dft_mul",
    )(z, *(conv if conv_in else ()), a_lat, a_ctx, g_lat, g_ctx)


TIME_CHUNK = 512


def _hy_inv_kernel(*refs, ctx, seq, conv_z):
    if conv_z:
        y_ref, bl_ref, bc_ref, z_ref, zw_ref, zb_ref, x_ref, xw_ref, xb_ref, d_ref, o_ref = refs
        z = _short_conv_rows(z_ref[...], zw_ref[...], zb_ref[...], ctx)
    else:
        y_ref, bl_ref, bc_ref, z_ref, x_ref, xw_ref, xb_ref, d_ref, o_ref = refs
        z = z_ref[...]
    gate = _short_conv_rows(x_ref[...], xw_ref[...], xb_ref[...], ctx)
    d = d_ref[...]

    def part(b_ref, y, t0, length):
        ck = min(TIME_CHUNK, length)
        for c0 in range(0, length, ck):
            rows = slice(t0 + c0, t0 + c0 + ck)
            conv = _dot(b_ref[c0:c0 + ck, :], y)
            o_ref[rows, :] = gate[rows, :] * (conv + d * z[rows, :])

    part(bc_ref, y_ref[0:2 * ctx, :], 0, ctx)
    part(bl_ref, y_ref[2 * ctx:2 * (ctx + seq), :], ctx, seq)


def _hy_inv(y, b_lat, b_ctx, z, conv_z, hy, xblk, conv_x, dskip, order, ctx, seq):
    b, rows, _ = y.shape
    tok = ctx + seq
    has_z = conv_z is not None
    return pl.pallas_call(
        functools.partial(_hy_inv_kernel, ctx=ctx, seq=seq, conv_z=has_z),
        grid=(b,),
        in_specs=[
            pl.BlockSpec((None, rows, MIX_B), lambda i: (i, 0, 0)),
            _resident(b_lat.shape, lambda i: (0, 0)),
            _resident(b_ctx.shape, lambda i: (0, 0)),
            pl.BlockSpec((None, tok, MIX_B), lambda i: (i, 0, 0)),
        ]
        + (_conv_specs(0) if has_z else [])
        + [pl.BlockSpec((None, tok, MIX_B), lambda i: (i, 0, xblk))]
        + _conv_specs(xblk)
        + [pl.BlockSpec((None, 1, MIX_B), lambda i: (order, 0, 0))],
        out_specs=pl.BlockSpec((None, tok, MIX_B), lambda i: (i, 0, 0)),
        out_shape=jax.ShapeDtypeStruct((b, tok, MIX_B), F32),
        compiler_params=_cparams("parallel"),
        name="hyena_idft_gate",
    )(y, b_lat, b_ctx, z, *(conv_z if has_z else ()), hy, *conv_x, dskip)


def _odd_in_kernel(x_ref, sh_ref, sc_ref, w_ref, qn_ref, kn_ref, q_ref, k_ref, v_ref):
    h = _normmod(x_ref[...], sh_ref[...], sc_ref[...]).astype(BF16)
    lane = lax.broadcasted_iota(jnp.int32, (x_ref.shape[0], LANES), 1)
    lo = lane < NA_HD
    half = D_MODEL // 2
    for part, gain_ref, scale, dst in ((0, qn_ref, NA_HD ** -0.5 * LOG2E, q_ref), (1, kn_ref, 1.0, k_ref)):
        gain = gain_ref[...]
        for c in range(2):
            base = part * D_MODEL + c * half
            acc = _dot(h, w_ref[:, base:base + half])
            for g in range(half // LANES):
                a = acc[:, g * LANES:(g + 1) * LANES]
                sq = a * a
                s_lo = jnp.sum(jnp.where(lo, sq, 0.0), axis=-1, keepdims=True)
                s_hi = jnp.sum(jnp.where(lo, 0.0, sq), axis=-1, keepdims=True)
                inv = lax.rsqrt(jnp.where(lo, s_lo, s_hi) * (1.0 / NA_HD) + EPS)
                y = a * inv * gain
                if scale != 1.0:
                    y = y * scale
                dst[:, c * half + g * LANES:c * half + (g + 1) * LANES] = y.astype(BF16)
    for c in range(2):
        base = 2 * D_MODEL + c * half
        v_ref[:, c * half:(c + 1) * half] = _dot(h, w_ref[:, base:base + half]).astype(BF16)


def _odd_in_proj(x2, mod5, layer, tiles, w_qkv, qn2, kn2):
    m = x2.shape[0]
    out = jax.ShapeDtypeStruct((m, D_MODEL), BF16)
    return pl.pallas_call(
        _odd_in_kernel,
        grid=(tiles.n,),
        in_specs=[
            tiles.row_spec(D_MODEL), tiles.mod_spec(layer, 0), tiles.mod_spec(layer, 1),
            _resident((D_MODEL, 3 * D_MODEL), lambda i: (0, 0)),
            pl.BlockSpec((1, LANES), lambda i: (0, 0)), pl.BlockSpec((1, LANES), lambda i: (0, 0)),
        ],
        out_specs=[tiles.row_spec(D_MODEL)] * 3,
        out_shape=(out, out, out),
        compiler_params=_cparams("parallel"),
        name="odd_qkv_proj",
    )(x2, mod5, mod5, w_qkv, qn2, kn2)


def _pair_queries(q):
    lane = lax.broadcasted_iota(jnp.int32, q.shape, 1)
    zero = jnp.zeros_like(q)
    return jnp.concatenate([jnp.where(lane < NA_HD, q, zero), jnp.where(lane < NA_HD, zero, q)], axis=0)


def _unpair(res, n):
    lane = lax.broadcasted_iota(jnp.int32, (n, LANES), 1)
    return jnp.where(lane < NA_HD, res[0:n, :], res[n:2 * n, :])


def _natten_kernel(q_ref, k_ref, v_ref, t_ref, o_ref, va_ref, *, ctx, rows_n, win_rows):
    tok = v_ref.shape[0]
    va_ref[:, 0:LANES] = v_ref[...]
    va_ref[:, LANES:2 * LANES] = jnp.ones((tok, LANES), BF16)
    kc = k_ref[0:ctx, :]
    vc = va_ref[0:ctx, :]

    def softmax_pv(scores, values):
        m = functools.reduce(jnp.maximum, [jnp.max(s, axis=1, keepdims=True) for s in scores])
        res = None
        for s, vblk in zip(scores, values):
            part = _dot(jnp.exp2(s - m).astype(BF16), vblk)
            res = part if res is None else res + part
        return res[:, 0:LANES] * (1.0 / res[:, LANES:2 * LANES])

    w = _pair_queries(q_ref[0:ctx, :])
    o_ref[0:ctx, :] = _unpair(softmax_pv([_dot_nt(w, kc)], [vc]), ctx).astype(BF16)

    nk = win_rows * GRID_W

    def body(r, carry):
        rs = jnp.clip(r - win_rows // 2, 0, rows_n - win_rows)
        dr0 = rs - r + NA_WIN_ROWS - 1
        q_off = pl.multiple_of(ctx + r * GRID_W, GRID_W)
        k_off = pl.multiple_of(ctx + rs * GRID_W, GRID_W)
        w = _pair_queries(q_ref[pl.ds(q_off, GRID_W), :])
        bias = t_ref[dr0 & 1, :, pl.ds(pl.multiple_of((dr0 >> 1) * LANES, LANES), nk)]
        s_lat = _dot_nt(w, k_ref[pl.ds(k_off, nk), :]) + bias
        s_ctx = _dot_nt(w, kc)
        out = softmax_pv([s_lat, s_ctx], [va_ref[pl.ds(k_off, nk), :], vc])
        o_ref[pl.ds(q_off, GRID_W), :] = _unpair(out, GRID_W).astype(BF16)
        return carry

    lax.fori_loop(0, rows_n, body, 0, unroll=min(NA_UNROLL, rows_n))


def _natten(q, k, v, table, ctx, seq):
    b, tok, _ = q.shape
    rows_n = seq // GRID_W
    assert rows_n >= NA_WIN_ROWS and rows_n % min(NA_UNROLL, rows_n) == 0
    pairs = NA_HEADS // 2
    pair_spec = pl.BlockSpec((None, tok, LANES), lambda i, p: (i, 0, p))
    return pl.pallas_call(
        functools.partial(_natten_kernel, ctx=ctx, rows_n=rows_n, win_rows=NA_WIN_ROWS),
        grid=(b, pairs),
        in_specs=[
            pair_spec, pair_spec, pair_spec,
            pl.BlockSpec((None,) + table.shape[1:], lambda i, p: (p, 0, 0, 0)),
        ],
        out_specs=pair_spec,
        out_shape=jax.ShapeDtypeStruct((b, tok, D_MODEL), BF16),
        scratch_shapes=[pltpu.VMEM((tok, 2 * LANES), BF16)],
        compiler_params=_cparams("parallel", "arbitrary"),
        name="natten",
    )(q, k, v, table)


def _natten_bias_table(rpb):
    n_dr = 2 * NA_WIN_ROWS - 1
    n_dc = 2 * NA_WIN_COLS - 1
    cidx = np.arange(GRID_W)
    cstart = np.clip(cidx - NA_WIN_COLS // 2, 0, GRID_W - NA_WIN_COLS)
    cmask = (cidx[None, :] >= cstart[:, None]) & (cidx[None, :] < cstart[:, None] + NA_WIN_COLS)
    dc = np.clip(cidx[None, :] - cidx[:, None] + NA_WIN_COLS - 1, 0, n_dc - 1)
    onehot = ((dc[None] == np.arange(n_dc)[:, None, None]) & cmask[None]).astype(np.float32)
    tb = jnp.einsum("hrc,cqk->hrqk", rpb.astype(F32), jnp.asarray(onehot), precision=lax.Precision.HIGHEST)
    tb = tb * LOG2E + jnp.asarray(np.where(cmask, 0.0, NEG_BIG), F32)
    tb = tb.reshape(NA_HEADS // 2, 2, n_dr, GRID_W, GRID_W).transpose(0, 1, 3, 2, 4)
    tb = tb.reshape(NA_HEADS // 2, 2 * GRID_W, n_dr * GRID_W)
    width = (n_dr + 1) * GRID_W
    tb = jnp.pad(tb, ((0, 0), (0, 0), (0, width + GRID_W - n_dr * GRID_W)))
    return jnp.stack([tb[:, :, 0:width], tb[:, :, GRID_W:GRID_W + width]], axis=1)


FF_CHUNK = 1024


def _mix_mlp_kernel(*refs, widths, per_batch, latent_only):
    x_ref, g1_ref, sh_ref, sc_ref, g2_ref = refs[:5]
    in_refs = refs[5:5 + len(widths)]
    wo_ref, w1_ref, w2_ref, o_ref = refs[5 + len(widths):]

    def compute():
        mix = None
        off = 0
        for r, wd in zip(in_refs, widths):
            part = _dot(r[...].astype(BF16), wo_ref[off:off + wd, :])
            mix = part if mix is None else mix + part
            off += wd
        x = x_ref[...] + g1_ref[...] * mix
        h = _normmod(x, sh_ref[...], sc_ref[...]).astype(BF16)
        acc = None
        for c0 in range(0, D_FF, FF_CHUNK):
            a = jnp.maximum(_dot(h, w1_ref[:, c0:c0 + FF_CHUNK]), 0.0)
            part = _dot((a * a).astype(BF16), w2_ref[c0:c0 + FF_CHUNK, :])
            acc = part if acc is None else acc + part
        o_ref[...] = x + g2_ref[...] * acc

    if latent_only:
        pl.when(pl.program_id(0) % per_batch != 0)(compute)
    else:
        compute()


def _mix_mlp(x2, mod5, layer, tiles, parts, w_out, w1, w2, latent_only):
    widths = tuple(p.shape[1] for p in parts)
    pb = tiles.per_batch
    if latent_only:
        out_rows = tiles.batch * (tiles.tok - tiles.ctx)
        out_spec = pl.BlockSpec((ROW_TILE, D_MODEL), lambda i: (i - i // pb - jnp.where(i % pb == 0, 0, 1), 0))
    else:
        out_rows = x2.shape[0]
        out_spec = tiles.row_spec(D_MODEL)
    return pl.pallas_call(
        functools.partial(_mix_mlp_kernel, widths=widths, per_batch=pb, latent_only=latent_only),
        grid=(tiles.n,),
        in_specs=[tiles.row_spec(D_MODEL), tiles.mod_spec(layer, 2),
                  tiles.mod_spec(layer, 3), tiles.mod_spec(layer, 4), tiles.mod_spec(layer, 5)]
        + [tiles.row_spec(wd) for wd in widths]
        + [_resident(w_out.shape, lambda i: (0, 0)), _resident(w1.shape, lambda i: (0, 0)),
           _resident(w2.shape, lambda i: (0, 0))],
        out_specs=out_spec,
        out_shape=jax.ShapeDtypeStruct((out_rows, D_MODEL), F32),
        input_output_aliases={} if latent_only else {0: 0},
        compiler_params=_cparams("arbitrary" if latent_only else "parallel"),
        name="mix_out_mlp",
    )(x2, mod5, mod5, mod5, mod5, *parts, w_out, w1, w2)


def _pad_cols(a, width):
    return jnp.pad(a, ((0, 0), (0, width - a.shape[1])))


def kernel(x, c, ctx, c_ctx, w_mod, b_mod, w_mlp_in, w_mlp_out, e_w_in, e_gate_b, e_hnorm, e_conv_w, e_conv_b,
           e_f_w1, e_f_b1, e_f_w2, e_f_b2, e_f_w3, e_f_freq, e_hy_d, e_w_out, o_w_qkv, o_qn, o_kn, o_rpb, o_w_out):
    batch, seq, d = x.shape
    n_ctx = ctx.shape[1]
    depth = w_mod.shape[0]
    tok = n_ctx + seq
    assert d == D_MODEL and seq % CHUNK == 0 and n_ctx == CHUNK and seq % GRID_W == 0
    tiles = _Tiles(batch, tok, n_ctx)

    n_samp = -(-(batch + 1) // 8) * 8
    cvec = jnp.concatenate([c, c_ctx[None, :], jnp.zeros((n_samp - batch - 1, d), F32)], axis=0)
    mod5 = _modulation(cvec, w_mod, b_mod)

    cos_t, sin_t = _rope_tables(n_ctx, seq)
    a_lat, b_lat = _dft_matrices(seq)
    a_ctx, b_ctx = _dft_matrices(n_ctx)

    xs = jnp.concatenate([ctx, x], axis=1).reshape(batch * tok, d)

    for l in range(depth):
        i = l // 2
        if l % 2 == 0:
            w_in = e_w_in[i]
            g0 = 4 * MIX_A
            w_pad = jnp.concatenate(
                [w_in[:, :g0], _pad_cols(w_in[:, g0:g0 + N_GATES], LANES), w_in[:, g0 + N_GATES:]], axis=1).astype(BF16)
            gate_b = _pad_cols(e_gate_b[i][None, :], LANES)
            q, k, v, sg, g, hy = _even_in_proj(xs, mod5, l, tiles, w_pad, gate_b, cos_t, sin_t)
            as3 = lambda a: a.reshape(batch, tok, a.shape[-1])
            gcol = as3(g)
            grow = gcol[:, :, :N_GATES].transpose(0, 2, 1)
            hh = MLSTM_HEADS
            gi_rows = jnp.concatenate([grow[:, 0:hh], grow[:, 2 * hh:3 * hh]], axis=1)
            gf_rows = jnp.concatenate([grow[:, hh:2 * hh], grow[:, 3 * hh:4 * hh]], axis=1)
            a_out = _mlstm(as3(q), as3(k), as3(v), as3(sg), gcol, gi_rows, gf_rows, e_hnorm[i][None, :])

            hy3 = as3(hy)
            conv = (e_conv_w[i], e_conv_b[i][None, :])
            pad2 = lambda a: jnp.pad(a, ((0, LANES - a.shape[0]), (0, LANES - a.shape[1])))
            w1p = pad2(e_f_w1[i]).astype(BF16)
            w2p = pad2(e_f_w2[i]).astype(BF16)
            w3p = jnp.pad(e_f_w3[i], ((0, LANES - HYENA_FFN), (0, 0))).astype(BF16)
            b1p = _pad_cols(e_f_b1[i][None, :], LANES)
            b2p = _pad_cols(e_f_b2[i][None, :], LANES)
            frp = _pad_cols(e_f_freq[i][None, :], LANES)
            g_lat = _filter_spectrum(a_lat, _hyena_filters(seq, w1p, b1p, w2p, b2p, frp, w3p))
            g_ctx = _filter_spectrum(a_ctx, _hyena_filters(n_ctx, w1p, b1p, w2p, b2p, frp, w3p))
            dskip = e_hy_d[i][:, None, :]
            z, conv_z = hy3, conv
            for o in range(e_hy_d.shape[1]):
                y = _hy_fwd(z, conv_z, a_lat, a_ctx, g_lat, g_ctx, o, n_ctx, seq)
                z = _hy_inv(y, b_lat, b_ctx, z, conv_z, hy3, 1 + o, conv, dskip, o, n_ctx, seq)
                conv_z = None
            parts = [a_out.reshape(batch * tok, MIX_A), z.reshape(batch * tok, MIX_B)]
            w_out = e_w_out[i].astype(BF16)
        else:
            rep = LANES // NA_HD
            qn2 = jnp.tile(o_qn[i], rep)[None, :]
            kn2 = jnp.tile(o_kn[i], rep)[None, :]
            q, k, v = _odd_in_proj(xs, mod5, l, tiles, o_w_qkv[i].astype(BF16), qn2, kn2)
            as3 = lambda a: a.reshape(batch, tok, D_MODEL)
            table = _natten_bias_table(o_rpb[i])
            att = _natten(as3(q), as3(k), as3(v), table, n_ctx, seq)
            parts = [att.reshape(batch * tok, D_MODEL)]
            w_out = o_w_out[i].astype(BF16)
        xs = _mix_mlp(xs, mod5, l, tiles, parts, w_out, w_mlp_in[l].astype(BF16), w_mlp_out[l].astype(BF16),
                      latent_only=(l == depth - 1))

    return xs.reshape(batch, seq, d)
```

```python
import functools
import math

import numpy as np
import jax
import jax.numpy as jnp
from jax import lax
from jax.experimental import pallas as pl
from jax.experimental.pallas import tpu as pltpu

F32 = jnp.float32
BF16 = jnp.bfloat16

D_MODEL = 1024
D_FF = 4 * D_MODEL
EPS = 1e-6
ROPE_BASE = 10000.0
GRID_W = 64
MIX_A = D_MODEL // 2
MIX_B = D_MODEL - MIX_A
MLSTM_HEADS = 4
MLSTM_HD = MIX_A // MLSTM_HEADS
N_GATES = 4 * MLSTM_HEADS
HYENA_EMB = 33
HYENA_FFN = 64
HYENA_TARGET = 1e-2
HYENA_FAST = 0.3
HYENA_SLOW = 1.5
NA_HEADS = 16
NA_HD = D_MODEL // NA_HEADS
NA_WIN_ROWS = 8
NA_WIN_COLS = 16

LANES = 128
SUB_ROWS = 256
TILE_ROWS = 768
CHUNK = 256
VMEM_LIMIT = 56 * 1024 * 1024
NEG_BIG = -1e30
LOG2E = math.log2(math.e)
DFT_SPLIT = 64
NA_UNROLL = 16


def _cparams(*sem):
    return pltpu.CompilerParams(dimension_semantics=sem, vmem_limit_bytes=VMEM_LIMIT)


def _dot(a, b):
    return jnp.dot(a, b, preferred_element_type=F32)


def _dot_nt(a, b):
    return lax.dot_general(a, b, (((1,), (1,)), ((), ())), preferred_element_type=F32)


def _dot_tn(a, b):
    return lax.dot_general(a, b, (((0,), (0,)), ((), ())), preferred_element_type=F32)


def _resident(shape, index_map):
    return pl.BlockSpec(shape, index_map, pipeline_mode=pl.Buffered(1))


def _normmod(x, sh, sc):
    ms = jnp.mean(x * x, axis=-1, keepdims=True)
    return (x * lax.rsqrt(ms + EPS)) * (1.0 + sc) + sh


def _sigmoid(x):
    return 1.0 / (1.0 + jnp.exp(-x))


def _log_sigmoid(x):
    return jnp.minimum(x, 0.0) - jnp.log(1.0 + jnp.exp(-jnp.abs(x)))


def _mod_kernel(c_ref, w_ref, b_ref, o_ref):
    c = c_ref[...]
    s = (c * _sigmoid(c)).astype(BF16)
    o_ref[...] = _dot(s, w_ref[...].astype(BF16)) + b_ref[...]


def _modulation(cvec, w_mod, b_mod):
    depth, d, d6 = w_mod.shape
    ns = cvec.shape[0]
    nj = d6 // d
    out = pl.pallas_call(
        _mod_kernel,
        grid=(depth, nj),
        in_specs=[
            pl.BlockSpec((ns, d), lambda l, j: (0, 0)),
            pl.BlockSpec((None, d, d), lambda l, j: (l, 0, j)),
            pl.BlockSpec((None, 1, d), lambda l, j: (l, 0, j)),
        ],
        out_specs=pl.BlockSpec((None, ns, d), lambda l, j: (l, 0, j)),
        out_shape=jax.ShapeDtypeStruct((depth, ns, d6), F32),
        compiler_params=_cparams("arbitrary", "arbitrary"),
        name="modulation",
    )(cvec, w_mod, b_mod.reshape(depth, 1, d6))
    return out.reshape(depth, ns, nj, 1, d)


class _Tiles:
    def __init__(self, batch, tok, ctx, rows):
        assert ctx == SUB_ROWS and rows % SUB_ROWS == 0 and tok % rows == 0
        self.batch, self.tok, self.ctx, self.rows = batch, tok, ctx, rows
        self.n_sub = rows // SUB_ROWS
        self.per_batch = tok // rows
        self.n = batch * self.per_batch

    def mod_specs(self, layer, which):
        blk = (None, None, None, 1, D_MODEL)
        return [pl.BlockSpec(blk, lambda i: (layer, self.batch, which, 0, 0)),
                pl.BlockSpec(blk, lambda i: (layer, i // self.per_batch, which, 0, 0))]

    def row_spec(self, width):
        return pl.BlockSpec((self.rows, width), lambda i: (i, 0))

    def pos_spec(self, width):
        return pl.BlockSpec((self.rows, width), lambda i: (i % self.per_batch, 0))


def _sub_mod(c_ref, b_ref, sb, per_batch):
    if sb > 0:
        return b_ref[...]
    return jnp.where(pl.program_id(0) % per_batch == 0, c_ref[...], b_ref[...])


def _sub_rows(sb):
    return slice(sb * SUB_ROWS, (sb + 1) * SUB_ROWS)


def _even_in_kernel(x_ref, shc_ref, shb_ref, scc_ref, scb_ref, w_ref, gb_ref, cos_ref, sin_ref,
                    q_ref, k_ref, v_ref, sg_ref, g_ref, hy_ref, *, n_sub, per_batch):
    lane = lax.broadcasted_iota(jnp.int32, (SUB_ROWS, MLSTM_HD), 1)
    first = (lane % (MLSTM_HD // 2)) < (MLSTM_HD // 4)
    g0 = 4 * MIX_A
    h0 = g0 + LANES
    for sb in range(n_sub):
        rows = _sub_rows(sb)
        h = _normmod(x_ref[rows, :], _sub_mod(shc_ref, shb_ref, sb, per_batch),
                     _sub_mod(scc_ref, scb_ref, sb, per_batch)).astype(BF16)
        cos = cos_ref[rows, :]
        sin = sin_ref[rows, :]

        def rope(a):
            part = jnp.where(first, pltpu.roll(a, LANES - MLSTM_HD // 4, 1), pltpu.roll(a, MLSTM_HD // 4, 1))
            return a * cos + part * sin

        acc = _dot(h, w_ref[:, 0:MIX_A])
        for hd in range(MLSTM_HEADS):
            sl = slice(hd * MLSTM_HD, (hd + 1) * MLSTM_HD)
            q_ref[rows, sl] = rope(acc[:, sl]).astype(BF16)
        acc = _dot(h, w_ref[:, MIX_A:2 * MIX_A]) * (MLSTM_HD ** -0.5)
        for hd in range(MLSTM_HEADS):
            sl = slice(hd * MLSTM_HD, (hd + 1) * MLSTM_HD)
            k_ref[rows, sl] = rope(acc[:, sl]).astype(BF16)
        v_ref[rows, :] = _dot(h, w_ref[:, 2 * MIX_A:3 * MIX_A]).astype(BF16)
        sg_ref[rows, :] = _sigmoid(_dot(h, w_ref[:, 3 * MIX_A:4 * MIX_A])).astype(BF16)
        g_ref[rows, :] = _dot(h, w_ref[:, g0:g0 + LANES]) + gb_ref[...]
        for j in range(3):
            hy_ref[rows, j * MIX_B:(j + 1) * MIX_B] = _dot(h, w_ref[:, h0 + j * MIX_B:h0 + (j + 1) * MIX_B])


def _even_in_proj(x2, mod5, layer, tiles, w_pad, gate_b_pad, cos_t, sin_t):
    m = x2.shape[0]
    n_w = w_pad.shape[1]
    outs = (
        jax.ShapeDtypeStruct((m, MIX_A), BF16), jax.ShapeDtypeStruct((m, MIX_A), BF16),
        jax.ShapeDtypeStruct((m, MIX_A), BF16), jax.ShapeDtypeStruct((m, MIX_A), BF16),
        jax.ShapeDtypeStruct((m, LANES), F32), jax.ShapeDtypeStruct((m, 3 * MIX_B), F32),
    )
    return pl.pallas_call(
        functools.partial(_even_in_kernel, n_sub=tiles.n_sub, per_batch=tiles.per_batch),
        grid=(tiles.n,),
        in_specs=[tiles.row_spec(D_MODEL)] + tiles.mod_specs(layer, 0) + tiles.mod_specs(layer, 1) + [
            _resident((D_MODEL, n_w), lambda i: (0, 0)),
            pl.BlockSpec((1, LANES), lambda i: (0, 0)),
            tiles.pos_spec(MLSTM_HD), tiles.pos_spec(MLSTM_HD),
        ],
        out_specs=[tiles.row_spec(MIX_A)] * 4 + [tiles.row_spec(LANES), tiles.row_spec(3 * MIX_B)],
        out_shape=outs,
        compiler_params=_cparams("parallel"),
        name="even_in_proj",
    )(x2, mod5, mod5, mod5, mod5, w_pad, gate_b_pad, cos_t, sin_t)


def _rope_tables(ctx, seq):
    half = MLSTM_HD // 2
    nf = half // 2
    inv = ROPE_BASE ** (-np.arange(nf, dtype=np.float64) / nf)
    t = np.arange(seq)
    rows, cols = t // GRID_W, t % GRID_W

    def one(pos):
        ang = pos[:, None].astype(np.float64) * inv[None, :]
        c = np.concatenate([np.cos(ang), np.cos(ang)], axis=-1)
        s = np.concatenate([-np.sin(ang), np.sin(ang)], axis=-1)
        return c, s

    cr, sr = one(rows)
    cc, sc = one(cols)
    cos = np.concatenate([cr, cc], axis=-1)
    sin = np.concatenate([sr, sc], axis=-1)
    cos = np.concatenate([np.ones((ctx, MLSTM_HD)), cos], axis=0)
    sin = np.concatenate([np.zeros((ctx, MLSTM_HD)), sin], axis=0)
    return jnp.asarray(cos, F32), jnp.asarray(sin, F32)


def _seg_scan(y, pos, op, reverse, axis):
    n = y.shape[axis]
    k = 1
    while k < CHUNK:
        if reverse:
            y = jnp.where(pos < CHUNK - k, op(y, pltpu.roll(y, n - k, axis)), y)
        else:
            y = jnp.where(pos >= k, op(y, pltpu.roll(y, k, axis)), y)
        k *= 2
    return y


def _mlstm_gate_scans(gc_ref, gi_ref, gf_ref, bq_ref, aq_ref, mq_ref, ar_ref):
    g = gc_ref[...]
    lane = lax.broadcasted_iota(jnp.int32, g.shape, 1)
    pos = lax.broadcasted_iota(jnp.int32, g.shape, 0) % CHUNK
    fwd = lane < 2 * MLSTM_HEADS
    lf = pltpu.roll(_log_sigmoid(g) * LOG2E, LANES - MLSTM_HEADS, 1)
    b = jnp.where(fwd, _seg_scan(lf, pos, jnp.add, False, 0), _seg_scan(lf, pos, jnp.add, True, 0))
    a = g * LOG2E - b
    bq_ref[...] = b
    aq_ref[...] = a
    mq_ref[...] = jnp.where(fwd, _seg_scan(a, pos, jnp.maximum, False, 0), _seg_scan(a, pos, jnp.maximum, True, 0))

    lfr = _log_sigmoid(gf_ref[...]) * LOG2E
    sub = lax.broadcasted_iota(jnp.int32, lfr.shape, 0)
    posr = lax.broadcasted_iota(jnp.int32, lfr.shape, 1) % CHUNK
    br = jnp.where(sub < MLSTM_HEADS, _seg_scan(lfr, posr, jnp.add, False, 1), _seg_scan(lfr, posr, jnp.add, True, 1))
    ar_ref[...] = gi_ref[...] * LOG2E - br


def _mlstm_chunk(qc, kc, vc, b_col, a_col, amax_col, a_row, state, m, lower):
    t, dv = vc.shape
    r = lax.broadcasted_iota(jnp.int32, (t, t), 0)
    c = lax.broadcasted_iota(jnp.int32, (t, t), 1)
    incl = (c <= r) if lower else (c >= r)
    mm = jnp.maximum(m, amax_col)
    m_t = b_col + mm
    w = jnp.exp2(jnp.where(incl, a_row - mm, -jnp.inf))
    sc = jnp.exp2(m - mm)
    qkw = _dot_nt(qc, kc) * w
    v_aug = jnp.concatenate([vc, jnp.ones_like(vc)], axis=1)
    res = sc * _dot(qc, state.astype(BF16)) + _dot(qkw.astype(BF16), v_aug)
    h = res[:, :dv] / jnp.maximum(jnp.abs(res[:, dv:]), jnp.exp2(-m_t))
    e = t - 1 if lower else 0
    bl = b_col[e:e + 1, :]
    m_new = bl + jnp.maximum(m, amax_col[e:e + 1, :])
    decay = jnp.exp2(bl + m - m_new)
    kw = (kc.astype(F32) * jnp.exp2(bl + a_col - m_new)).astype(BF16)
    return h, decay * state + _dot_tn(kw, v_aug), m_new


def _mlstm_kernel(q_ref, k_ref, v_ref, sg_ref, gc_ref, gi_ref, gf_ref, hn_ref, o_ref,
                  hf_ref, hb_ref, bq_ref, aq_ref, mq_ref, ar_ref, st_ref, *, n_chunks):
    _mlstm_gate_scans(gc_ref, gi_ref, gf_ref, bq_ref, aq_ref, mq_ref, ar_ref)
    st_ref[...] = jnp.zeros(st_ref.shape, F32)

    def run(off, hd, m, lower, dst_ref):
        lane = hd if lower else 2 * MLSTM_HEADS + hd
        row = hd if lower else MLSTM_HEADS + hd
        slot = 2 * hd + (0 if lower else 1)
        rows = pl.ds(off, CHUNK)
        cols = slice(hd * MLSTM_HD, (hd + 1) * MLSTM_HD)
        h, state, m = _mlstm_chunk(
            q_ref[rows, cols], k_ref[rows, cols], v_ref[rows, cols],
            bq_ref[rows, lane:lane + 1], aq_ref[rows, lane:lane + 1], mq_ref[rows, lane:lane + 1],
            ar_ref[row:row + 1, rows], st_ref[slot], m, lower)
        st_ref[slot] = state
        dst_ref[rows, cols] = h
        return m

    def body(s, ms):
        off_f = pl.multiple_of(s * CHUNK, CHUNK)
        off_b = pl.multiple_of(jnp.where(s == 0, 0, n_chunks - s) * CHUNK, CHUNK)
        out = []
        for hd in range(MLSTM_HEADS):
            out.append(run(off_f, hd, ms[2 * hd], True, hf_ref))
            out.append(run(off_b, hd, ms[2 * hd + 1], False, hb_ref))
        return tuple(out)

    lax.fori_loop(0, n_chunks, body, tuple(jnp.zeros((1, 1), F32) for _ in range(2 * MLSTM_HEADS)))

    for hd in range(MLSTM_HEADS):
        cols = slice(hd * MLSTM_HD, (hd + 1) * MLSTM_HD)
        hm = hf_ref[:, cols] + hb_ref[:, cols]
        ms = jnp.mean(hm * hm, axis=-1, keepdims=True)
        y = hm * lax.rsqrt(ms + EPS) * hn_ref[:, cols]
        o_ref[:, cols] = (y * sg_ref[:, cols].astype(F32)).astype(BF16)


def _mlstm(q, k, v, sg, gcol, gi_rows, gf_rows, hnorm):
    b, tok, _ = q.shape
    full = pl.BlockSpec((None, tok, MIX_A), lambda i: (i, 0, 0))
    rows = pl.BlockSpec((None, 2 * MLSTM_HEADS, tok), lambda i: (i, 0, 0))
    return pl.pallas_call(
        functools.partial(_mlstm_kernel, n_chunks=tok // CHUNK),
        grid=(b,),
        in_specs=[
            full, full, full, full,
            pl.BlockSpec((None, tok, LANES), lambda i: (i, 0, 0)),
            rows, rows,
            pl.BlockSpec((1, MIX_A), lambda i: (0, 0)),
        ],
        out_specs=full,
        out_shape=jax.ShapeDtypeStruct((b, tok, MIX_A), BF16),
        scratch_shapes=[
            pltpu.VMEM((tok, MIX_A), F32), pltpu.VMEM((tok, MIX_A), F32),
            pltpu.VMEM((tok, LANES), F32), pltpu.VMEM((tok, LANES), F32), pltpu.VMEM((tok, LANES), F32),
            pltpu.VMEM((2 * MLSTM_HEADS, tok), F32),
            pltpu.VMEM((2 * MLSTM_HEADS, MLSTM_HD, 2 * MLSTM_HD), F32),
        ],
        compiler_params=_cparams("parallel"),
        name="mlstm",
    )(q, k, v, sg, gcol, gi_rows, gf_rows, hnorm)


def _short_conv_rows(x, w, b, ctx):
    tok = x.shape[0]
    t = lax.broadcasted_iota(jnp.int32, x.shape, 0)
    prev = jnp.where((t == 0) | (t == ctx), 0.0, pltpu.roll(x, 1, 0))
    nxt = jnp.where((t == ctx - 1) | (t == tok - 1), 0.0, pltpu.roll(x, tok - 1, 0))
    return b + prev * w[0:1, :] + x * w[1:2, :] + nxt * w[2:3, :]


def _filter_kernel(emb_ref, w1_ref, b1_ref, w2_ref, b2_ref, fr_ref, w3_ref, dec_ref, o_ref):
    fr = fr_ref[...]
    h = jnp.sin(fr * (_dot(emb_ref[...].astype(BF16), w1_ref[...]) + b1_ref[...]))
    h = jnp.sin(fr * (_dot(h.astype(BF16), w2_ref[...]) + b2_ref[...]))
    h = _dot(h.astype(BF16), w3_ref[...]) * dec_ref[...]
    o_ref[...] = h * lax.rsqrt(jnp.sum(h * h, axis=0, keepdims=True) + EPS)


def _hyena_filters(length, w1p, b1p, w2p, b2p, frp, w3p):
    bands = (HYENA_EMB - 1) // 2
    t = np.linspace(0.0, 1.0, length)[:, None]
    wpos = 2.0 * math.pi * np.arange(length) / length
    fr = np.linspace(1e-4, bands - 1, bands)
    ang = wpos[:, None] * fr[None, :]
    emb = np.concatenate([t, np.cos(ang), -np.sin(ang)], axis=-1)
    emb = np.pad(emb, ((0, 0), (0, LANES - HYENA_EMB)))
    deltas = np.abs(np.linspace(math.log(HYENA_TARGET) / HYENA_SLOW, math.log(HYENA_TARGET) / HYENA_FAST, MIX_B))
    decay = np.exp(-t * deltas[None, :])
    ngrp = w3p.shape[1] // MIX_B
    const = lambda g: (0, 0)
    return pl.pallas_call(
        _filter_kernel,
        grid=(ngrp,),
        in_specs=[
            pl.BlockSpec((length, LANES), const), pl.BlockSpec((LANES, LANES), const),
            pl.BlockSpec((1, LANES), const), pl.BlockSpec((LANES, LANES), const),
            pl.BlockSpec((1, LANES), const), pl.BlockSpec((1, LANES), const),
            pl.BlockSpec((LANES, MIX_B), lambda g: (0, g)),
            pl.BlockSpec((length, MIX_B), const),
        ],
        out_specs=pl.BlockSpec((length, MIX_B), lambda g: (0, g)),
        out_shape=jax.ShapeDtypeStruct((length, ngrp * MIX_B), F32),
        compiler_params=_cparams("arbitrary"),
        name="hyena_filters",
    )(jnp.asarray(emb, F32), w1p, b1p, w2p, b2p, frp, w3p, jnp.asarray(decay, F32))


def _dft_matrices(length):
    period = 2 * length
    k = jnp.arange(length, dtype=jnp.int32)[:, None]
    t = jnp.arange(length, dtype=jnp.int32)[None, :]
    unit = 2.0 * math.pi / period

    def table(freqs):
        ang = ((freqs[:, None] * t) & (period - 1)).astype(F32) * unit
        return jnp.cos(ang), jnp.sin(ang)

    c1, s1 = table(jnp.arange(length // DFT_SPLIT, dtype=jnp.int32) * DFT_SPLIT)
    c0, s0 = table(jnp.arange(DFT_SPLIT, dtype=jnp.int32))
    c = (c1[:, None, :] * c0[None, :, :] - s1[:, None, :] * s0[None, :, :]).reshape(length, length)
    s = (s1[:, None, :] * c0[None, :, :] + c1[:, None, :] * s0[None, :, :]).reshape(length, length)
    alt_t = jnp.where(t % 2 == 0, 1.0, -1.0).astype(F32)
    fwd = jnp.concatenate([c, jnp.where(k == 0, alt_t, -s)], axis=0)
    coef = jnp.where(t == 0, 1.0, 2.0).astype(F32) / period
    alt_k = jnp.where(k % 2 == 0, 1.0, -1.0).astype(F32)
    inv = jnp.concatenate([c * coef, jnp.where(t == 0, alt_k / period, -s * coef)], axis=1)
    return fwd.astype(BF16), inv.astype(BF16)


def _spectrum_kernel(a_ref, hf_ref, hb_ref, o_ref):
    half = pl.program_id(0)
    hf = hf_ref[...]
    hb = hb_ref[...]
    comb = jnp.where(half == 0, hf + hb, hf - hb).astype(BF16)
    o_ref[...] = _dot(a_ref[...], comb)

    @pl.when(half == 1)
    def _():
        t = lax.broadcasted_iota(jnp.int32, hf.shape, 0)
        o_ref[0:1, :] = jnp.sum(jnp.where(t % 2 == 0, hf + hb, -(hf + hb)), axis=0, keepdims=True)


def _filter_spectrum(fwd, filt):
    period, length = fwd.shape
    orders = filt.shape[1] // (2 * MIX_B)
    return pl.pallas_call(
        _spectrum_kernel,
        grid=(2, orders),
        in_specs=[
            pl.BlockSpec((length, length), lambda hlf, o: (hlf, 0)),
            pl.BlockSpec((length, MIX_B), lambda hlf, o: (0, 2 * o)),
            pl.BlockSpec((length, MIX_B), lambda hlf, o: (0, 2 * o + 1)),
        ],
        out_specs=pl.BlockSpec((length, MIX_B), lambda hlf, o: (hlf, o)),
        out_shape=jax.ShapeDtypeStruct((period, orders * MIX_B), F32),
        compiler_params=_cparams("arbitrary", "arbitrary"),
        name="hyena_filter_spectrum",
    )(fwd, filt, filt)


HY_CH = 256
BFLY_ROWS = 128


def _spectral_product(a_ref, g_ref, z, length):
    zr = _dot(a_ref[0:length, :], z)
    zi = _dot(a_ref[length:2 * length, :], z)
    gr = g_ref[0:length, :]
    gi = g_ref[length:2 * length, :]
    first = lax.broadcasted_iota(jnp.int32, zr.shape, 0) == 0
    yr = jnp.where(first, zr * gr, zr * gr - zi * gi)
    yi = jnp.where(first, zi * gi, zr * gi + zi * gr)
    return jnp.concatenate([yr, yi], axis=0).astype(BF16)


def _radix2_conv(ze, zo, a_ref, b_ref, g_ref, tw_ref):
    n = ze.shape[0]
    e = _dot(a_ref[...], ze)
    o = _dot(a_ref[...], zo)
    rep = ze.shape[1] // LANES
    f0r, f0i, f1r, f1i = [], [], [], []
    for r0 in range(0, n, BFLY_ROWS):
        rows = slice(r0, r0 + BFLY_ROWS)
        rows_im = slice(n + r0, n + r0 + BFLY_ROWS)
        er, ei, orr, oi = e[rows, :], e[rows_im, :], o[rows, :], o[rows_im, :]
        c = jnp.concatenate([tw_ref[rows, :]] * rep, axis=1)
        s = jnp.concatenate([tw_ref[rows_im, :]] * rep, axis=1)
        gar, gai, gbr, gbi = (g_ref[j * n + r0:j * n + r0 + BFLY_ROWS, :] for j in range(4))
        tr = c * orr + s * oi
        ti = c * oi - s * orr
        zar = er + tr
        zbr = er - tr
        zai = ei + ti
        zbi = ti - ei
        if r0 == 0:
            first = lax.broadcasted_iota(jnp.int32, er.shape, 0) == 0
            zai = jnp.where(first, zbr, zai)
            zbi = jnp.where(first, -oi, zbi)
            zbr = jnp.where(first, ei, zbr)
        yar = zar * gar - zai * gai
        yai = zar * gai + zai * gar
        if r0 == 0:
            yar = jnp.where(first, zar * gar, yar)
            yai = jnp.where(first, zai * gai, yai)
        ybr = zbr * gbr - zbi * gbi
        ybi = zbr * gbi + zbi * gbr
        dr = yar - ybr
        di = yai + ybi
        p0r = 0.5 * (yar + ybr)
        p0i = 0.5 * (yai - ybi)
        p1r = 0.5 * (c * dr - s * di)
        p1i = 0.5 * (c * di + s * dr)
        if r0 == 0:
            p0r = jnp.where(first, 0.5 * (yar + yai), p0r)
            p0i = jnp.where(first, ybr, p0i)
            p1r = jnp.where(first, 0.5 * (yar - yai), p1r)
            p1i = jnp.where(first, -ybi, p1i)
        f0r.append(p0r.astype(BF16))
        f0i.append(p0i.astype(BF16))
        f1r.append(p1r.astype(BF16))
        f1i.append(p1i.astype(BF16))
    y_even = _dot(b_ref[...], jnp.concatenate(f0r + f0i, axis=0))
    y_odd = _dot(b_ref[...], jnp.concatenate(f1r + f1i, axis=0))
    return y_even, y_odd


def _hyena_order_kernel(*refs, ctx, seq, conv_z):
    if conv_z:
        (z_ref, zw_ref, zb_ref, x_ref, xw_ref, xb_ref, d_ref, ah_ref, bh_ref, ac_ref, bc_ref, gl_ref, gc_ref,
         tw_ref, o_ref, zs_ref, gs_ref, os_ref) = refs
        z = _short_conv_rows(z_ref[...], zw_ref[...], zb_ref[...], ctx)
    else:
        (z_ref, x_ref, xw_ref, xb_ref, d_ref, ah_ref, bh_ref, ac_ref, bc_ref, gl_ref, gc_ref,
         tw_ref, o_ref, zs_ref, gs_ref, os_ref) = refs
        z = z_ref[...].astype(F32)
    gate = _short_conv_rows(x_ref[...], xw_ref[...], xb_ref[...], ctx)
    d = d_ref[...]
    half = seq // 2
    n_blk = zs_ref.shape[0]

    zc = z[0:ctx, :]
    yc = _dot(bc_ref[...], _spectral_product(ac_ref, gc_ref, zc.astype(BF16), ctx))
    o_ref[0:ctx, :] = (gate[0:ctx, :] * (yc + d * zc)).astype(o_ref.dtype)

    for j in range(n_blk):
        zs_ref[j] = z[ctx:, j * LANES:(j + 1) * LANES]
        gs_ref[j] = gate[ctx:, j * LANES:(j + 1) * LANES]

    def samples(ref, parity):
        return jnp.concatenate([ref[j, pl.ds(parity, half, stride=2), :] for j in range(n_blk)], axis=1)

    ze = samples(zs_ref, 0)
    zo = samples(zs_ref, 1)
    y_even, y_odd = _radix2_conv(ze.astype(BF16), zo.astype(BF16), ah_ref, bh_ref, gl_ref, tw_ref)
    for parity, y, zp in ((0, y_even, ze), (1, y_odd, zo)):
        out = samples(gs_ref, parity) * (y + d * zp)
        for j in range(n_blk):
            os_ref[j, pl.ds(parity, half, stride=2), :] = out[:, j * LANES:(j + 1) * LANES]
    for j in range(n_blk):
        o_ref[ctx:, j * LANES:(j + 1) * LANES] = os_ref[j].astype(o_ref.dtype)


def _conv_specs(blk):
    n_ch = MIX_B // HY_CH
    return [pl.BlockSpec((3, HY_CH), lambda i, j: (0, blk * n_ch + j)),
            pl.BlockSpec((1, HY_CH), lambda i, j: (0, blk * n_ch + j))]


def _hyena_order(z, conv_z, hy, xblk, conv_x, dskip, order, mats, g_lat, g_ctx, twiddle, ctx, seq, out_dtype):
    a_half, b_half, a_ctx, b_ctx = mats
    b, tok, _ = hy.shape
    n_ch = MIX_B // HY_CH
    has_z = conv_z is not None
    const = lambda i, j: (0, 0)
    return pl.pallas_call(
        functools.partial(_hyena_order_kernel, ctx=ctx, seq=seq, conv_z=has_z),
        grid=(b, n_ch),
        in_specs=[pl.BlockSpec((None, tok, HY_CH), lambda i, j: (i, 0, j))]
        + (_conv_specs(0) if has_z else [])
        + [pl.BlockSpec((None, tok, HY_CH), lambda i, j: (i, 0, xblk * n_ch + j))]
        + _conv_specs(xblk)
        + [
            pl.BlockSpec((None, 1, HY_CH), lambda i, j: (order, 0, j)),
            _resident(a_half.shape, const), _resident(b_half.shape, const),
            _resident(a_ctx.shape, const), _resident(b_ctx.shape, const),
            pl.BlockSpec((2 * seq, HY_CH), lambda i, j: (0, order * n_ch + j)),
            pl.BlockSpec((2 * ctx, HY_CH), lambda i, j: (0, order * n_ch + j)),
            _resident(twiddle.shape, const),
        ],
        out_specs=pl.BlockSpec((None, tok, HY_CH), lambda i, j: (i, 0, j)),
        out_shape=jax.ShapeDtypeStruct((b, tok, MIX_B), out_dtype),
        scratch_shapes=[pltpu.VMEM((HY_CH // LANES, seq, LANES), F32)] * 3,
        compiler_params=_cparams("parallel", "arbitrary"),
        name="hyena_order",
    )(z, *(conv_z if has_z else ()), hy, *conv_x, dskip, a_half, b_half, a_ctx, b_ctx, g_lat, g_ctx, twiddle)


def _butterfly_spectrum(g, length):
    n = length // 2
    gr, gi = g[0:length], g[length:2 * length]
    mirror = lambda a: jnp.concatenate([a[n:n + 1], jnp.flip(a[n + 1:length], axis=0)], axis=0)
    return jnp.concatenate([gr[0:n], gi[0:n], mirror(gr), mirror(gi)], axis=0)


def _twiddles(length):
    n = length // 2
    ang = jnp.arange(n, dtype=F32) * (2.0 * math.pi / (2 * length))
    tw = jnp.concatenate([jnp.cos(ang), jnp.sin(ang)])[:, None]
    return jnp.broadcast_to(tw, (2 * n, LANES))


def _odd_in_kernel(x_ref, shc_ref, shb_ref, scc_ref, scb_ref, w_ref, qn_ref, kn_ref, q_ref, k_ref, v_ref,
                   *, n_sub, per_batch):
    lane = lax.broadcasted_iota(jnp.int32, (SUB_ROWS, LANES), 1)
    lo = lane < NA_HD
    half = D_MODEL // 2
    for sb in range(n_sub):
        rows = _sub_rows(sb)
        h = _normmod(x_ref[rows, :], _sub_mod(shc_ref, shb_ref, sb, per_batch),
                     _sub_mod(scc_ref, scb_ref, sb, per_batch)).astype(BF16)
        for part, gain_ref, scale, dst in ((0, qn_ref, NA_HD ** -0.5 * LOG2E, q_ref), (1, kn_ref, 1.0, k_ref)):
            gain = gain_ref[...]
            for c in range(2):
                base = part * D_MODEL + c * half
                acc = _dot(h, w_ref[:, base:base + half])
                for g in range(half // LANES):
                    a = acc[:, g * LANES:(g + 1) * LANES]
                    sq = a * a
                    s_lo = jnp.sum(jnp.where(lo, sq, 0.0), axis=-1, keepdims=True)
                    s_hi = jnp.sum(jnp.where(lo, 0.0, sq), axis=-1, keepdims=True)
                    inv = lax.rsqrt(jnp.where(lo, s_lo, s_hi) * (1.0 / NA_HD) + EPS)
                    y = a * inv * gain
                    if scale != 1.0:
                        y = y * scale
                    dst[rows, c * half + g * LANES:c * half + (g + 1) * LANES] = y.astype(BF16)
        for c in range(2):
            base = 2 * D_MODEL + c * half
            v_ref[rows, c * half:(c + 1) * half] = _dot(h, w_ref[:, base:base + half]).astype(BF16)


def _odd_in_proj(x2, mod5, layer, tiles, w_qkv, qn2, kn2):
    m = x2.shape[0]
    out = jax.ShapeDtypeStruct((m, D_MODEL), BF16)
    return pl.pallas_call(
        functools.partial(_odd_in_kernel, n_sub=tiles.n_sub, per_batch=tiles.per_batch),
        grid=(tiles.n,),
        in_specs=[tiles.row_spec(D_MODEL)] + tiles.mod_specs(layer, 0) + tiles.mod_specs(layer, 1) + [
            _resident((D_MODEL, 3 * D_MODEL), lambda i: (0, 0)),
            pl.BlockSpec((1, LANES), lambda i: (0, 0)), pl.BlockSpec((1, LANES), lambda i: (0, 0)),
        ],
        out_specs=[tiles.row_spec(D_MODEL)] * 3,
        out_shape=(out, out, out),
        compiler_params=_cparams("parallel"),
        name="odd_qkv_proj",
    )(x2, mod5, mod5, mod5, mod5, w_qkv, qn2, kn2)


def _pair_queries(q):
    lane = lax.broadcasted_iota(jnp.int32, q.shape, 1)
    zero = jnp.zeros_like(q)
    return jnp.concatenate([jnp.where(lane < NA_HD, q, zero), jnp.where(lane < NA_HD, zero, q)], axis=0)


def _unpair(res, n):
    lane = lax.broadcasted_iota(jnp.int32, (n, LANES), 1)
    return jnp.where(lane < NA_HD, res[0:n, :], res[n:2 * n, :])


def _natten_kernel(q_ref, k_ref, v_ref, t_ref, o_ref, va_ref, *, ctx, rows_n, win_rows):
    tok = v_ref.shape[0]
    va_ref[:, 0:LANES] = v_ref[...]
    va_ref[:, LANES:2 * LANES] = jnp.ones((tok, LANES), BF16)
    kc = k_ref[0:ctx, :]
    vc = va_ref[0:ctx, :]

    def softmax_pv(scores, values):
        m = functools.reduce(jnp.maximum, [jnp.max(s, axis=1, keepdims=True) for s in scores])
        res = None
        for s, vblk in zip(scores, values):
            part = _dot(jnp.exp2(s - m).astype(BF16), vblk)
            res = part if res is None else res + part
        return res[:, 0:LANES] * (1.0 / res[:, LANES:2 * LANES])

    w = _pair_queries(q_ref[0:ctx, :])
    o_ref[0:ctx, :] = _unpair(softmax_pv([_dot_nt(w, kc)], [vc]), ctx).astype(BF16)

    nk = win_rows * GRID_W

    def body(r, carry):
        rs = jnp.clip(r - win_rows // 2, 0, rows_n - win_rows)
        dr0 = rs - r + NA_WIN_ROWS - 1
        q_off = pl.multiple_of(ctx + r * GRID_W, GRID_W)
        k_off = pl.multiple_of(ctx + rs * GRID_W, GRID_W)
        w = _pair_queries(q_ref[pl.ds(q_off, GRID_W), :])
        bias = t_ref[dr0 & 1, :, pl.ds(pl.multiple_of((dr0 >> 1) * LANES, LANES), nk)]
        s_lat = _dot_nt(w, k_ref[pl.ds(k_off, nk), :]) + bias
        s_ctx = _dot_nt(w, kc)
        out = softmax_pv([s_lat, s_ctx], [va_ref[pl.ds(k_off, nk), :], vc])
        o_ref[pl.ds(q_off, GRID_W), :] = _unpair(out, GRID_W).astype(BF16)
        return carry

    lax.fori_loop(0, rows_n, body, 0, unroll=min(NA_UNROLL, rows_n))


def _natten(q, k, v, table, ctx, seq):
    b, tok, _ = q.shape
    rows_n = seq // GRID_W
    assert rows_n >= NA_WIN_ROWS and rows_n % min(NA_UNROLL, rows_n) == 0
    pairs = NA_HEADS // 2
    pair_spec = pl.BlockSpec((None, tok, LANES), lambda i, p: (i, 0, p))
    return pl.pallas_call(
        functools.partial(_natten_kernel, ctx=ctx, rows_n=rows_n, win_rows=NA_WIN_ROWS),
        grid=(b, pairs),
        in_specs=[
            pair_spec, pair_spec, pair_spec,
            pl.BlockSpec((None,) + table.shape[1:], lambda i, p: (p, 0, 0, 0)),
        ],
        out_specs=pair_spec,
        out_shape=jax.ShapeDtypeStruct((b, tok, D_MODEL), BF16),
        scratch_shapes=[pltpu.VMEM((tok, 2 * LANES), BF16)],
        compiler_params=_cparams("parallel", "arbitrary"),
        name="natten",
    )(q, k, v, table)


def _natten_bias_table(rpb):
    n_dr = 2 * NA_WIN_ROWS - 1
    n_dc = 2 * NA_WIN_COLS - 1
    cidx = np.arange(GRID_W)
    cstart = np.clip(cidx - NA_WIN_COLS // 2, 0, GRID_W - NA_WIN_COLS)
    cmask = (cidx[None, :] >= cstart[:, None]) & (cidx[None, :] < cstart[:, None] + NA_WIN_COLS)
    dc = np.clip(cidx[None, :] - cidx[:, None] + NA_WIN_COLS - 1, 0, n_dc - 1)
    onehot = ((dc[None] == np.arange(n_dc)[:, None, None]) & cmask[None]).astype(np.float32)
    tb = jnp.einsum("hrc,cqk->hrqk", rpb.astype(F32), jnp.asarray(onehot), precision=lax.Precision.HIGHEST)
    tb = tb * LOG2E + jnp.asarray(np.where(cmask, 0.0, NEG_BIG), F32)
    tb = tb.reshape(NA_HEADS // 2, 2, n_dr, GRID_W, GRID_W).transpose(0, 1, 3, 2, 4)
    tb = tb.reshape(NA_HEADS // 2, 2 * GRID_W, n_dr * GRID_W)
    width = (n_dr + 1) * GRID_W
    tb = jnp.pad(tb, ((0, 0), (0, 0), (0, width + GRID_W - n_dr * GRID_W)))
    return jnp.stack([tb[:, :, 0:width], tb[:, :, GRID_W:GRID_W + width]], axis=1)


FF_CHUNK = 1024


def _mix_mlp_kernel(*refs, widths, n_sub, per_batch, latent_only):
    x_ref = refs[0]
    g1, sh, sc, g2 = (refs[1 + 2 * j:3 + 2 * j] for j in range(4))
    in_refs = refs[9:9 + len(widths)]
    wo_ref, w1_ref, w2_ref, o_ref = refs[9 + len(widths):]

    def compute(sb):
        rows = _sub_rows(sb)
        mix = None
        off = 0
        for r, wd in zip(in_refs, widths):
            part = _dot(r[rows, :].astype(BF16), wo_ref[off:off + wd, :])
            mix = part if mix is None else mix + part
            off += wd
        x = x_ref[rows, :] + _sub_mod(*g1, sb, per_batch) * mix
        h = _normmod(x, _sub_mod(*sh, sb, per_batch), _sub_mod(*sc, sb, per_batch)).astype(BF16)
        acc = None
        for c0 in range(0, D_FF, FF_CHUNK):
            a = jnp.maximum(_dot(h, w1_ref[:, c0:c0 + FF_CHUNK]), 0.0)
            part = _dot((a * a).astype(BF16), w2_ref[c0:c0 + FF_CHUNK, :])
            acc = part if acc is None else acc + part
        o_ref[rows, :] = x + _sub_mod(*g2, sb, per_batch) * acc

    if latent_only:
        assert n_sub == 1
        pl.when(pl.program_id(0) % per_batch != 0)(functools.partial(compute, 0))
    else:
        for sb in range(n_sub):
            compute(sb)


def _mix_mlp(x2, mod5, layer, tiles, parts, w_out, w1, w2, latent_only):
    widths = tuple(p.shape[1] for p in parts)
    pb = tiles.per_batch
    if latent_only:
        out_rows = tiles.batch * (tiles.tok - tiles.ctx)
        out_spec = pl.BlockSpec((tiles.rows, D_MODEL), lambda i: (i - i // pb - jnp.where(i % pb == 0, 0, 1), 0))
    else:
        out_rows = x2.shape[0]
        out_spec = tiles.row_spec(D_MODEL)
    mods = [s for which in (2, 3, 4, 5) for s in tiles.mod_specs(layer, which)]
    return pl.pallas_call(
        functools.partial(_mix_mlp_kernel, widths=widths, n_sub=tiles.n_sub, per_batch=pb, latent_only=latent_only),
        grid=(tiles.n,),
        in_specs=[tiles.row_spec(D_MODEL)] + mods
        + [tiles.row_spec(wd) for wd in widths]
        + [_resident(w_out.shape, lambda i: (0, 0)), _resident(w1.shape, lambda i: (0, 0)),
           _resident(w2.shape, lambda i: (0, 0))],
        out_specs=out_spec,
        out_shape=jax.ShapeDtypeStruct((out_rows, D_MODEL), F32),
        input_output_aliases={} if latent_only else {0: 0},
        compiler_params=_cparams("arbitrary" if latent_only else "parallel"),
        name="mix_out_mlp",
    )(x2, *([mod5] * len(mods)), *parts, w_out, w1, w2)


def _pad_cols(a, width):
    return jnp.pad(a, ((0, 0), (0, width - a.shape[1])))


def kernel(x, c, ctx, c_ctx, w_mod, b_mod, w_mlp_in, w_mlp_out, e_w_in, e_gate_b, e_hnorm, e_conv_w, e_conv_b,
           e_f_w1, e_f_b1, e_f_w2, e_f_b2, e_f_w3, e_f_freq, e_hy_d, e_w_out, o_w_qkv, o_qn, o_kn, o_rpb, o_w_out):
    batch, seq, d = x.shape
    n_ctx = ctx.shape[1]
    depth = w_mod.shape[0]
    tok = n_ctx + seq
    assert d == D_MODEL and seq % CHUNK == 0 and n_ctx == CHUNK and seq % GRID_W == 0
    tiles = _Tiles(batch, tok, n_ctx, TILE_ROWS if tok % TILE_ROWS == 0 else SUB_ROWS)
    tiles_last = _Tiles(batch, tok, n_ctx, SUB_ROWS)

    n_samp = -(-(batch + 1) // 8) * 8
    cvec = jnp.concatenate([c, c_ctx[None, :], jnp.zeros((n_samp - batch - 1, d), F32)], axis=0)
    mod5 = _modulation(cvec, w_mod, b_mod)

    cos_t, sin_t = _rope_tables(n_ctx, seq)
    a_lat, _ = _dft_matrices(seq)
    a_ctx, b_ctx = _dft_matrices(n_ctx)
    dft_mats = _dft_matrices(seq // 2) + (a_ctx, b_ctx)
    twiddle = _twiddles(seq)

    xs = jnp.concatenate([ctx, x], axis=1).reshape(batch * tok, d)

    for l in range(depth):
        i = l // 2
        if l % 2 == 0:
            w_in = e_w_in[i]
            g0 = 4 * MIX_A
            w_pad = jnp.concatenate(
                [w_in[:, :g0], _pad_cols(w_in[:, g0:g0 + N_GATES], LANES), w_in[:, g0 + N_GATES:]], axis=1).astype(BF16)
            gate_b = _pad_cols(e_gate_b[i][None, :], LANES)
            q, k, v, sg, g, hy = _even_in_proj(xs, mod5, l, tiles, w_pad, gate_b, cos_t, sin_t)
            as3 = lambda a: a.reshape(batch, tok, a.shape[-1])
            gcol = as3(g)
            grow = gcol[:, :, :N_GATES].transpose(0, 2, 1)
            hh = MLSTM_HEADS
            gi_rows = jnp.concatenate([grow[:, 0:hh], grow[:, 2 * hh:3 * hh]], axis=1)
            gf_rows = jnp.concatenate([grow[:, hh:2 * hh], grow[:, 3 * hh:4 * hh]], axis=1)
            a_out = _mlstm(as3(q), as3(k), as3(v), as3(sg), gcol, gi_rows, gf_rows, e_hnorm[i][None, :])

            hy3 = as3(hy)
            conv = (e_conv_w[i], e_conv_b[i][None, :])
            pad2 = lambda a: jnp.pad(a, ((0, LANES - a.shape[0]), (0, LANES - a.shape[1])))
            w1p = pad2(e_f_w1[i]).astype(BF16)
            w2p = pad2(e_f_w2[i]).astype(BF16)
            w3p = jnp.pad(e_f_w3[i], ((0, LANES - HYENA_FFN), (0, 0))).astype(BF16)
            b1p = _pad_cols(e_f_b1[i][None, :], LANES)
            b2p = _pad_cols(e_f_b2[i][None, :], LANES)
            frp = _pad_cols(e_f_freq[i][None, :], LANES)
            g_lat = _butterfly_spectrum(
                _filter_spectrum(a_lat, _hyena_filters(seq, w1p, b1p, w2p, b2p, frp, w3p)), seq)
            g_ctx = _filter_spectrum(a_ctx, _hyena_filters(n_ctx, w1p, b1p, w2p, b2p, frp, w3p))
            dskip = e_hy_d[i][:, None, :]
            n_ord = e_hy_d.shape[1]
            z, conv_z = hy3, conv
            for o in range(n_ord):
                z = _hyena_order(z, conv_z, hy3, 1 + o, conv, dskip, o, dft_mats, g_lat, g_ctx, twiddle,
                                 n_ctx, seq, F32)
                conv_z = None
            parts = [a_out.reshape(batch * tok, MIX_A), z.reshape(batch * tok, MIX_B)]
            w_out = e_w_out[i].astype(BF16)
        else:
            rep = LANES // NA_HD
            qn2 = jnp.tile(o_qn[i], rep)[None, :]
            kn2 = jnp.tile(o_kn[i], rep)[None, :]
            q, k, v = _odd_in_proj(xs, mod5, l, tiles, o_w_qkv[i].astype(BF16), qn2, kn2)
            as3 = lambda a: a.reshape(batch, tok, D_MODEL)
            table = _natten_bias_table(o_rpb[i])
            att = _natten(as3(q), as3(k), as3(v), table, n_ctx, seq)
            parts = [att.reshape(batch * tok, D_MODEL)]
            w_out = o_w_out[i].astype(BF16)
        last = l == depth - 1
        xs = _mix_mlp(xs, mod5, l, tiles_last if last else tiles, parts, w_out,
                      w_mlp_in[l].astype(BF16), w_mlp_out[l].astype(BF16), latent_only=last)

    return xs.reshape(batch, seq, d)
```

```python
import functools
import math

import numpy as np
import jax
import jax.numpy as jnp
from jax import lax
from jax.experimental import pallas as pl
from jax.experimental.pallas import tpu as pltpu

F32 = jnp.float32
BF16 = jnp.bfloat16

D_MODEL = 1024
D_FF = 4 * D_MODEL
EPS = 1e-6
ROPE_BASE = 10000.0
GRID_W = 64
MIX_A = D_MODEL // 2
MIX_B = D_MODEL - MIX_A
MLSTM_HEADS = 4
MLSTM_HD = MIX_A // MLSTM_HEADS
N_GATES = 4 * MLSTM_HEADS
HYENA_EMB = 33
HYENA_FFN = 64
HYENA_TARGET = 1e-2
HYENA_FAST = 0.3
HYENA_SLOW = 1.5
NA_HEADS = 16
NA_HD = D_MODEL // NA_HEADS
NA_WIN_ROWS = 8
NA_WIN_COLS = 16

LANES = 128
SUB_ROWS = 256
TILE_ROWS = 768
CHUNK = 256
VMEM_LIMIT = 56 * 1024 * 1024
NEG_BIG = -1e30
LOG2E = math.log2(math.e)
DFT_SPLIT = 64
NA_UNROLL = 16


def _cparams(*sem):
    return pltpu.CompilerParams(dimension_semantics=sem, vmem_limit_bytes=VMEM_LIMIT)


def _dot(a, b):
    return jnp.dot(a, b, preferred_element_type=F32)


def _dot_nt(a, b):
    return lax.dot_general(a, b, (((1,), (1,)), ((), ())), preferred_element_type=F32)


def _dot_tn(a, b):
    return lax.dot_general(a, b, (((0,), (0,)), ((), ())), preferred_element_type=F32)


def _resident(shape, index_map):
    return pl.BlockSpec(shape, index_map, pipeline_mode=pl.Buffered(1))


def _normmod(x, sh, sc):
    ms = jnp.mean(x * x, axis=-1, keepdims=True)
    return (x * lax.rsqrt(ms + EPS)) * (1.0 + sc) + sh


def _sigmoid(x):
    return 1.0 / (1.0 + jnp.exp(-x))


def _log_sigmoid(x):
    return jnp.minimum(x, 0.0) - jnp.log(1.0 + jnp.exp(-jnp.abs(x)))


def _mod_kernel(c_ref, w_ref, b_ref, o_ref):
    c = c_ref[...]
    s = (c * _sigmoid(c)).astype(BF16)
    o_ref[...] = _dot(s, w_ref[...].astype(BF16)) + b_ref[...]


def _modulation(cvec, w_mod, b_mod):
    depth, d, d6 = w_mod.shape
    ns = cvec.shape[0]
    nj = d6 // d
    out = pl.pallas_call(
        _mod_kernel,
        grid=(depth, nj),
        in_specs=[
            pl.BlockSpec((ns, d), lambda l, j: (0, 0)),
            pl.BlockSpec((None, d, d), lambda l, j: (l, 0, j)),
            pl.BlockSpec((None, 1, d), lambda l, j: (l, 0, j)),
        ],
        out_specs=pl.BlockSpec((None, ns, d), lambda l, j: (l, 0, j)),
        out_shape=jax.ShapeDtypeStruct((depth, ns, d6), F32),
        compiler_params=_cparams("arbitrary", "arbitrary"),
        name="modulation",
    )(cvec, w_mod, b_mod.reshape(depth, 1, d6))
    return out.reshape(depth, ns, nj, 1, d)


class _Tiles:
    def __init__(self, batch, tok, ctx, rows):
        assert ctx == SUB_ROWS and rows % SUB_ROWS == 0 and tok % rows == 0
        self.batch, self.tok, self.ctx, self.rows = batch, tok, ctx, rows
        self.n_sub = rows // SUB_ROWS
        self.per_batch = tok // rows
        self.n = batch * self.per_batch

    def mod_specs(self, layer, which):
        blk = (None, None, None, 1, D_MODEL)
        return [pl.BlockSpec(blk, lambda i: (layer, self.batch, which, 0, 0)),
                pl.BlockSpec(blk, lambda i: (layer, i // self.per_batch, which, 0, 0))]

    def row_spec(self, width):
        return pl.BlockSpec((self.rows, width), lambda i: (i, 0))

    def pos_spec(self, width):
        return pl.BlockSpec((self.rows, width), lambda i: (i % self.per_batch, 0))


def _sub_mod(c_ref, b_ref, sb, per_batch):
    if sb > 0:
        return b_ref[...]
    return jnp.where(pl.program_id(0) % per_batch == 0, c_ref[...], b_ref[...])


def _sub_rows(sb):
    return slice(sb * SUB_ROWS, (sb + 1) * SUB_ROWS)


def _even_in_kernel(x_ref, shc_ref, shb_ref, scc_ref, scb_ref, w_ref, gb_ref, cos_ref, sin_ref,
                    q_ref, k_ref, v_ref, sg_ref, gt_ref, hy_ref, *, n_sub, per_batch):
    lane = lax.broadcasted_iota(jnp.int32, (SUB_ROWS, MLSTM_HD), 1)
    first = (lane % (MLSTM_HD // 2)) < (MLSTM_HD // 4)
    g0 = 4 * MIX_A
    h0 = g0 + LANES
    for sb in range(n_sub):
        rows = _sub_rows(sb)
        h = _normmod(x_ref[rows, :], _sub_mod(shc_ref, shb_ref, sb, per_batch),
                     _sub_mod(scc_ref, scb_ref, sb, per_batch)).astype(BF16)
        cos = cos_ref[rows, :]
        sin = sin_ref[rows, :]

        def rope(a):
            part = jnp.where(first, pltpu.roll(a, LANES - MLSTM_HD // 4, 1), pltpu.roll(a, MLSTM_HD // 4, 1))
            return a * cos + part * sin

        acc = _dot(h, w_ref[:, 0:MIX_A])
        for hd in range(MLSTM_HEADS):
            sl = slice(hd * MLSTM_HD, (hd + 1) * MLSTM_HD)
            q_ref[rows, sl] = rope(acc[:, sl]).astype(BF16)
        acc = _dot(h, w_ref[:, MIX_A:2 * MIX_A]) * (MLSTM_HD ** -0.5)
        for hd in range(MLSTM_HEADS):
            sl = slice(hd * MLSTM_HD, (hd + 1) * MLSTM_HD)
            k_ref[rows, sl] = rope(acc[:, sl]).astype(BF16)
        v_ref[rows, :] = _dot(h, w_ref[:, 2 * MIX_A:3 * MIX_A]).astype(BF16)
        sg_ref[rows, :] = _sigmoid(_dot(h, w_ref[:, 3 * MIX_A:4 * MIX_A])).astype(BF16)
        gt_ref[:, rows] = (_dot(h, w_ref[:, g0:g0 + LANES]) + gb_ref[...]).T
        for j in range(3):
            hy_ref[rows, j * MIX_B:(j + 1) * MIX_B] = _dot(h, w_ref[:, h0 + j * MIX_B:h0 + (j + 1) * MIX_B])


def _even_in_proj(x2, mod5, layer, tiles, w_pad, gate_b_pad, cos_t, sin_t):
    m = x2.shape[0]
    n_w = w_pad.shape[1]
    outs = (
        jax.ShapeDtypeStruct((m, MIX_A), BF16), jax.ShapeDtypeStruct((m, MIX_A), BF16),
        jax.ShapeDtypeStruct((m, MIX_A), BF16), jax.ShapeDtypeStruct((m, MIX_A), BF16),
        jax.ShapeDtypeStruct((LANES, m), F32), jax.ShapeDtypeStruct((m, 3 * MIX_B), F32),
    )
    return pl.pallas_call(
        functools.partial(_even_in_kernel, n_sub=tiles.n_sub, per_batch=tiles.per_batch),
        grid=(tiles.n,),
        in_specs=[tiles.row_spec(D_MODEL)] + tiles.mod_specs(layer, 0) + tiles.mod_specs(layer, 1) + [
            _resident((D_MODEL, n_w), lambda i: (0, 0)),
            pl.BlockSpec((1, LANES), lambda i: (0, 0)),
            tiles.pos_spec(MLSTM_HD), tiles.pos_spec(MLSTM_HD),
        ],
        out_specs=[tiles.row_spec(MIX_A)] * 4
        + [pl.BlockSpec((LANES, tiles.rows), lambda i: (0, i)), tiles.row_spec(3 * MIX_B)],
        out_shape=outs,
        compiler_params=_cparams("parallel"),
        name="even_in_proj",
    )(x2, mod5, mod5, mod5, mod5, w_pad, gate_b_pad, cos_t, sin_t)


def _rope_tables(ctx, seq):
    half = MLSTM_HD // 2
    nf = half // 2
    inv = ROPE_BASE ** (-np.arange(nf, dtype=np.float64) / nf)
    t = np.arange(seq)
    rows, cols = t // GRID_W, t % GRID_W

    def one(pos):
        ang = pos[:, None].astype(np.float64) * inv[None, :]
        c = np.concatenate([np.cos(ang), np.cos(ang)], axis=-1)
        s = np.concatenate([-np.sin(ang), np.sin(ang)], axis=-1)
        return c, s

    cr, sr = one(rows)
    cc, sc = one(cols)
    cos = np.concatenate([cr, cc], axis=-1)
    sin = np.concatenate([sr, sc], axis=-1)
    cos = np.concatenate([np.ones((ctx, MLSTM_HD)), cos], axis=0)
    sin = np.concatenate([np.zeros((ctx, MLSTM_HD)), sin], axis=0)
    return jnp.asarray(cos, F32), jnp.asarray(sin, F32)


def _seg_scan(y, pos, op, reverse, axis):
    n = y.shape[axis]
    k = 1
    while k < CHUNK:
        if reverse:
            y = jnp.where(pos < CHUNK - k, op(y, pltpu.roll(y, n - k, axis)), y)
        else:
            y = jnp.where(pos >= k, op(y, pltpu.roll(y, k, axis)), y)
        k *= 2
    return y


def _mlstm_gate_scans(gt_ref, qc_refs, ar_ref):
    hh = MLSTM_HEADS
    g = gt_ref[...]
    gi = jnp.concatenate([g[0:hh], g[2 * hh:3 * hh]], axis=0) * LOG2E
    lf = _log_sigmoid(jnp.concatenate([g[hh:2 * hh], g[3 * hh:4 * hh]], axis=0)) * LOG2E
    fwd = lax.broadcasted_iota(jnp.int32, lf.shape, 0) < hh
    pos = lax.broadcasted_iota(jnp.int32, lf.shape, 1) % CHUNK
    b = jnp.where(fwd, _seg_scan(lf, pos, jnp.add, False, 1), _seg_scan(lf, pos, jnp.add, True, 1))
    a = gi - b
    amax = jnp.where(fwd, _seg_scan(a, pos, jnp.maximum, False, 1), _seg_scan(a, pos, jnp.maximum, True, 1))
    ar_ref[...] = a
    stack = jnp.concatenate([b, a, amax, jnp.zeros_like(b)], axis=0)
    hi = stack.astype(BF16)
    rest = stack - hi.astype(F32)
    mid = rest.astype(BF16)
    lo = (rest - mid.astype(F32)).astype(BF16)
    n = stack.shape[0]
    nq = b.shape[0]
    eye = (lax.broadcasted_iota(jnp.int32, (n, LANES), 0) == lax.broadcasted_iota(jnp.int32, (n, LANES), 1))
    eye = eye.astype(F32).astype(BF16)
    cols = (_dot_tn(hi, eye) + _dot_tn(mid, eye)) + _dot_tn(lo, eye)
    for j, ref in enumerate(qc_refs):
        ref[...] = cols if j == 0 else pltpu.roll(cols, LANES - j * nq, 1)


def _mlstm_chunk(qc, kc, vc, b_col, a_col, amax_col, a_row, state, m, lower):
    t, dv = vc.shape
    r = lax.broadcasted_iota(jnp.int32, (t, t), 0)
    c = lax.broadcasted_iota(jnp.int32, (t, t), 1)
    incl = (c <= r) if lower else (c >= r)
    mm = jnp.maximum(m, amax_col)
    m_t = b_col + mm
    w = jnp.exp2(jnp.where(incl, a_row - mm, -jnp.inf))
    sc = jnp.exp2(m - mm)
    qkw = _dot_nt(qc, kc) * w
    v_aug = jnp.concatenate([vc, jnp.ones_like(vc)], axis=1)
    res = sc * _dot(qc, state.astype(BF16)) + _dot(qkw.astype(BF16), v_aug)
    h = res[:, :dv] / jnp.maximum(jnp.abs(res[:, dv:]), jnp.exp2(-m_t))
    e = t - 1 if lower else 0
    bl = b_col[e:e + 1, :]
    m_new = bl + jnp.maximum(m, amax_col[e:e + 1, :])
    decay = jnp.exp2(bl + m - m_new)
    kw = (kc.astype(F32) * jnp.exp2(bl + a_col - m_new)).astype(BF16)
    return h, decay * state + _dot_tn(kw, v_aug), m_new


def _mlstm_kernel(q_ref, k_ref, v_ref, sg_ref, gt_ref, hn_ref, o_ref, hf_ref, hb_ref, bq_ref, aq_ref, mq_ref,
                  ar_ref, st_ref, *, n_chunks):
    _mlstm_gate_scans(gt_ref, (bq_ref, aq_ref, mq_ref), ar_ref)
    st_ref[...] = jnp.zeros(st_ref.shape, F32)

    def run(off, hd, m, lower, dst_ref):
        r = hd if lower else MLSTM_HEADS + hd
        slot = 2 * hd + (0 if lower else 1)
        rows = pl.ds(off, CHUNK)
        cols = slice(hd * MLSTM_HD, (hd + 1) * MLSTM_HD)
        h, state, m = _mlstm_chunk(
            q_ref[rows, cols], k_ref[rows, cols], v_ref[rows, cols],
            bq_ref[rows, r:r + 1], aq_ref[rows, r:r + 1], mq_ref[rows, r:r + 1],
            ar_ref[r:r + 1, rows], st_ref[slot], m, lower)
        st_ref[slot] = state
        dst_ref[rows, cols] = h
        return m

    def body(s, ms):
        off_f = pl.multiple_of(s * CHUNK, CHUNK)
        off_b = pl.multiple_of(jnp.where(s == 0, 0, n_chunks - s) * CHUNK, CHUNK)
        out = []
        for hd in range(MLSTM_HEADS):
            out.append(run(off_f, hd, ms[2 * hd], True, hf_ref))
            out.append(run(off_b, hd, ms[2 * hd + 1], False, hb_ref))
        return tuple(out)

    lax.fori_loop(0, n_chunks, body, tuple(jnp.zeros((1, 1), F32) for _ in range(2 * MLSTM_HEADS)))

    for hd in range(MLSTM_HEADS):
        cols = slice(hd * MLSTM_HD, (hd + 1) * MLSTM_HD)
        hm = hf_ref[:, cols] + hb_ref[:, cols]
        ms = jnp.mean(hm * hm, axis=-1, keepdims=True)
        y = hm * lax.rsqrt(ms + EPS) * hn_ref[:, cols]
        o_ref[:, cols] = (y * sg_ref[:, cols].astype(F32)).astype(BF16)


def _mlstm(q, k, v, sg, gates_t, hnorm):
    b, tok, _ = q.shape
    full = pl.BlockSpec((None, tok, MIX_A), lambda i: (i, 0, 0))
    return pl.pallas_call(
        functools.partial(_mlstm_kernel, n_chunks=tok // CHUNK),
        grid=(b,),
        in_specs=[
            full, full, full, full,
            pl.BlockSpec((N_GATES, tok), lambda i: (0, i)),
            pl.BlockSpec((1, MIX_A), lambda i: (0, 0)),
        ],
        out_specs=full,
        out_shape=jax.ShapeDtypeStruct((b, tok, MIX_A), BF16),
        scratch_shapes=[
            pltpu.VMEM((tok, MIX_A), F32), pltpu.VMEM((tok, MIX_A), F32),
            pltpu.VMEM((tok, LANES), F32), pltpu.VMEM((tok, LANES), F32), pltpu.VMEM((tok, LANES), F32),
            pltpu.VMEM((2 * MLSTM_HEADS, tok), F32),
            pltpu.VMEM((2 * MLSTM_HEADS, MLSTM_HD, 2 * MLSTM_HD), F32),
        ],
        compiler_params=_cparams("parallel"),
        name="mlstm",
    )(q, k, v, sg, gates_t, hnorm)


def _short_conv_rows(x, w, b, ctx):
    tok = x.shape[0]
    t = lax.broadcasted_iota(jnp.int32, x.shape, 0)
    prev = jnp.where((t == 0) | (t == ctx), 0.0, pltpu.roll(x, 1, 0))
    nxt = jnp.where((t == ctx - 1) | (t == tok - 1), 0.0, pltpu.roll(x, tok - 1, 0))
    return b + prev * w[0:1, :] + x * w[1:2, :] + nxt * w[2:3, :]


def _filter_kernel(emb_ref, w1_ref, b1_ref, w2_ref, b2_ref, fr_ref, w3_ref, dec_ref, o_ref, hid_ref):
    @pl.when(pl.program_id(0) == 0)
    def _():
        fr = fr_ref[...]
        h = jnp.sin(fr * (_dot(emb_ref[...].astype(BF16), w1_ref[...]) + b1_ref[...]))
        hid_ref[...] = jnp.sin(fr * (_dot(h.astype(BF16), w2_ref[...]) + b2_ref[...])).astype(BF16)

    h = _dot(hid_ref[...], w3_ref[...]) * dec_ref[...]
    o_ref[...] = h * lax.rsqrt(jnp.sum(h * h, axis=0, keepdims=True) + EPS)


def _hyena_filters(length, w1p, b1p, w2p, b2p, frp, w3p):
    bands = (HYENA_EMB - 1) // 2
    t = np.linspace(0.0, 1.0, length)[:, None]
    wpos = 2.0 * math.pi * np.arange(length) / length
    fr = np.linspace(1e-4, bands - 1, bands)
    ang = wpos[:, None] * fr[None, :]
    emb = np.concatenate([t, np.cos(ang), -np.sin(ang)], axis=-1)
    emb = np.pad(emb, ((0, 0), (0, LANES - HYENA_EMB)))
    deltas = np.abs(np.linspace(math.log(HYENA_TARGET) / HYENA_SLOW, math.log(HYENA_TARGET) / HYENA_FAST, MIX_B))
    decay = np.exp(-t * deltas[None, :])
    ngrp = w3p.shape[1] // MIX_B
    const = lambda g: (0, 0)
    return pl.pallas_call(
        _filter_kernel,
        grid=(ngrp,),
        in_specs=[
            pl.BlockSpec((length, LANES), const), pl.BlockSpec((LANES, LANES), const),
            pl.BlockSpec((1, LANES), const), pl.BlockSpec((LANES, LANES), const),
            pl.BlockSpec((1, LANES), const), pl.BlockSpec((1, LANES), const),
            pl.BlockSpec((LANES, MIX_B), lambda g: (0, g)),
            pl.BlockSpec((length, MIX_B), const),
        ],
        out_specs=pl.BlockSpec((length, MIX_B), lambda g: (0, g)),
        out_shape=jax.ShapeDtypeStruct((length, ngrp * MIX_B), F32),
        scratch_shapes=[pltpu.VMEM((length, LANES), BF16)],
        compiler_params=_cparams("arbitrary"),
        name="hyena_filters",
    )(jnp.asarray(emb, F32), w1p, b1p, w2p, b2p, frp, w3p, jnp.asarray(decay, F32))


def _dft_matrices(length):
    period = 2 * length
    k = jnp.arange(length, dtype=jnp.int32)[:, None]
    t = jnp.arange(length, dtype=jnp.int32)[None, :]
    unit = 2.0 * math.pi / period

    def table(freqs):
        ang = ((freqs[:, None] * t) & (period - 1)).astype(F32) * unit
        return jnp.cos(ang), jnp.sin(ang)

    c1, s1 = table(jnp.arange(length // DFT_SPLIT, dtype=jnp.int32) * DFT_SPLIT)
    c0, s0 = table(jnp.arange(DFT_SPLIT, dtype=jnp.int32))
    c = (c1[:, None, :] * c0[None, :, :] - s1[:, None, :] * s0[None, :, :]).reshape(length, length)
    s = (s1[:, None, :] * c0[None, :, :] + c1[:, None, :] * s0[None, :, :]).reshape(length, length)
    alt_t = jnp.where(t % 2 == 0, 1.0, -1.0).astype(F32)
    fwd = jnp.concatenate([c, jnp.where(k == 0, alt_t, -s)], axis=0)
    coef = jnp.where(t == 0, 1.0, 2.0).astype(F32) / period
    alt_k = jnp.where(k % 2 == 0, 1.0, -1.0).astype(F32)
    inv = jnp.concatenate([c * coef, jnp.where(t == 0, alt_k / period, -s * coef)], axis=1)
    return fwd.astype(BF16), inv.astype(BF16)


def _spectrum_kernel(a_ref, hf_ref, hb_ref, o_ref):
    half = pl.program_id(0)
    hf = hf_ref[...]
    hb = hb_ref[...]
    comb = jnp.where(half == 0, hf + hb, hf - hb).astype(BF16)
    o_ref[...] = _dot(a_ref[...], comb)

    @pl.when(half == 1)
    def _():
        t = lax.broadcasted_iota(jnp.int32, hf.shape, 0)
        o_ref[0:1, :] = jnp.sum(jnp.where(t % 2 == 0, hf + hb, -(hf + hb)), axis=0, keepdims=True)


def _filter_spectrum(fwd, filt):
    period, length = fwd.shape
    orders = filt.shape[1] // (2 * MIX_B)
    return pl.pallas_call(
        _spectrum_kernel,
        grid=(2, orders),
        in_specs=[
            pl.BlockSpec((length, length), lambda hlf, o: (hlf, 0)),
            pl.BlockSpec((length, MIX_B), lambda hlf, o: (0, 2 * o)),
            pl.BlockSpec((length, MIX_B), lambda hlf, o: (0, 2 * o + 1)),
        ],
        out_specs=pl.BlockSpec((length, MIX_B), lambda hlf, o: (hlf, o)),
        out_shape=jax.ShapeDtypeStruct((period, orders * MIX_B), F32),
        compiler_params=_cparams("arbitrary", "arbitrary"),
        name="hyena_filter_spectrum",
    )(fwd, filt, filt)


HY_CH = 256
BFLY_ROWS = 128


def _spectral_product(a_ref, g_ref, z, length):
    zr = _dot(a_ref[0:length, :], z)
    zi = _dot(a_ref[length:2 * length, :], z)
    gr = g_ref[0:length, :]
    gi = g_ref[length:2 * length, :]
    first = lax.broadcasted_iota(jnp.int32, zr.shape, 0) == 0
    yr = jnp.where(first, zr * gr, zr * gr - zi * gi)
    yi = jnp.where(first, zi * gi, zr * gi + zi * gr)
    return jnp.concatenate([yr, yi], axis=0).astype(BF16)


def _radix2_conv(ze, zo, a_ref, b_ref, g_ref, tw_ref):
    n = ze.shape[0]
    e = _dot(a_ref[...], ze)
    o = _dot(a_ref[...], zo)
    rep = ze.shape[1] // LANES
    f0r, f0i, f1r, f1i = [], [], [], []
    for r0 in range(0, n, BFLY_ROWS):
        rows = slice(r0, r0 + BFLY_ROWS)
        rows_im = slice(n + r0, n + r0 + BFLY_ROWS)
        er, ei, orr, oi = e[rows, :], e[rows_im, :], o[rows, :], o[rows_im, :]
        c = jnp.concatenate([tw_ref[rows, :]] * rep, axis=1)
        s = jnp.concatenate([tw_ref[rows_im, :]] * rep, axis=1)
        gar, gai, gbr, gbi = (g_ref[j * n + r0:j * n + r0 + BFLY_ROWS, :] for j in range(4))
        tr = c * orr + s * oi
        ti = c * oi - s * orr
        zar = er + tr
        zbr = er - tr
        zai = ei + ti
        zbi = ti - ei
        if r0 == 0:
            first = lax.broadcasted_iota(jnp.int32, er.shape, 0) == 0
            zai = jnp.where(first, zbr, zai)
            zbi = jnp.where(first, -oi, zbi)
            zbr = jnp.where(first, ei, zbr)
        yar = zar * gar - zai * gai
        yai = zar * gai + zai * gar
        if r0 == 0:
            yar = jnp.where(first, zar * gar, yar)
            yai = jnp.where(first, zai * gai, yai)
        ybr = zbr * gbr - zbi * gbi
        ybi = zbr * gbi + zbi * gbr
        dr = yar - ybr
        di = yai + ybi
        p0r = 0.5 * (yar + ybr)
        p0i = 0.5 * (yai - ybi)
        p1r = 0.5 * (c * dr - s * di)
        p1i = 0.5 * (c * di + s * dr)
        if r0 == 0:
            p0r = jnp.where(first, 0.5 * (yar + yai), p0r)
            p0i = jnp.where(first, ybr, p0i)
            p1r = jnp.where(first, 0.5 * (yar - yai), p1r)
            p1i = jnp.where(first, -ybi, p1i)
        f0r.append(p0r.astype(BF16))
        f0i.append(p0i.astype(BF16))
        f1r.append(p1r.astype(BF16))
        f1i.append(p1i.astype(BF16))
    y_even = _dot(b_ref[...], jnp.concatenate(f0r + f0i, axis=0))
    y_odd = _dot(b_ref[...], jnp.concatenate(f1r + f1i, axis=0))
    return y_even, y_odd


def _hyena_order_kernel(*refs, ctx, seq, conv_z):
    if conv_z:
        (z_ref, zw_ref, zb_ref, x_ref, xw_ref, xb_ref, d_ref, ah_ref, bh_ref, ac_ref, bc_ref, gl_ref, gc_ref,
         tw_ref, o_ref, zs_ref, gs_ref, os_ref) = refs
        z = _short_conv_rows(z_ref[...], zw_ref[...], zb_ref[...], ctx)
    else:
        (z_ref, x_ref, xw_ref, xb_ref, d_ref, ah_ref, bh_ref, ac_ref, bc_ref, gl_ref, gc_ref,
         tw_ref, o_ref, zs_ref, gs_ref, os_ref) = refs
        z = z_ref[...].astype(F32)
    gate = _short_conv_rows(x_ref[...], xw_ref[...], xb_ref[...], ctx)
    d = d_ref[...]
    half = seq // 2
    n_blk = zs_ref.shape[0]

    zc = z[0:ctx, :]
    yc = _dot(bc_ref[...], _spectral_product(ac_ref, gc_ref, zc.astype(BF16), ctx))
    o_ref[0:ctx, :] = (gate[0:ctx, :] * (yc + d * zc)).astype(o_ref.dtype)

    for j in range(n_blk):
        zs_ref[j] = z[ctx:, j * LANES:(j + 1) * LANES]
        gs_ref[j] = gate[ctx:, j * LANES:(j + 1) * LANES]

    def samples(ref, parity):
        return jnp.concatenate([ref[j, pl.ds(parity, half, stride=2), :] for j in range(n_blk)], axis=1)

    ze = samples(zs_ref, 0)
    zo = samples(zs_ref, 1)
    y_even, y_odd = _radix2_conv(ze.astype(BF16), zo.astype(BF16), ah_ref, bh_ref, gl_ref, tw_ref)
    for parity, y, zp in ((0, y_even, ze), (1, y_odd, zo)):
        out = samples(gs_ref, parity) * (y + d * zp)
        for j in range(n_blk):
            os_ref[j, pl.ds(parity, half, stride=2), :] = out[:, j * LANES:(j + 1) * LANES]
    for j in range(n_blk):
        o_ref[ctx:, j * LANES:(j + 1) * LANES] = os_ref[j].astype(o_ref.dtype)


def _conv_specs(blk):
    n_ch = MIX_B // HY_CH
    return [pl.BlockSpec((3, HY_CH), lambda i, j: (0, blk * n_ch + j)),
            pl.BlockSpec((1, HY_CH), lambda i, j: (0, blk * n_ch + j))]


def _hyena_order(z, conv_z, hy, xblk, conv_x, dskip, order, mats, g_lat, g_ctx, twiddle, ctx, seq, out_dtype):
    a_half, b_half, a_ctx, b_ctx = mats
    b, tok, _ = hy.shape
    n_ch = MIX_B // HY_CH
    has_z = conv_z is not None
    const = lambda i, j: (0, 0)
    return pl.pallas_call(
        functools.partial(_hyena_order_kernel, ctx=ctx, seq=seq, conv_z=has_z),
        grid=(b, n_ch),
        in_specs=[pl.BlockSpec((None, tok, HY_CH), lambda i, j: (i, 0, j))]
        + (_conv_specs(0) if has_z else [])
        + [pl.BlockSpec((None, tok, HY_CH), lambda i, j: (i, 0, xblk * n_ch + j))]
        + _conv_specs(xblk)
        + [
            pl.BlockSpec((None, 1, HY_CH), lambda i, j: (order, 0, j)),
            _resident(a_half.shape, const), _resident(b_half.shape, const),
            _resident(a_ctx.shape, const), _resident(b_ctx.shape, const),
            pl.BlockSpec((2 * seq, HY_CH), lambda i, j: (0, order * n_ch + j)),
            pl.BlockSpec((2 * ctx, HY_CH), lambda i, j: (0, order * n_ch + j)),
            _resident(twiddle.shape, const),
        ],
        out_specs=pl.BlockSpec((None, tok, HY_CH), lambda i, j: (i, 0, j)),
        out_shape=jax.ShapeDtypeStruct((b, tok, MIX_B), out_dtype),
        scratch_shapes=[pltpu.VMEM((HY_CH // LANES, seq, LANES), F32)] * 3,
        compiler_params=_cparams("parallel", "arbitrary"),
        name="hyena_order",
    )(z, *(conv_z if has_z else ()), hy, *conv_x, dskip, a_half, b_half, a_ctx, b_ctx, g_lat, g_ctx, twiddle)


def _spectrum_bfly_kernel(a_ref, hf_ref, hb_ref, o_ref):
    q = pl.program_id(0)
    hf = hf_ref[...]
    hb = hb_ref[...]
    t = lax.broadcasted_iota(jnp.int32, hf.shape, 0)
    alt = jnp.where(t % 2 == 0, 1.0, -1.0)
    comb = jnp.where(q % 2 == 0, hf + hb, hf - hb) * jnp.where(q < 2, 1.0, alt)
    res = _dot(a_ref[...], comb.astype(BF16))
    o_ref[...] = jnp.where(q == 3, -res, res)

    quarter = t % 4

    @pl.when(q == 1)
    def _():
        o_ref[0:1, :] = jnp.sum(alt * (hf + hb), axis=0, keepdims=True)

    @pl.when(q == 2)
    def _():
        w = jnp.where(quarter == 0, 1.0, jnp.where(quarter == 2, -1.0, 0.0))
        o_ref[0:1, :] = jnp.sum(w * (hf + hb), axis=0, keepdims=True)

    @pl.when(q == 3)
    def _():
        w = jnp.where(quarter == 1, -1.0, jnp.where(quarter == 3, 1.0, 0.0))
        o_ref[0:1, :] = jnp.sum(w * (hf - hb), axis=0, keepdims=True)


def _filter_spectrum_bfly(fwd, filt):
    period, length = fwd.shape
    n = length // 2
    orders = filt.shape[1] // (2 * MIX_B)
    return pl.pallas_call(
        _spectrum_bfly_kernel,
        grid=(4, orders),
        in_specs=[
            pl.BlockSpec((n, length), lambda q, o: ((q % 2) * (length // n), 0)),
            pl.BlockSpec((length, MIX_B), lambda q, o: (0, 2 * o)),
            pl.BlockSpec((length, MIX_B), lambda q, o: (0, 2 * o + 1)),
        ],
        out_specs=pl.BlockSpec((n, MIX_B), lambda q, o: (q, o)),
        out_shape=jax.ShapeDtypeStruct((period, orders * MIX_B), F32),
        compiler_params=_cparams("arbitrary", "arbitrary"),
        name="hyena_filter_spectrum_bfly",
    )(fwd, filt, filt)


def _twiddles(length):
    n = length // 2
    ang = jnp.arange(n, dtype=F32) * (2.0 * math.pi / (2 * length))
    tw = jnp.concatenate([jnp.cos(ang), jnp.sin(ang)])[:, None]
    return jnp.broadcast_to(tw, (2 * n, LANES))


def _odd_in_kernel(x_ref, shc_ref, shb_ref, scc_ref, scb_ref, w_ref, qn_ref, kn_ref, q_ref, k_ref, v_ref,
                   *, n_sub, per_batch):
    lane = lax.broadcasted_iota(jnp.int32, (SUB_ROWS, LANES), 1)
    lo = lane < NA_HD
    half = D_MODEL // 2
    for sb in range(n_sub):
        rows = _sub_rows(sb)
        h = _normmod(x_ref[rows, :], _sub_mod(shc_ref, shb_ref, sb, per_batch),
                     _sub_mod(scc_ref, scb_ref, sb, per_batch)).astype(BF16)
        for part, gain_ref, scale, dst in ((0, qn_ref, NA_HD ** -0.5 * LOG2E, q_ref), (1, kn_ref, 1.0, k_ref)):
            gain = gain_ref[...]
            for c in range(2):
                base = part * D_MODEL + c * half
                acc = _dot(h, w_ref[:, base:base + half])
                for g in range(half // LANES):
                    a = acc[:, g * LANES:(g + 1) * LANES]
                    sq = a * a
                    s_lo = jnp.sum(jnp.where(lo, sq, 0.0), axis=-1, keepdims=True)
                    s_hi = jnp.sum(jnp.where(lo, 0.0, sq), axis=-1, keepdims=True)
                    inv = lax.rsqrt(jnp.where(lo, s_lo, s_hi) * (1.0 / NA_HD) + EPS)
                    y = a * inv * gain
                    if scale != 1.0:
                        y = y * scale
                    dst[rows, c * half + g * LANES:c * half + (g + 1) * LANES] = y.astype(BF16)
        for c in range(2):
            base = 2 * D_MODEL + c * half
            v_ref[rows, c * half:(c + 1) * half] = _dot(h, w_ref[:, base:base + half]).astype(BF16)


def _odd_in_proj(x2, mod5, layer, tiles, w_qkv, qn2, kn2):
    m = x2.shape[0]
    out = jax.ShapeDtypeStruct((m, D_MODEL), BF16)
    return pl.pallas_call(
        functools.partial(_odd_in_kernel, n_sub=tiles.n_sub, per_batch=tiles.per_batch),
        grid=(tiles.n,),
        in_specs=[tiles.row_spec(D_MODEL)] + tiles.mod_specs(layer, 0) + tiles.mod_specs(layer, 1) + [
            _resident((D_MODEL, 3 * D_MODEL), lambda i: (0, 0)),
            pl.BlockSpec((1, LANES), lambda i: (0, 0)), pl.BlockSpec((1, LANES), lambda i: (0, 0)),
        ],
        out_specs=[tiles.row_spec(D_MODEL)] * 3,
        out_shape=(out, out, out),
        compiler_params=_cparams("parallel"),
        name="odd_qkv_proj",
    )(x2, mod5, mod5, mod5, mod5, w_qkv, qn2, kn2)


def _pair_queries(q):
    lane = lax.broadcasted_iota(jnp.int32, q.shape, 1)
    zero = jnp.zeros_like(q)
    return jnp.concatenate([jnp.where(lane < NA_HD, q, zero), jnp.where(lane < NA_HD, zero, q)], axis=0)


def _unpair(res, n):
    lane = lax.broadcasted_iota(jnp.int32, (n, LANES), 1)
    return jnp.where(lane < NA_HD, res[0:n, :], res[n:2 * n, :])


def _natten_kernel(q_ref, k_ref, v_ref, t_ref, o_ref, va_ref, *, ctx, rows_n, win_rows):
    tok = v_ref.shape[0]
    va_ref[:, 0:LANES] = v_ref[...]
    va_ref[:, LANES:2 * LANES] = jnp.ones((tok, LANES), BF16)
    kc = k_ref[0:ctx, :]
    vc = va_ref[0:ctx, :]

    def softmax_pv(scores, values):
        m = functools.reduce(jnp.maximum, [jnp.max(s, axis=1, keepdims=True) for s in scores])
        res = None
        for s, vblk in zip(scores, values):
            part = _dot(jnp.exp2(s - m).astype(BF16), vblk)
            res = part if res is None else res + part
        return res[:, 0:LANES] * (1.0 / res[:, LANES:2 * LANES])

    w = _pair_queries(q_ref[0:ctx, :])
    o_ref[0:ctx, :] = _unpair(softmax_pv([_dot_nt(w, kc)], [vc]), ctx).astype(BF16)

    nk = win_rows * GRID_W

    def body(r, carry):
        rs = jnp.clip(r - win_rows // 2, 0, rows_n - win_rows)
        dr0 = rs - r + NA_WIN_ROWS - 1
        q_off = pl.multiple_of(ctx + r * GRID_W, GRID_W)
        k_off = pl.multiple_of(ctx + rs * GRID_W, GRID_W)
        w = _pair_queries(q_ref[pl.ds(q_off, GRID_W), :])
        bias = t_ref[dr0 & 1, :, pl.ds(pl.multiple_of((dr0 >> 1) * LANES, LANES), nk)]
        s_lat = _dot_nt(w, k_ref[pl.ds(k_off, nk), :]) + bias
        s_ctx = _dot_nt(w, kc)
        out = softmax_pv([s_lat, s_ctx], [va_ref[pl.ds(k_off, nk), :], vc])
        o_ref[pl.ds(q_off, GRID_W), :] = _unpair(out, GRID_W).astype(BF16)
        return carry

    lax.fori_loop(0, rows_n, body, 0, unroll=min(NA_UNROLL, rows_n))


def _natten(q, k, v, table, ctx, seq):
    b, tok, _ = q.shape
    rows_n = seq // GRID_W
    assert rows_n >= NA_WIN_ROWS and rows_n % min(NA_UNROLL, rows_n) == 0
    pairs = NA_HEADS // 2
    pair_spec = pl.BlockSpec((None, tok, LANES), lambda i, p: (i, 0, p))
    return pl.pallas_call(
        functools.partial(_natten_kernel, ctx=ctx, rows_n=rows_n, win_rows=NA_WIN_ROWS),
        grid=(b, pairs),
        in_specs=[
            pair_spec, pair_spec, pair_spec,
            pl.BlockSpec((None,) + table.shape[1:], lambda i, p: (p, 0, 0, 0)),
        ],
        out_specs=pair_spec,
        out_shape=jax.ShapeDtypeStruct((b, tok, D_MODEL), BF16),
        scratch_shapes=[pltpu.VMEM((tok, 2 * LANES), BF16)],
        compiler_params=_cparams("parallel", "arbitrary"),
        name="natten",
    )(q, k, v, table)


def _natten_bias_table(rpb):
    n_dr = 2 * NA_WIN_ROWS - 1
    n_dc = 2 * NA_WIN_COLS - 1
    cidx = np.arange(GRID_W)
    cstart = np.clip(cidx - NA_WIN_COLS // 2, 0, GRID_W - NA_WIN_COLS)
    cmask = (cidx[None, :] >= cstart[:, None]) & (cidx[None, :] < cstart[:, None] + NA_WIN_COLS)
    dc = np.clip(cidx[None, :] - cidx[:, None] + NA_WIN_COLS - 1, 0, n_dc - 1)
    onehot = ((dc[None] == np.arange(n_dc)[:, None, None]) & cmask[None]).astype(np.float32)
    tb = jnp.einsum("hrc,cqk->hrqk", rpb.astype(F32), jnp.asarray(onehot), precision=lax.Precision.HIGHEST)
    tb = tb * LOG2E + jnp.asarray(np.where(cmask, 0.0, NEG_BIG), F32)
    tb = tb.reshape(NA_HEADS // 2, 2, n_dr, GRID_W, GRID_W).transpose(0, 1, 3, 2, 4)
    tb = tb.reshape(NA_HEADS // 2, 2 * GRID_W, n_dr * GRID_W)
    width = (n_dr + 1) * GRID_W
    tb = jnp.pad(tb, ((0, 0), (0, 0), (0, width + GRID_W - n_dr * GRID_W)))
    return jnp.stack([tb[:, :, 0:width], tb[:, :, GRID_W:GRID_W + width]], axis=1)


FF_CHUNK = 1024


def _mix_mlp_kernel(*refs, widths, n_sub, per_batch, latent_only):
    x_ref = refs[0]
    g1, sh, sc, g2 = (refs[1 + 2 * j:3 + 2 * j] for j in range(4))
    in_refs = refs[9:9 + len(widths)]
    wo_ref, w1_ref, w2_ref, o_ref = refs[9 + len(widths):]

    def compute(sb):
        rows = _sub_rows(sb)
        mix = None
        off = 0
        for r, wd in zip(in_refs, widths):
            part = _dot(r[rows, :].astype(BF16), wo_ref[off:off + wd, :])
            mix = part if mix is None else mix + part
            off += wd
        x = x_ref[rows, :] + _sub_mod(*g1, sb, per_batch) * mix
        h = _normmod(x, _sub_mod(*sh, sb, per_batch), _sub_mod(*sc, sb, per_batch)).astype(BF16)
        acc = None
        for c0 in range(0, D_FF, FF_CHUNK):
            a = jnp.maximum(_dot(h, w1_ref[:, c0:c0 + FF_CHUNK]), 0.0)
            part = _dot((a * a).astype(BF16), w2_ref[c0:c0 + FF_CHUNK, :])
            acc = part if acc is None else acc + part
        o_ref[rows, :] = x + _sub_mod(*g2, sb, per_batch) * acc

    if latent_only:
        assert n_sub == 1
        pl.when(pl.program_id(0) % per_batch != 0)(functools.partial(compute, 0))
    else:
        for sb in range(n_sub):
            compute(sb)


def _mix_mlp(x2, mod5, layer, tiles, parts, w_out, w1, w2, latent_only):
    widths = tuple(p.shape[1] for p in parts)
    pb = tiles.per_batch
    if latent_only:
        out_rows = tiles.batch * (tiles.tok - tiles.ctx)
        out_spec = pl.BlockSpec((tiles.rows, D_MODEL), lambda i: (i - i // pb - jnp.where(i % pb == 0, 0, 1), 0))
    else:
        out_rows = x2.shape[0]
        out_spec = tiles.row_spec(D_MODEL)
    mods = [s for which in (2, 3, 4, 5) for s in tiles.mod_specs(layer, which)]
    return pl.pallas_call(
        functools.partial(_mix_mlp_kernel, widths=widths, n_sub=tiles.n_sub, per_batch=pb, latent_only=latent_only),
        grid=(tiles.n,),
        in_specs=[tiles.row_spec(D_MODEL)] + mods
        + [tiles.row_spec(wd) for wd in widths]
        + [_resident(w_out.shape, lambda i: (0, 0)), _resident(w1.shape, lambda i: (0, 0)),
           _resident(w2.shape, lambda i: (0, 0))],
        out_specs=out_spec,
        out_shape=jax.ShapeDtypeStruct((out_rows, D_MODEL), F32),
        input_output_aliases={} if latent_only else {0: 0},
        compiler_params=_cparams("arbitrary" if latent_only else "parallel"),
        name="mix_out_mlp",
    )(x2, *([mod5] * len(mods)), *parts, w_out, w1, w2)


def _pad_cols(a, width):
    return jnp.pad(a, ((0, 0), (0, width - a.shape[1])))


def kernel(x, c, ctx, c_ctx, w_mod, b_mod, w_mlp_in, w_mlp_out, e_w_in, e_gate_b, e_hnorm, e_conv_w, e_conv_b,
           e_f_w1, e_f_b1, e_f_w2, e_f_b2, e_f_w3, e_f_freq, e_hy_d, e_w_out, o_w_qkv, o_qn, o_kn, o_rpb, o_w_out):
    batch, seq, d = x.shape
    n_ctx = ctx.shape[1]
    depth = w_mod.shape[0]
    tok = n_ctx + seq
    assert d == D_MODEL and seq % CHUNK == 0 and n_ctx == CHUNK and seq % GRID_W == 0
    tiles = _Tiles(batch, tok, n_ctx, TILE_ROWS if tok % TILE_ROWS == 0 else SUB_ROWS)
    tiles_last = _Tiles(batch, tok, n_ctx, SUB_ROWS)

    n_samp = -(-(batch + 1) // 8) * 8
    cvec = jnp.concatenate([c, c_ctx[None, :], jnp.zeros((n_samp - batch - 1, d), F32)], axis=0)
    mod5 = _modulation(cvec, w_mod, b_mod)

    cos_t, sin_t = _rope_tables(n_ctx, seq)
    a_lat, _ = _dft_matrices(seq)
    a_ctx, b_ctx = _dft_matrices(n_ctx)
    dft_mats = _dft_matrices(seq // 2) + (a_ctx, b_ctx)
    twiddle = _twiddles(seq)

    xs = jnp.concatenate([ctx, x], axis=1).reshape(batch * tok, d)

    for l in range(depth):
        i = l // 2
        if l % 2 == 0:
            w_in = e_w_in[i]
            g0 = 4 * MIX_A
            w_pad = jnp.concatenate(
                [w_in[:, :g0], _pad_cols(w_in[:, g0:g0 + N_GATES], LANES), w_in[:, g0 + N_GATES:]], axis=1).astype(BF16)
            gate_b = _pad_cols(e_gate_b[i][None, :], LANES)
            q, k, v, sg, g_t, hy = _even_in_proj(xs, mod5, l, tiles, w_pad, gate_b, cos_t, sin_t)
            as3 = lambda a: a.reshape(batch, tok, a.shape[-1])
            a_out = _mlstm(as3(q), as3(k), as3(v), as3(sg), g_t, e_hnorm[i][None, :])

            hy3 = as3(hy)
            conv = (e_conv_w[i], e_conv_b[i][None, :])
            pad2 = lambda a: jnp.pad(a, ((0, LANES - a.shape[0]), (0, LANES - a.shape[1])))
            w1p = pad2(e_f_w1[i]).astype(BF16)
            w2p = pad2(e_f_w2[i]).astype(BF16)
            w3p = jnp.pad(e_f_w3[i], ((0, LANES - HYENA_FFN), (0, 0))).astype(BF16)
            b1p = _pad_cols(e_f_b1[i][None, :], LANES)
            b2p = _pad_cols(e_f_b2[i][None, :], LANES)
            frp = _pad_cols(e_f_freq[i][None, :], LANES)
            g_lat = _filter_spectrum_bfly(a_lat, _hyena_filters(seq, w1p, b1p, w2p, b2p, frp, w3p))
            g_ctx = _filter_spectrum(a_ctx, _hyena_filters(n_ctx, w1p, b1p, w2p, b2p, frp, w3p))
            dskip = e_hy_d[i][:, None, :]
            n_ord = e_hy_d.shape[1]
            z, conv_z = hy3, conv
            for o in range(n_ord):
                z = _hyena_order(z, conv_z, hy3, 1 + o, conv, dskip, o, dft_mats, g_lat, g_ctx, twiddle,
                                 n_ctx, seq, F32)
                conv_z = None
            parts = [a_out.reshape(batch * tok, MIX_A), z.reshape(batch * tok, MIX_B)]
            w_out = e_w_out[i].astype(BF16)
        else:
            rep = LANES // NA_HD
            qn2 = jnp.tile(o_qn[i], rep)[None, :]
            kn2 = jnp.tile(o_kn[i], rep)[None, :]
            q, k, v = _odd_in_proj(xs, mod5, l, tiles, o_w_qkv[i].astype(BF16), qn2, kn2)
            as3 = lambda a: a.reshape(batch, tok, D_MODEL)
            table = _natten_bias_table(o_rpb[i])
            att = _natten(as3(q), as3(k), as3(v), table, n_ctx, seq)
            parts = [att.reshape(batch * tok, D_MODEL)]
            w_out = o_w_out[i].astype(BF16)
        last = l == depth - 1
        xs = _mix_mlp(xs, mod5, l, tiles_last if last else tiles, parts, w_out,
                      w_mlp_in[l].astype(BF16), w_mlp_out[l].astype(BF16), latent_only=last)

    return xs.reshape(batch, seq, d)
```

```python
import functools
import math

import numpy as np
import jax
import jax.numpy as jnp
from jax import lax
from jax.experimental import pallas as pl
from jax.experimental.pallas import tpu as pltpu

F32 = jnp.float32
BF16 = jnp.bfloat16

D_MODEL = 1024
D_FF = 4 * D_MODEL
EPS = 1e-6
ROPE_BASE = 10000.0
GRID_W = 64
MIX_A = D_MODEL // 2
MIX_B = D_MODEL - MIX_A
MLSTM_HEADS = 4
MLSTM_HD = MIX_A // MLSTM_HEADS
N_GATES = 4 * MLSTM_HEADS
HYENA_EMB = 33
HYENA_FFN = 64
HYENA_TARGET = 1e-2
HYENA_FAST = 0.3
HYENA_SLOW = 1.5
NA_HEADS = 16
NA_HD = D_MODEL // NA_HEADS
NA_WIN_ROWS = 8
NA_WIN_COLS = 16

LANES = 128
SUB_ROWS = 256
TILE_ROWS = 768
CHUNK = 256
VMEM_LIMIT = 56 * 1024 * 1024
NEG_BIG = -1e30
LOG2E = math.log2(math.e)
DFT_SPLIT = 64
NA_UNROLL = 32


def _cparams(*sem):
    return pltpu.CompilerParams(dimension_semantics=sem, vmem_limit_bytes=VMEM_LIMIT)


def _dot(a, b):
    return jnp.dot(a, b, preferred_element_type=F32)


def _dot_nt(a, b):
    return lax.dot_general(a, b, (((1,), (1,)), ((), ())), preferred_element_type=F32)


def _dot_tn(a, b):
    return lax.dot_general(a, b, (((0,), (0,)), ((), ())), preferred_element_type=F32)


def _resident(shape, index_map):
    return pl.BlockSpec(shape, index_map, pipeline_mode=pl.Buffered(1))


def _normmod(x, sh, sc):
    ms = jnp.mean(x * x, axis=-1, keepdims=True)
    return (x * lax.rsqrt(ms + EPS)) * (1.0 + sc) + sh


def _sigmoid(x):
    return 1.0 / (1.0 + jnp.exp(-x))


def _log_sigmoid(x):
    return jnp.minimum(x, 0.0) - jnp.log(1.0 + jnp.exp(-jnp.abs(x)))


def _mod_kernel(c_ref, w_ref, b_ref, o_ref):
    c = c_ref[...]
    s = (c * _sigmoid(c)).astype(BF16)
    o_ref[...] = _dot(s, w_ref[...].astype(BF16)) + b_ref[...]


def _modulation(cvec, w_mod, b_mod):
    depth, d, d6 = w_mod.shape
    ns = cvec.shape[0]
    nj = d6 // d
    out = pl.pallas_call(
        _mod_kernel,
        grid=(depth, nj),
        in_specs=[
            pl.BlockSpec((ns, d), lambda l, j: (0, 0)),
            pl.BlockSpec((None, d, d), lambda l, j: (l, 0, j)),
            pl.BlockSpec((None, 1, d), lambda l, j: (l, 0, j)),
        ],
        out_specs=pl.BlockSpec((None, ns, d), lambda l, j: (l, 0, j)),
        out_shape=jax.ShapeDtypeStruct((depth, ns, d6), F32),
        compiler_params=_cparams("arbitrary", "arbitrary"),
        name="modulation",
    )(cvec, w_mod, b_mod.reshape(depth, 1, d6))
    return out.reshape(depth, ns, nj, 1, d)


class _Tiles:
    def __init__(self, batch, tok, ctx, rows):
        assert ctx == SUB_ROWS and rows % SUB_ROWS == 0 and tok % rows == 0
        self.batch, self.tok, self.ctx, self.rows = batch, tok, ctx, rows
        self.n_sub = rows // SUB_ROWS
        self.per_batch = tok // rows
        self.n = batch * self.per_batch

    def mod_specs(self, layer, which):
        blk = (None, None, None, 1, D_MODEL)
        return [pl.BlockSpec(blk, lambda i: (layer, self.batch, which, 0, 0)),
                pl.BlockSpec(blk, lambda i: (layer, i // self.per_batch, which, 0, 0))]

    def row_spec(self, width):
        return pl.BlockSpec((self.rows, width), lambda i: (i, 0))

    def pos_spec(self, width):
        return pl.BlockSpec((self.rows, width), lambda i: (i % self.per_batch, 0))


def _sub_mod(c_ref, b_ref, sb, per_batch):
    if sb > 0:
        return b_ref[...]
    return jnp.where(pl.program_id(0) % per_batch == 0, c_ref[...], b_ref[...])


def _sub_rows(sb):
    return slice(sb * SUB_ROWS, (sb + 1) * SUB_ROWS)


def _even_in_kernel(x_ref, shc_ref, shb_ref, scc_ref, scb_ref, w_ref, gb_ref, cos_ref, sin_ref,
                    q_ref, k_ref, v_ref, sg_ref, gt_ref, hy_ref, *, n_sub, per_batch):
    lane = lax.broadcasted_iota(jnp.int32, (SUB_ROWS, MLSTM_HD), 1)
    first = (lane % (MLSTM_HD // 2)) < (MLSTM_HD // 4)
    g0 = 4 * MIX_A
    h0 = g0 + LANES
    for sb in range(n_sub):
        rows = _sub_rows(sb)
        h = _normmod(x_ref[rows, :], _sub_mod(shc_ref, shb_ref, sb, per_batch),
                     _sub_mod(scc_ref, scb_ref, sb, per_batch)).astype(BF16)
        cos = cos_ref[rows, :]
        sin = sin_ref[rows, :]

        def rope(a):
            part = jnp.where(first, pltpu.roll(a, LANES - MLSTM_HD // 4, 1), pltpu.roll(a, MLSTM_HD // 4, 1))
            return a * cos + part * sin

        acc = _dot(h, w_ref[:, 0:MIX_A])
        for hd in range(MLSTM_HEADS):
            sl = slice(hd * MLSTM_HD, (hd + 1) * MLSTM_HD)
            q_ref[rows, sl] = rope(acc[:, sl]).astype(BF16)
        acc = _dot(h, w_ref[:, MIX_A:2 * MIX_A]) * (MLSTM_HD ** -0.5)
        for hd in range(MLSTM_HEADS):
            sl = slice(hd * MLSTM_HD, (hd + 1) * MLSTM_HD)
            k_ref[rows, sl] = rope(acc[:, sl]).astype(BF16)
        v_ref[rows, :] = _dot(h, w_ref[:, 2 * MIX_A:3 * MIX_A]).astype(BF16)
        sg_ref[rows, :] = _sigmoid(_dot(h, w_ref[:, 3 * MIX_A:4 * MIX_A])).astype(BF16)
        gt_ref[:, rows] = (_dot(h, w_ref[:, g0:g0 + LANES]) + gb_ref[...]).T
        for j in range(3):
            hy_ref[rows, j * MIX_B:(j + 1) * MIX_B] = _dot(h, w_ref[:, h0 + j * MIX_B:h0 + (j + 1) * MIX_B])


def _even_in_proj(x2, mod5, layer, tiles, w_pad, gate_b_pad, cos_t, sin_t):
    m = x2.shape[0]
    n_w = w_pad.shape[1]
    outs = (
        jax.ShapeDtypeStruct((m, MIX_A), BF16), jax.ShapeDtypeStruct((m, MIX_A), BF16),
        jax.ShapeDtypeStruct((m, MIX_A), BF16), jax.ShapeDtypeStruct((m, MIX_A), BF16),
        jax.ShapeDtypeStruct((LANES, m), F32), jax.ShapeDtypeStruct((m, 3 * MIX_B), F32),
    )
    return pl.pallas_call(
        functools.partial(_even_in_kernel, n_sub=tiles.n_sub, per_batch=tiles.per_batch),
        grid=(tiles.n,),
        in_specs=[tiles.row_spec(D_MODEL)] + tiles.mod_specs(layer, 0) + tiles.mod_specs(layer, 1) + [
            _resident((D_MODEL, n_w), lambda i: (0, 0)),
            pl.BlockSpec((1, LANES), lambda i: (0, 0)),
            tiles.pos_spec(MLSTM_HD), tiles.pos_spec(MLSTM_HD),
        ],
        out_specs=[tiles.row_spec(MIX_A)] * 4
        + [pl.BlockSpec((LANES, tiles.rows), lambda i: (0, i)), tiles.row_spec(3 * MIX_B)],
        out_shape=outs,
        compiler_params=_cparams("parallel"),
        name="even_in_proj",
    )(x2, mod5, mod5, mod5, mod5, w_pad, gate_b_pad, cos_t, sin_t)


def _rope_tables(ctx, seq):
    half = MLSTM_HD // 2
    nf = half // 2
    inv = ROPE_BASE ** (-np.arange(nf, dtype=np.float64) / nf)
    t = np.arange(seq)
    rows, cols = t // GRID_W, t % GRID_W

    def one(pos):
        ang = pos[:, None].astype(np.float64) * inv[None, :]
        c = np.concatenate([np.cos(ang), np.cos(ang)], axis=-1)
        s = np.concatenate([-np.sin(ang), np.sin(ang)], axis=-1)
        return c, s

    cr, sr = one(rows)
    cc, sc = one(cols)
    cos = np.concatenate([cr, cc], axis=-1)
    sin = np.concatenate([sr, sc], axis=-1)
    cos = np.concatenate([np.ones((ctx, MLSTM_HD)), cos], axis=0)
    sin = np.concatenate([np.zeros((ctx, MLSTM_HD)), sin], axis=0)
    return jnp.asarray(cos, F32), jnp.asarray(sin, F32)


def _seg_scan(y, pos, op, reverse, axis):
    n = y.shape[axis]
    k = 1
    while k < CHUNK:
        if reverse:
            y = jnp.where(pos < CHUNK - k, op(y, pltpu.roll(y, n - k, axis)), y)
        else:
            y = jnp.where(pos >= k, op(y, pltpu.roll(y, k, axis)), y)
        k *= 2
    return y


def _mlstm_gate_scans(gt_ref, qc_refs, ar_ref):
    hh = MLSTM_HEADS
    g = gt_ref[...]
    gi = jnp.concatenate([g[0:hh], g[2 * hh:3 * hh]], axis=0) * LOG2E
    lf = _log_sigmoid(jnp.concatenate([g[hh:2 * hh], g[3 * hh:4 * hh]], axis=0)) * LOG2E
    fwd = lax.broadcasted_iota(jnp.int32, lf.shape, 0) < hh
    pos = lax.broadcasted_iota(jnp.int32, lf.shape, 1) % CHUNK
    b = jnp.where(fwd, _seg_scan(lf, pos, jnp.add, False, 1), _seg_scan(lf, pos, jnp.add, True, 1))
    a = gi - b
    amax = jnp.where(fwd, _seg_scan(a, pos, jnp.maximum, False, 1), _seg_scan(a, pos, jnp.maximum, True, 1))
    ar_ref[...] = a
    stack = jnp.concatenate([b, a, amax, jnp.zeros_like(b)], axis=0)
    hi = stack.astype(BF16)
    rest = stack - hi.astype(F32)
    mid = rest.astype(BF16)
    lo = (rest - mid.astype(F32)).astype(BF16)
    n = stack.shape[0]
    nq = b.shape[0]
    eye = (lax.broadcasted_iota(jnp.int32, (n, LANES), 0) == lax.broadcasted_iota(jnp.int32, (n, LANES), 1))
    eye = eye.astype(F32).astype(BF16)
    cols = (_dot_tn(hi, eye) + _dot_tn(mid, eye)) + _dot_tn(lo, eye)
    for j, ref in enumerate(qc_refs):
        ref[...] = cols if j == 0 else pltpu.roll(cols, LANES - j * nq, 1)


def _mlstm_chunk(qc, kc, vc, b_col, a_col, amax_col, a_row, state, m, lower):
    t, dv = vc.shape
    r = lax.broadcasted_iota(jnp.int32, (t, t), 0)
    c = lax.broadcasted_iota(jnp.int32, (t, t), 1)
    incl = (c <= r) if lower else (c >= r)
    mm = jnp.maximum(m, amax_col)
    m_t = b_col + mm
    w = jnp.exp2(jnp.where(incl, a_row - mm, -jnp.inf))
    sc = jnp.exp2(m - mm)
    qkw = _dot_nt(qc, kc) * w
    v_aug = jnp.concatenate([vc, jnp.ones_like(vc)], axis=1)
    res = sc * _dot(qc, state.astype(BF16)) + _dot(qkw.astype(BF16), v_aug)
    h = res[:, :dv] / jnp.maximum(jnp.abs(res[:, dv:]), jnp.exp2(-m_t))
    e = t - 1 if lower else 0
    bl = b_col[e:e + 1, :]
    m_new = bl + jnp.maximum(m, amax_col[e:e + 1, :])
    decay = jnp.exp2(bl + m - m_new)
    kw = (kc.astype(F32) * jnp.exp2(bl + a_col - m_new)).astype(BF16)
    return h, decay * state + _dot_tn(kw, v_aug), m_new


def _mlstm_kernel(q_ref, k_ref, v_ref, sg_ref, gt_ref, hn_ref, o_ref, hf_ref, hb_ref, bq_ref, aq_ref, mq_ref,
                  ar_ref, st_ref, *, n_chunks):
    _mlstm_gate_scans(gt_ref, (bq_ref, aq_ref, mq_ref), ar_ref)
    st_ref[...] = jnp.zeros(st_ref.shape, F32)

    def run(off, hd, m, lower, dst_ref):
        r = hd if lower else MLSTM_HEADS + hd
        slot = 2 * hd + (0 if lower else 1)
        rows = pl.ds(off, CHUNK)
        cols = slice(hd * MLSTM_HD, (hd + 1) * MLSTM_HD)
        h, state, m = _mlstm_chunk(
            q_ref[rows, cols], k_ref[rows, cols], v_ref[rows, cols],
            bq_ref[rows, r:r + 1], aq_ref[rows, r:r + 1], mq_ref[rows, r:r + 1],
            ar_ref[r:r + 1, rows], st_ref[slot], m, lower)
        st_ref[slot] = state
        dst_ref[rows, cols] = h
        return m

    def body(s, ms):
        off_f = pl.multiple_of(s * CHUNK, CHUNK)
        off_b = pl.multiple_of(jnp.where(s == 0, 0, n_chunks - s) * CHUNK, CHUNK)
        out = []
        for hd in range(MLSTM_HEADS):
            out.append(run(off_f, hd, ms[2 * hd], True, hf_ref))
            out.append(run(off_b, hd, ms[2 * hd + 1], False, hb_ref))
        return tuple(out)

    lax.fori_loop(0, n_chunks, body, tuple(jnp.zeros((1, 1), F32) for _ in range(2 * MLSTM_HEADS)))

    for hd in range(MLSTM_HEADS):
        cols = slice(hd * MLSTM_HD, (hd + 1) * MLSTM_HD)
        hm = hf_ref[:, cols] + hb_ref[:, cols]
        ms = jnp.mean(hm * hm, axis=-1, keepdims=True)
        y = hm * lax.rsqrt(ms + EPS) * hn_ref[:, cols]
        o_ref[:, cols] = (y * sg_ref[:, cols].astype(F32)).astype(BF16)


def _mlstm(q, k, v, sg, gates_t, hnorm):
    b, tok, _ = q.shape
    full = pl.BlockSpec((None, tok, MIX_A), lambda i: (i, 0, 0))
    return pl.pallas_call(
        functools.partial(_mlstm_kernel, n_chunks=tok // CHUNK),
        grid=(b,),
        in_specs=[
            full, full, full, full,
            pl.BlockSpec((N_GATES, tok), lambda i: (0, i)),
            pl.BlockSpec((1, MIX_A), lambda i: (0, 0)),
        ],
        out_specs=full,
        out_shape=jax.ShapeDtypeStruct((b, tok, MIX_A), BF16),
        scratch_shapes=[
            pltpu.VMEM((tok, MIX_A), F32), pltpu.VMEM((tok, MIX_A), F32),
            pltpu.VMEM((tok, LANES), F32), pltpu.VMEM((tok, LANES), F32), pltpu.VMEM((tok, LANES), F32),
            pltpu.VMEM((2 * MLSTM_HEADS, tok), F32),
            pltpu.VMEM((2 * MLSTM_HEADS, MLSTM_HD, 2 * MLSTM_HD), F32),
        ],
        compiler_params=_cparams("parallel"),
        name="mlstm",
    )(q, k, v, sg, gates_t, hnorm)


def _short_conv_rows(x, w, b, ctx):
    tok = x.shape[0]
    t = lax.broadcasted_iota(jnp.int32, x.shape, 0)
    prev = jnp.where((t == 0) | (t == ctx), 0.0, pltpu.roll(x, 1, 0))
    nxt = jnp.where((t == ctx - 1) | (t == tok - 1), 0.0, pltpu.roll(x, tok - 1, 0))
    return b + prev * w[0:1, :] + x * w[1:2, :] + nxt * w[2:3, :]


def _filter_kernel(emb_ref, w1_ref, b1_ref, w2_ref, b2_ref, fr_ref, w3_ref, dec_ref, o_ref, hid_ref):
    @pl.when(pl.program_id(0) == 0)
    def _():
        fr = fr_ref[...]
        h = jnp.sin(fr * (_dot(emb_ref[...].astype(BF16), w1_ref[...]) + b1_ref[...]))
        hid_ref[...] = jnp.sin(fr * (_dot(h.astype(BF16), w2_ref[...]) + b2_ref[...])).astype(BF16)

    h = _dot(hid_ref[...], w3_ref[...]) * dec_ref[...]
    o_ref[...] = h * lax.rsqrt(jnp.sum(h * h, axis=0, keepdims=True) + EPS)


def _hyena_filters(length, w1p, b1p, w2p, b2p, frp, w3p):
    bands = (HYENA_EMB - 1) // 2
    t = np.linspace(0.0, 1.0, length)[:, None]
    wpos = 2.0 * math.pi * np.arange(length) / length
    fr = np.linspace(1e-4, bands - 1, bands)
    ang = wpos[:, None] * fr[None, :]
    emb = np.concatenate([t, np.cos(ang), -np.sin(ang)], axis=-1)
    emb = np.pad(emb, ((0, 0), (0, LANES - HYENA_EMB)))
    deltas = np.abs(np.linspace(math.log(HYENA_TARGET) / HYENA_SLOW, math.log(HYENA_TARGET) / HYENA_FAST, MIX_B))
    decay = np.exp(-t * deltas[None, :])
    ngrp = w3p.shape[1] // MIX_B
    const = lambda g: (0, 0)
    return pl.pallas_call(
        _filter_kernel,
        grid=(ngrp,),
        in_specs=[
            pl.BlockSpec((length, LANES), const), pl.BlockSpec((LANES, LANES), const),
            pl.BlockSpec((1, LANES), const), pl.BlockSpec((LANES, LANES), const),
            pl.BlockSpec((1, LANES), const), pl.BlockSpec((1, LANES), const),
            pl.BlockSpec((LANES, MIX_B), lambda g: (0, g)),
            pl.BlockSpec((length, MIX_B), const),
        ],
        out_specs=pl.BlockSpec((length, MIX_B), lambda g: (0, g)),
        out_shape=jax.ShapeDtypeStruct((length, ngrp * MIX_B), F32),
        scratch_shapes=[pltpu.VMEM((length, LANES), BF16)],
        compiler_params=_cparams("arbitrary"),
        name="hyena_filters",
    )(jnp.asarray(emb, F32), w1p, b1p, w2p, b2p, frp, w3p, jnp.asarray(decay, F32))


def _dft_matrices(length):
    period = 2 * length
    k = jnp.arange(length, dtype=jnp.int32)[:, None]
    t = jnp.arange(length, dtype=jnp.int32)[None, :]
    unit = 2.0 * math.pi / period

    def table(freqs):
        ang = ((freqs[:, None] * t) & (period - 1)).astype(F32) * unit
        return jnp.cos(ang), jnp.sin(ang)

    c1, s1 = table(jnp.arange(length // DFT_SPLIT, dtype=jnp.int32) * DFT_SPLIT)
    c0, s0 = table(jnp.arange(DFT_SPLIT, dtype=jnp.int32))
    c = (c1[:, None, :] * c0[None, :, :] - s1[:, None, :] * s0[None, :, :]).reshape(length, length)
    s = (s1[:, None, :] * c0[None, :, :] + c1[:, None, :] * s0[None, :, :]).reshape(length, length)
    alt_t = jnp.where(t % 2 == 0, 1.0, -1.0).astype(F32)
    fwd = jnp.concatenate([c, jnp.where(k == 0, alt_t, -s)], axis=0)
    coef = jnp.where(t == 0, 1.0, 2.0).astype(F32) / period
    alt_k = jnp.where(k % 2 == 0, 1.0, -1.0).astype(F32)
    inv = jnp.concatenate([c * coef, jnp.where(t == 0, alt_k / period, -s * coef)], axis=1)
    return fwd.astype(BF16), inv.astype(BF16)


def _spectrum_kernel(a_ref, hf_ref, hb_ref, o_ref):
    half = pl.program_id(0)
    hf = hf_ref[...]
    hb = hb_ref[...]
    comb = jnp.where(half == 0, hf + hb, hf - hb).astype(BF16)
    o_ref[...] = _dot(a_ref[...], comb)

    @pl.when(half == 1)
    def _():
        t = lax.broadcasted_iota(jnp.int32, hf.shape, 0)
        o_ref[0:1, :] = jnp.sum(jnp.where(t % 2 == 0, hf + hb, -(hf + hb)), axis=0, keepdims=True)


def _filter_spectrum(fwd, filt):
    period, length = fwd.shape
    orders = filt.shape[1] // (2 * MIX_B)
    return pl.pallas_call(
        _spectrum_kernel,
        grid=(2, orders),
        in_specs=[
            pl.BlockSpec((length, length), lambda hlf, o: (hlf, 0)),
            pl.BlockSpec((length, MIX_B), lambda hlf, o: (0, 2 * o)),
            pl.BlockSpec((length, MIX_B), lambda hlf, o: (0, 2 * o + 1)),
        ],
        out_specs=pl.BlockSpec((length, MIX_B), lambda hlf, o: (hlf, o)),
        out_shape=jax.ShapeDtypeStruct((period, orders * MIX_B), F32),
        compiler_params=_cparams("arbitrary", "arbitrary"),
        name="hyena_filter_spectrum",
    )(fwd, filt, filt)


HY_CH = 256
BFLY_ROWS = 128


def _spectral_product(a_ref, g_ref, z, length):
    zr = _dot(a_ref[0:length, :], z)
    zi = _dot(a_ref[length:2 * length, :], z)
    gr = g_ref[0:length, :]
    gi = g_ref[length:2 * length, :]
    first = lax.broadcasted_iota(jnp.int32, zr.shape, 0) == 0
    yr = jnp.where(first, zr * gr, zr * gr - zi * gi)
    yi = jnp.where(first, zi * gi, zr * gi + zi * gr)
    return jnp.concatenate([yr, yi], axis=0).astype(BF16)


def _radix2_conv(ze, zo, a_ref, b_ref, g_ref, tw_ref):
    n = ze.shape[0]
    e = _dot(a_ref[...], ze)
    o = _dot(a_ref[...], zo)
    rep = ze.shape[1] // LANES
    f0r, f0i, f1r, f1i = [], [], [], []
    for r0 in range(0, n, BFLY_ROWS):
        rows = slice(r0, r0 + BFLY_ROWS)
        rows_im = slice(n + r0, n + r0 + BFLY_ROWS)
        er, ei, orr, oi = e[rows, :], e[rows_im, :], o[rows, :], o[rows_im, :]
        c = jnp.concatenate([tw_ref[rows, :]] * rep, axis=1)
        s = jnp.concatenate([tw_ref[rows_im, :]] * rep, axis=1)
        gar, gai, gbr, gbi = (g_ref[j * n + r0:j * n + r0 + BFLY_ROWS, :] for j in range(4))
        tr = c * orr + s * oi
        ti = c * oi - s * orr
        zar = er + tr
        zbr = er - tr
        zai = ei + ti
        zbi = ti - ei
        if r0 == 0:
            first = lax.broadcasted_iota(jnp.int32, er.shape, 0) == 0
            zai = jnp.where(first, zbr, zai)
            zbi = jnp.where(first, -oi, zbi)
            zbr = jnp.where(first, ei, zbr)
        yar = zar * gar - zai * gai
        yai = zar * gai + zai * gar
        if r0 == 0:
            yar = jnp.where(first, zar * gar, yar)
            yai = jnp.where(first, zai * gai, yai)
        ybr = zbr * gbr - zbi * gbi
        ybi = zbr * gbi + zbi * gbr
        dr = yar - ybr
        di = yai + ybi
        p0r = 0.5 * (yar + ybr)
        p0i = 0.5 * (yai - ybi)
        p1r = 0.5 * (c * dr - s * di)
        p1i = 0.5 * (c * di + s * dr)
        if r0 == 0:
            p0r = jnp.where(first, 0.5 * (yar + yai), p0r)
            p0i = jnp.where(first, ybr, p0i)
            p1r = jnp.where(first, 0.5 * (yar - yai), p1r)
            p1i = jnp.where(first, -ybi, p1i)
        f0r.append(p0r.astype(BF16))
        f0i.append(p0i.astype(BF16))
        f1r.append(p1r.astype(BF16))
        f1i.append(p1i.astype(BF16))
    y_even = _dot(b_ref[...], jnp.concatenate(f0r + f0i, axis=0))
    y_odd = _dot(b_ref[...], jnp.concatenate(f1r + f1i, axis=0))
    return y_even, y_odd


def _hyena_order_kernel(*refs, ctx, seq, conv_z):
    if conv_z:
        (z_ref, zw_ref, zb_ref, x_ref, xw_ref, xb_ref, d_ref, ah_ref, bh_ref, ac_ref, bc_ref, gl_ref, gc_ref,
         tw_ref, o_ref, zs_ref, gs_ref, os_ref) = refs
        z = _short_conv_rows(z_ref[...], zw_ref[...], zb_ref[...], ctx)
    else:
        (z_ref, x_ref, xw_ref, xb_ref, d_ref, ah_ref, bh_ref, ac_ref, bc_ref, gl_ref, gc_ref,
         tw_ref, o_ref, zs_ref, gs_ref, os_ref) = refs
        z = z_ref[...].astype(F32)
    gate = _short_conv_rows(x_ref[...], xw_ref[...], xb_ref[...], ctx)
    d = d_ref[...]
    half = seq // 2
    n_blk = zs_ref.shape[0]

    zc = z[0:ctx, :]
    yc = _dot(bc_ref[...], _spectral_product(ac_ref, gc_ref, zc.astype(BF16), ctx))
    o_ref[0:ctx, :] = (gate[0:ctx, :] * (yc + d * zc)).astype(o_ref.dtype)

    for j in range(n_blk):
        zs_ref[j] = z[ctx:, j * LANES:(j + 1) * LANES]
        gs_ref[j] = gate[ctx:, j * LANES:(j + 1) * LANES]

    def samples(ref, parity):
        return jnp.concatenate([ref[j, pl.ds(parity, half, stride=2), :] for j in range(n_blk)], axis=1)

    ze = samples(zs_ref, 0)
    zo = samples(zs_ref, 1)
    y_even, y_odd = _radix2_conv(ze.astype(BF16), zo.astype(BF16), ah_ref, bh_ref, gl_ref, tw_ref)
    for parity, y, zp in ((0, y_even, ze), (1, y_odd, zo)):
        out = samples(gs_ref, parity) * (y + d * zp)
        for j in range(n_blk):
            os_ref[j, pl.ds(parity, half, stride=2), :] = out[:, j * LANES:(j + 1) * LANES]
    for j in range(n_blk):
        o_ref[ctx:, j * LANES:(j + 1) * LANES] = os_ref[j].astype(o_ref.dtype)


def _conv_specs(blk):
    n_ch = MIX_B // HY_CH
    return [pl.BlockSpec((3, HY_CH), lambda j, i: (0, blk * n_ch + j)),
            pl.BlockSpec((1, HY_CH), lambda j, i: (0, blk * n_ch + j))]


def _hyena_order(z, conv_z, hy, xblk, conv_x, dskip, order, mats, g_lat, g_ctx, twiddle, ctx, seq, out_dtype):
    a_half, b_half, a_ctx, b_ctx = mats
    b, tok, _ = hy.shape
    n_ch = MIX_B // HY_CH
    has_z = conv_z is not None
    const = lambda j, i: (0, 0)
    return pl.pallas_call(
        functools.partial(_hyena_order_kernel, ctx=ctx, seq=seq, conv_z=has_z),
        grid=(n_ch, b),
        in_specs=[pl.BlockSpec((None, tok, HY_CH), lambda j, i: (i, 0, j))]
        + (_conv_specs(0) if has_z else [])
        + [pl.BlockSpec((None, tok, HY_CH), lambda j, i: (i, 0, xblk * n_ch + j))]
        + _conv_specs(xblk)
        + [
            pl.BlockSpec((None, 1, HY_CH), lambda j, i: (order, 0, j)),
            _resident(a_half.shape, const), _resident(b_half.shape, const),
            _resident(a_ctx.shape, const), _resident(b_ctx.shape, const),
            pl.BlockSpec((2 * seq, HY_CH), lambda j, i: (0, order * n_ch + j)),
            pl.BlockSpec((2 * ctx, HY_CH), lambda j, i: (0, order * n_ch + j)),
            _resident(twiddle.shape, const),
        ],
        out_specs=pl.BlockSpec((None, tok, HY_CH), lambda j, i: (i, 0, j)),
        out_shape=jax.ShapeDtypeStruct((b, tok, MIX_B), out_dtype),
        scratch_shapes=[pltpu.VMEM((HY_CH // LANES, seq, LANES), F32)] * 3,
        compiler_params=_cparams("parallel", "parallel"),
        name="hyena_order",
    )(z, *(conv_z if has_z else ()), hy, *conv_x, dskip, a_half, b_half, a_ctx, b_ctx, g_lat, g_ctx, twiddle)


def _spectrum_bfly_kernel(a_ref, hf_ref, hb_ref, o_ref):
    q = pl.program_id(0)
    hf = hf_ref[...]
    hb = hb_ref[...]
    t = lax.broadcasted_iota(jnp.int32, hf.shape, 0)
    alt = jnp.where(t % 2 == 0, 1.0, -1.0)
    comb = jnp.where(q % 2 == 0, hf + hb, hf - hb) * jnp.where(q < 2, 1.0, alt)
    res = _dot(a_ref[...], comb.astype(BF16))
    o_ref[...] = jnp.where(q == 3, -res, res)

    quarter = t % 4

    @pl.when(q == 1)
    def _():
        o_ref[0:1, :] = jnp.sum(alt * (hf + hb), axis=0, keepdims=True)

    @pl.when(q == 2)
    def _():
        w = jnp.where(quarter == 0, 1.0, jnp.where(quarter == 2, -1.0, 0.0))
        o_ref[0:1, :] = jnp.sum(w * (hf + hb), axis=0, keepdims=True)

    @pl.when(q == 3)
    def _():
        w = jnp.where(quarter == 1, -1.0, jnp.where(quarter == 3, 1.0, 0.0))
        o_ref[0:1, :] = jnp.sum(w * (hf - hb), axis=0, keepdims=True)


def _filter_spectrum_bfly(fwd, filt):
    period, length = fwd.shape
    n = length // 2
    orders = filt.shape[1] // (2 * MIX_B)
    return pl.pallas_call(
        _spectrum_bfly_kernel,
        grid=(4, orders),
        in_specs=[
            pl.BlockSpec((n, length), lambda q, o: ((q % 2) * (length // n), 0)),
            pl.BlockSpec((length, MIX_B), lambda q, o: (0, 2 * o)),
            pl.BlockSpec((length, MIX_B), lambda q, o: (0, 2 * o + 1)),
        ],
        out_specs=pl.BlockSpec((n, MIX_B), lambda q, o: (q, o)),
        out_shape=jax.ShapeDtypeStruct((period, orders * MIX_B), F32),
        compiler_params=_cparams("arbitrary", "arbitrary"),
        name="hyena_filter_spectrum_bfly",
    )(fwd, filt, filt)


def _twiddles(length):
    n = length // 2
    ang = jnp.arange(n, dtype=F32) * (2.0 * math.pi / (2 * length))
    tw = jnp.concatenate([jnp.cos(ang), jnp.sin(ang)])[:, None]
    return jnp.broadcast_to(tw, (2 * n, LANES))


def _odd_in_kernel(x_ref, shc_ref, shb_ref, scc_ref, scb_ref, w_ref, qn_ref, kn_ref, q_ref, k_ref, v_ref,
                   *, n_sub, per_batch):
    lane = lax.broadcasted_iota(jnp.int32, (SUB_ROWS, LANES), 1)
    lo = lane < NA_HD
    half = D_MODEL // 2
    for sb in range(n_sub):
        rows = _sub_rows(sb)
        h = _normmod(x_ref[rows, :], _sub_mod(shc_ref, shb_ref, sb, per_batch),
                     _sub_mod(scc_ref, scb_ref, sb, per_batch)).astype(BF16)
        for part, gain_ref, scale, dst in ((0, qn_ref, NA_HD ** -0.5 * LOG2E, q_ref), (1, kn_ref, 1.0, k_ref)):
            gain = gain_ref[...]
            for c in range(2):
                base = part * D_MODEL + c * half
                acc = _dot(h, w_ref[:, base:base + half])
                for g in range(half // LANES):
                    a = acc[:, g * LANES:(g + 1) * LANES]
                    sq = a * a
                    s_lo = jnp.sum(jnp.where(lo, sq, 0.0), axis=-1, keepdims=True)
                    s_hi = jnp.sum(jnp.where(lo, 0.0, sq), axis=-1, keepdims=True)
                    inv = lax.rsqrt(jnp.where(lo, s_lo, s_hi) * (1.0 / NA_HD) + EPS)
                    y = a * inv * gain
                    if scale != 1.0:
                        y = y * scale
                    dst[rows, c * half + g * LANES:c * half + (g + 1) * LANES] = y.astype(BF16)
        for c in range(2):
            base = 2 * D_MODEL + c * half
            v_ref[rows, c * half:(c + 1) * half] = _dot(h, w_ref[:, base:base + half]).astype(BF16)


def _odd_in_proj(x2, mod5, layer, tiles, w_qkv, qn2, kn2):
    m = x2.shape[0]
    out = jax.ShapeDtypeStruct((m, D_MODEL), BF16)
    return pl.pallas_call(
        functools.partial(_odd_in_kernel, n_sub=tiles.n_sub, per_batch=tiles.per_batch),
        grid=(tiles.n,),
        in_specs=[tiles.row_spec(D_MODEL)] + tiles.mod_specs(layer, 0) + tiles.mod_specs(layer, 1) + [
            _resident((D_MODEL, 3 * D_MODEL), lambda i: (0, 0)),
            pl.BlockSpec((1, LANES), lambda i: (0, 0)), pl.BlockSpec((1, LANES), lambda i: (0, 0)),
        ],
        out_specs=[tiles.row_spec(D_MODEL)] * 3,
        out_shape=(out, out, out),
        compiler_params=_cparams("parallel"),
        name="odd_qkv_proj",
    )(x2, mod5, mod5, mod5, mod5, w_qkv, qn2, kn2)


def _pair_queries(q):
    lane = lax.broadcasted_iota(jnp.int32, q.shape, 1)
    zero = jnp.zeros_like(q)
    return jnp.concatenate([jnp.where(lane < NA_HD, q, zero), jnp.where(lane < NA_HD, zero, q)], axis=0)


def _unpair(res, n):
    lane = lax.broadcasted_iota(jnp.int32, (n, LANES), 1)
    return jnp.where(lane < NA_HD, res[0:n, :], res[n:2 * n, :])


def _natten_kernel(q_ref, k_ref, v_ref, t_ref, o_ref, va_ref, *, ctx, rows_n, win_rows):
    tok = v_ref.shape[0]
    va_ref[:, 0:LANES] = v_ref[...]
    va_ref[:, LANES:2 * LANES] = jnp.ones((tok, LANES), BF16)
    kc = k_ref[0:ctx, :]
    vc = va_ref[0:ctx, :]

    def softmax_pv(scores, values):
        m = functools.reduce(jnp.maximum, [jnp.max(s, axis=1, keepdims=True) for s in scores])
        res = None
        for s, vblk in zip(scores, values):
            part = _dot(jnp.exp2(s - m).astype(BF16), vblk)
            res = part if res is None else res + part
        return res[:, 0:LANES] * (1.0 / res[:, LANES:2 * LANES])

    w = _pair_queries(q_ref[0:ctx, :])
    o_ref[0:ctx, :] = _unpair(softmax_pv([_dot_nt(w, kc)], [vc]), ctx).astype(BF16)

    nk = win_rows * GRID_W

    def body(r, carry):
        rs = jnp.clip(r - win_rows // 2, 0, rows_n - win_rows)
        dr0 = rs - r + NA_WIN_ROWS - 1
        q_off = pl.multiple_of(ctx + r * GRID_W, GRID_W)
        k_off = pl.multiple_of(ctx + rs * GRID_W, GRID_W)
        w = _pair_queries(q_ref[pl.ds(q_off, GRID_W), :])
        bias = t_ref[dr0 & 1, :, pl.ds(pl.multiple_of((dr0 >> 1) * LANES, LANES), nk)]
        s_lat = _dot_nt(w, k_ref[pl.ds(k_off, nk), :]) + bias
        s_ctx = _dot_nt(w, kc)
        out = softmax_pv([s_lat, s_ctx], [va_ref[pl.ds(k_off, nk), :], vc])
        o_ref[pl.ds(q_off, GRID_W), :] = _unpair(out, GRID_W).astype(BF16)
        return carry

    lax.fori_loop(0, rows_n, body, 0, unroll=min(NA_UNROLL, rows_n))


def _natten(q, k, v, table, ctx, seq):
    b, tok, _ = q.shape
    rows_n = seq // GRID_W
    assert rows_n >= NA_WIN_ROWS and rows_n % min(NA_UNROLL, rows_n) == 0
    pairs = NA_HEADS // 2
    pair_spec = pl.BlockSpec((None, tok, LANES), lambda p, i: (i, 0, p))
    return pl.pallas_call(
        functools.partial(_natten_kernel, ctx=ctx, rows_n=rows_n, win_rows=NA_WIN_ROWS),
        grid=(pairs, b),
        in_specs=[
            pair_spec, pair_spec, pair_spec,
            pl.BlockSpec((None,) + table.shape[1:], lambda p, i: (p, 0, 0, 0)),
        ],
        out_specs=pair_spec,
        out_shape=jax.ShapeDtypeStruct((b, tok, D_MODEL), BF16),
        scratch_shapes=[pltpu.VMEM((tok, 2 * LANES), BF16)],
        compiler_params=_cparams("parallel", "parallel"),
        name="natten",
    )(q, k, v, table)


def _natten_bias_table(rpb):
    n_dr = 2 * NA_WIN_ROWS - 1
    n_dc = 2 * NA_WIN_COLS - 1
    cidx = np.arange(GRID_W)
    cstart = np.clip(cidx - NA_WIN_COLS // 2, 0, GRID_W - NA_WIN_COLS)
    cmask = (cidx[None, :] >= cstart[:, None]) & (cidx[None, :] < cstart[:, None] + NA_WIN_COLS)
    dc = np.clip(cidx[None, :] - cidx[:, None] + NA_WIN_COLS - 1, 0, n_dc - 1)
    onehot = ((dc[None] == np.arange(n_dc)[:, None, None]) & cmask[None]).astype(np.float32)
    tb = jnp.einsum("hrc,cqk->hrqk", rpb.astype(F32), jnp.asarray(onehot), precision=lax.Precision.HIGHEST)
    tb = tb * LOG2E + jnp.asarray(np.where(cmask, 0.0, NEG_BIG), F32)
    tb = tb.reshape(NA_HEADS // 2, 2, n_dr, GRID_W, GRID_W).transpose(0, 1, 3, 2, 4)
    tb = tb.reshape(NA_HEADS // 2, 2 * GRID_W, n_dr * GRID_W)
    width = (n_dr + 1) * GRID_W
    tb = jnp.pad(tb, ((0, 0), (0, 0), (0, width + GRID_W - n_dr * GRID_W)))
    return jnp.stack([tb[:, :, 0:width], tb[:, :, GRID_W:GRID_W + width]], axis=1)


FF_CHUNK = 1024


def _mix_mlp_kernel(*refs, widths, n_sub, per_batch, latent_only):
    x_ref = refs[0]
    g1, sh, sc, g2 = (refs[1 + 2 * j:3 + 2 * j] for j in range(4))
    in_refs = refs[9:9 + len(widths)]
    wo_ref, w1_ref, w2_ref, o_ref = refs[9 + len(widths):]

    def compute(sb):
        rows = _sub_rows(sb)
        mix = None
        off = 0
        for r, wd in zip(in_refs, widths):
            part = _dot(r[rows, :].astype(BF16), wo_ref[off:off + wd, :])
            mix = part if mix is None else mix + part
            off += wd
        x = x_ref[rows, :] + _sub_mod(*g1, sb, per_batch) * mix
        h = _normmod(x, _sub_mod(*sh, sb, per_batch), _sub_mod(*sc, sb, per_batch)).astype(BF16)
        acc = None
        for c0 in range(0, D_FF, FF_CHUNK):
            a = jnp.maximum(_dot(h, w1_ref[:, c0:c0 + FF_CHUNK]), 0.0)
            part = _dot((a * a).astype(BF16), w2_ref[c0:c0 + FF_CHUNK, :])
            acc = part if acc is None else acc + part
        o_ref[rows, :] = x + _sub_mod(*g2, sb, per_batch) * acc

    if latent_only:
        assert n_sub == 1
        pl.when(pl.program_id(0) % per_batch != 0)(functools.partial(compute, 0))
    else:
        for sb in range(n_sub):
            compute(sb)


def _mix_mlp(x2, mod5, layer, tiles, parts, w_out, w1, w2, latent_only):
    widths = tuple(p.shape[1] for p in parts)
    pb = tiles.per_batch
    if latent_only:
        out_rows = tiles.batch * (tiles.tok - tiles.ctx)
        out_spec = pl.BlockSpec((tiles.rows, D_MODEL), lambda i: (i - i // pb - jnp.where(i % pb == 0, 0, 1), 0))
    else:
        out_rows = x2.shape[0]
        out_spec = tiles.row_spec(D_MODEL)
    mods = [s for which in (2, 3, 4, 5) for s in tiles.mod_specs(layer, which)]
    return pl.pallas_call(
        functools.partial(_mix_mlp_kernel, widths=widths, n_sub=tiles.n_sub, per_batch=pb, latent_only=latent_only),
        grid=(tiles.n,),
        in_specs=[tiles.row_spec(D_MODEL)] + mods
        + [tiles.row_spec(wd) for wd in widths]
        + [_resident(w_out.shape, lambda i: (0, 0)), _resident(w1.shape, lambda i: (0, 0)),
           _resident(w2.shape, lambda i: (0, 0))],
        out_specs=out_spec,
        out_shape=jax.ShapeDtypeStruct((out_rows, D_MODEL), F32),
        input_output_aliases={} if latent_only else {0: 0},
        compiler_params=_cparams("arbitrary" if latent_only else "parallel"),
        name="mix_out_mlp",
    )(x2, *([mod5] * len(mods)), *parts, w_out, w1, w2)


def _pad_cols(a, width):
    return jnp.pad(a, ((0, 0), (0, width - a.shape[1])))


def kernel(x, c, ctx, c_ctx, w_mod, b_mod, w_mlp_in, w_mlp_out, e_w_in, e_gate_b, e_hnorm, e_conv_w, e_conv_b,
           e_f_w1, e_f_b1, e_f_w2, e_f_b2, e_f_w3, e_f_freq, e_hy_d, e_w_out, o_w_qkv, o_qn, o_kn, o_rpb, o_w_out):
    batch, seq, d = x.shape
    n_ctx = ctx.shape[1]
    depth = w_mod.shape[0]
    tok = n_ctx + seq
    assert d == D_MODEL and seq % CHUNK == 0 and n_ctx == CHUNK and seq % GRID_W == 0
    tiles = _Tiles(batch, tok, n_ctx, TILE_ROWS if tok % TILE_ROWS == 0 else SUB_ROWS)
    tiles_last = _Tiles(batch, tok, n_ctx, SUB_ROWS)

    n_samp = -(-(batch + 1) // 8) * 8
    cvec = jnp.concatenate([c, c_ctx[None, :], jnp.zeros((n_samp - batch - 1, d), F32)], axis=0)
    mod5 = _modulation(cvec, w_mod, b_mod)

    cos_t, sin_t = _rope_tables(n_ctx, seq)
    a_lat, _ = _dft_matrices(seq)
    a_ctx, b_ctx = _dft_matrices(n_ctx)
    dft_mats = _dft_matrices(seq // 2) + (a_ctx, b_ctx)
    twiddle = _twiddles(seq)

    xs = jnp.concatenate([ctx, x], axis=1).reshape(batch * tok, d)

    for l in range(depth):
        i = l // 2
        if l % 2 == 0:
            w_in = e_w_in[i]
            g0 = 4 * MIX_A
            w_pad = jnp.concatenate(
                [w_in[:, :g0], _pad_cols(w_in[:, g0:g0 + N_GATES], LANES), w_in[:, g0 + N_GATES:]], axis=1).astype(BF16)
            gate_b = _pad_cols(e_gate_b[i][None, :], LANES)
            q, k, v, sg, g_t, hy = _even_in_proj(xs, mod5, l, tiles, w_pad, gate_b, cos_t, sin_t)
            as3 = lambda a: a.reshape(batch, tok, a.shape[-1])
            a_out = _mlstm(as3(q), as3(k), as3(v), as3(sg), g_t, e_hnorm[i][None, :])

            hy3 = as3(hy)
            conv = (e_conv_w[i], e_conv_b[i][None, :])
            pad2 = lambda a: jnp.pad(a, ((0, LANES - a.shape[0]), (0, LANES - a.shape[1])))
            w1p = pad2(e_f_w1[i]).astype(BF16)
            w2p = pad2(e_f_w2[i]).astype(BF16)
            w3p = jnp.pad(e_f_w3[i], ((0, LANES - HYENA_FFN), (0, 0))).astype(BF16)
            b1p = _pad_cols(e_f_b1[i][None, :], LANES)
            b2p = _pad_cols(e_f_b2[i][None, :], LANES)
            frp = _pad_cols(e_f_freq[i][None, :], LANES)
            g_lat = _filter_spectrum_bfly(a_lat, _hyena_filters(seq, w1p, b1p, w2p, b2p, frp, w3p))
            g_ctx = _filter_spectrum(a_ctx, _hyena_filters(n_ctx, w1p, b1p, w2p, b2p, frp, w3p))
            dskip = e_hy_d[i][:, None, :]
            n_ord = e_hy_d.shape[1]
            z, conv_z = hy3, conv
            for o in range(n_ord):
                z = _hyena_order(z, conv_z, hy3, 1 + o, conv, dskip, o, dft_mats, g_lat, g_ctx, twiddle,
                                 n_ctx, seq, F32)
                conv_z = None
            parts = [a_out.reshape(batch * tok, MIX_A), z.reshape(batch * tok, MIX_B)]
            w_out = e_w_out[i].astype(BF16)
        else:
            rep = LANES // NA_HD
            qn2 = jnp.tile(o_qn[i], rep)[None, :]
            kn2 = jnp.tile(o_kn[i], rep)[None, :]
            q, k, v = _odd_in_proj(xs, mod5, l, tiles, o_w_qkv[i].astype(BF16), qn2, kn2)
            as3 = lambda a: a.reshape(batch, tok, D_MODEL)
            table = _natten_bias_table(o_rpb[i])
            att = _natten(as3(q), as3(k), as3(v), table, n_ctx, seq)
            parts = [att.reshape(batch * tok, D_MODEL)]
            w_out = o_w_out[i].astype(BF16)
        last = l == depth - 1
        xs = _mix_mlp(xs, mod5, l, tiles_last if last else tiles, parts, w_out,
                      w_mlp_in[l].astype(BF16), w_mlp_out[l].astype(BF16), latent_only=last)

    return xs.reshape(batch, seq, d)
```

```python
import functools
import math

import numpy as np
import jax
import jax.numpy as jnp
from jax import lax
from jax.experimental import pallas as pl
from jax.experimental.pallas import tpu as pltpu

F32 = jnp.float32
BF16 = jnp.bfloat16

D_MODEL = 1024
D_FF = 4 * D_MODEL
EPS = 1e-6
ROPE_BASE = 10000.0
GRID_W = 64
MIX_A = D_MODEL // 2
MIX_B = D_MODEL - MIX_A
MLSTM_HEADS = 4
MLSTM_HD = MIX_A // MLSTM_HEADS
N_GATES = 4 * MLSTM_HEADS
HYENA_EMB = 33
HYENA_FFN = 64
HYENA_TARGET = 1e-2
HYENA_FAST = 0.3
HYENA_SLOW = 1.5
NA_HEADS = 16
NA_HD = D_MODEL // NA_HEADS
NA_WIN_ROWS = 8
NA_WIN_COLS = 16

LANES = 128
SUB_ROWS = 256
TILE_ROWS = 768
CHUNK = 256
VMEM_LIMIT = 56 * 1024 * 1024
NEG_BIG = -1e30
LOG2E = math.log2(math.e)
DFT_SPLIT = 64
NA_UNROLL = 32


def _cparams(*sem):
    return pltpu.CompilerParams(dimension_semantics=sem, vmem_limit_bytes=VMEM_LIMIT)


def _dot(a, b):
    return jnp.dot(a, b, preferred_element_type=F32)


def _dot_nt(a, b):
    return lax.dot_general(a, b, (((1,), (1,)), ((), ())), preferred_element_type=F32)


def _dot_tn(a, b):
    return lax.dot_general(a, b, (((0,), (0,)), ((), ())), preferred_element_type=F32)


def _resident(shape, index_map):
    return pl.BlockSpec(shape, index_map, pipeline_mode=pl.Buffered(1))


def _normmod(x, sh, sc):
    ms = jnp.mean(x * x, axis=-1, keepdims=True)
    return (x * lax.rsqrt(ms + EPS)) * (1.0 + sc) + sh


def _sigmoid(x):
    return 1.0 / (1.0 + jnp.exp(-x))


def _log_sigmoid(x):
    return jnp.minimum(x, 0.0) - jnp.log(1.0 + jnp.exp(-jnp.abs(x)))


def _mod_kernel(c_ref, w_ref, b_ref, o_ref):
    c = c_ref[...]
    s = (c * _sigmoid(c)).astype(BF16)
    o_ref[...] = _dot(s, w_ref[...].astype(BF16)) + b_ref[...]


def _modulation(cvec, w_mod, b_mod):
    depth, d, d6 = w_mod.shape
    ns = cvec.shape[0]
    nj = d6 // d
    out = pl.pallas_call(
        _mod_kernel,
        grid=(depth, nj),
        in_specs=[
            pl.BlockSpec((ns, d), lambda l, j: (0, 0)),
            pl.BlockSpec((None, d, d), lambda l, j: (l, 0, j)),
            pl.BlockSpec((None, 1, d), lambda l, j: (l, 0, j)),
        ],
        out_specs=pl.BlockSpec((None, ns, d), lambda l, j: (l, 0, j)),
        out_shape=jax.ShapeDtypeStruct((depth, ns, d6), F32),
        compiler_params=_cparams("arbitrary", "arbitrary"),
        name="modulation",
    )(cvec, w_mod, b_mod.reshape(depth, 1, d6))
    return out.reshape(depth, ns, nj, 1, d)


class _Tiles:
    def __init__(self, batch, tok, ctx, rows):
        assert ctx == SUB_ROWS and rows % SUB_ROWS == 0 and tok % rows == 0
        self.batch, self.tok, self.ctx, self.rows = batch, tok, ctx, rows
        self.n_sub = rows // SUB_ROWS
        self.per_batch = tok // rows
        self.n = batch * self.per_batch

    def mod_specs(self, layer, which):
        blk = (None, None, None, 1, D_MODEL)
        return [pl.BlockSpec(blk, lambda i: (layer, self.batch, which, 0, 0)),
                pl.BlockSpec(blk, lambda i: (layer, i // self.per_batch, which, 0, 0))]

    def row_spec(self, width):
        return pl.BlockSpec((self.rows, width), lambda i: (i, 0))

    def pos_spec(self, width):
        return pl.BlockSpec((self.rows, width), lambda i: (i % self.per_batch, 0))


def _sub_mod(c_ref, b_ref, sb, per_batch):
    if sb > 0:
        return b_ref[...]
    return jnp.where(pl.program_id(0) % per_batch == 0, c_ref[...], b_ref[...])


def _sub_rows(sb):
    return slice(sb * SUB_ROWS, (sb + 1) * SUB_ROWS)


def _even_in_kernel(x_ref, shc_ref, shb_ref, scc_ref, scb_ref, w_ref, gb_ref, cos_ref, sin_ref,
                    q_ref, kt_ref, v_ref, sg_ref, gt_ref, hy_ref, *, n_sub, per_batch):
    lane = lax.broadcasted_iota(jnp.int32, (SUB_ROWS, MLSTM_HD), 1)
    first = (lane % (MLSTM_HD // 2)) < (MLSTM_HD // 4)
    g0 = 4 * MIX_A
    h0 = g0 + LANES
    for sb in range(n_sub):
        rows = _sub_rows(sb)
        h = _normmod(x_ref[rows, :], _sub_mod(shc_ref, shb_ref, sb, per_batch),
                     _sub_mod(scc_ref, scb_ref, sb, per_batch)).astype(BF16)
        cos = cos_ref[rows, :]
        sin = sin_ref[rows, :]

        def rope(a):
            part = jnp.where(first, pltpu.roll(a, LANES - MLSTM_HD // 4, 1), pltpu.roll(a, MLSTM_HD // 4, 1))
            return a * cos + part * sin

        acc = _dot(h, w_ref[:, 0:MIX_A])
        for hd in range(MLSTM_HEADS):
            sl = slice(hd * MLSTM_HD, (hd + 1) * MLSTM_HD)
            q_ref[rows, sl] = rope(acc[:, sl]).astype(BF16)
        acc = _dot(h, w_ref[:, MIX_A:2 * MIX_A]) * (MLSTM_HD ** -0.5)
        for hd in range(MLSTM_HEADS):
            sl = slice(hd * MLSTM_HD, (hd + 1) * MLSTM_HD)
            kt_ref[sl, rows] = rope(acc[:, sl]).T.astype(BF16)
        v_ref[rows, :] = _dot(h, w_ref[:, 2 * MIX_A:3 * MIX_A]).astype(BF16)
        sg_ref[rows, :] = _sigmoid(_dot(h, w_ref[:, 3 * MIX_A:4 * MIX_A])).astype(BF16)
        gt_ref[:, rows] = (_dot(h, w_ref[:, g0:g0 + LANES]) + gb_ref[...]).T
        for j in range(3):
            hy_ref[rows, j * MIX_B:(j + 1) * MIX_B] = _dot(h, w_ref[:, h0 + j * MIX_B:h0 + (j + 1) * MIX_B])


def _even_in_proj(x2, mod5, layer, tiles, w_pad, gate_b_pad, cos_t, sin_t):
    m = x2.shape[0]
    n_w = w_pad.shape[1]
    outs = (
        jax.ShapeDtypeStruct((m, MIX_A), BF16), jax.ShapeDtypeStruct((MIX_A, m), BF16),
        jax.ShapeDtypeStruct((m, MIX_A), BF16), jax.ShapeDtypeStruct((m, MIX_A), BF16),
        jax.ShapeDtypeStruct((LANES, m), F32), jax.ShapeDtypeStruct((m, 3 * MIX_B), F32),
    )
    return pl.pallas_call(
        functools.partial(_even_in_kernel, n_sub=tiles.n_sub, per_batch=tiles.per_batch),
        grid=(tiles.n,),
        in_specs=[tiles.row_spec(D_MODEL)] + tiles.mod_specs(layer, 0) + tiles.mod_specs(layer, 1) + [
            _resident((D_MODEL, n_w), lambda i: (0, 0)),
            pl.BlockSpec((1, LANES), lambda i: (0, 0)),
            tiles.pos_spec(MLSTM_HD), tiles.pos_spec(MLSTM_HD),
        ],
        out_specs=[tiles.row_spec(MIX_A), pl.BlockSpec((MIX_A, tiles.rows), lambda i: (0, i)),
                   tiles.row_spec(MIX_A), tiles.row_spec(MIX_A),
                   pl.BlockSpec((LANES, tiles.rows), lambda i: (0, i)), tiles.row_spec(3 * MIX_B)],
        out_shape=outs,
        compiler_params=_cparams("parallel"),
        name="even_in_proj",
    )(x2, mod5, mod5, mod5, mod5, w_pad, gate_b_pad, cos_t, sin_t)


def _rope_tables(ctx, seq):
    half = MLSTM_HD // 2
    nf = half // 2
    inv = ROPE_BASE ** (-np.arange(nf, dtype=np.float64) / nf)
    t = np.arange(seq)
    rows, cols = t // GRID_W, t % GRID_W

    def one(pos):
        ang = pos[:, None].astype(np.float64) * inv[None, :]
        c = np.concatenate([np.cos(ang), np.cos(ang)], axis=-1)
        s = np.concatenate([-np.sin(ang), np.sin(ang)], axis=-1)
        return c, s

    cr, sr = one(rows)
    cc, sc = one(cols)
    cos = np.concatenate([cr, cc], axis=-1)
    sin = np.concatenate([sr, sc], axis=-1)
    cos = np.concatenate([np.ones((ctx, MLSTM_HD)), cos], axis=0)
    sin = np.concatenate([np.zeros((ctx, MLSTM_HD)), sin], axis=0)
    return jnp.asarray(cos, F32), jnp.asarray(sin, F32)


def _seg_scan(y, pos, op, reverse, axis):
    n = y.shape[axis]
    k = 1
    while k < CHUNK:
        if reverse:
            y = jnp.where(pos < CHUNK - k, op(y, pltpu.roll(y, n - k, axis)), y)
        else:
            y = jnp.where(pos >= k, op(y, pltpu.roll(y, k, axis)), y)
        k *= 2
    return y


def _mlstm_gate_scans(gt_ref, qc_refs, ar_ref):
    hh = MLSTM_HEADS
    g = gt_ref[...]
    gi = jnp.concatenate([g[0:hh], g[2 * hh:3 * hh]], axis=0) * LOG2E
    lf = _log_sigmoid(jnp.concatenate([g[hh:2 * hh], g[3 * hh:4 * hh]], axis=0)) * LOG2E
    fwd = lax.broadcasted_iota(jnp.int32, lf.shape, 0) < hh
    pos = lax.broadcasted_iota(jnp.int32, lf.shape, 1) % CHUNK
    b = jnp.where(fwd, _seg_scan(lf, pos, jnp.add, False, 1), _seg_scan(lf, pos, jnp.add, True, 1))
    a = gi - b
    amax = jnp.where(fwd, _seg_scan(a, pos, jnp.maximum, False, 1), _seg_scan(a, pos, jnp.maximum, True, 1))
    ar_ref[...] = a
    stack = jnp.concatenate([b, amax], axis=0)
    hi = stack.astype(BF16)
    rest = stack - hi.astype(F32)
    mid = rest.astype(BF16)
    lo = (rest - mid.astype(F32)).astype(BF16)
    n = stack.shape[0]
    nq = b.shape[0]
    eye = (lax.broadcasted_iota(jnp.int32, (n, LANES), 0) == lax.broadcasted_iota(jnp.int32, (n, LANES), 1))
    eye = eye.astype(F32).astype(BF16)
    cols = (_dot_tn(hi, eye) + _dot_tn(mid, eye)) + _dot_tn(lo, eye)
    for j, ref in enumerate(qc_refs):
        ref[...] = cols if j == 0 else pltpu.roll(cols, LANES - j * nq, 1)


def _mlstm_chunk(qc, ktc, vc, b_col, amax_col, a_row, state, m, lower):
    t, dv = vc.shape
    r = lax.broadcasted_iota(jnp.int32, (t, t), 0)
    c = lax.broadcasted_iota(jnp.int32, (t, t), 1)
    incl = (c <= r) if lower else (c >= r)
    mm = jnp.broadcast_to(jnp.maximum(m, amax_col), (t, dv))
    b_rows = jnp.broadcast_to(b_col, (t, dv))
    sc = jnp.exp2(m - mm)
    floor = jnp.exp2(-(b_rows + mm))
    rep = t // dv
    w = jnp.exp2(jnp.where(incl, a_row - jnp.concatenate([mm] * rep, axis=1), -jnp.inf))
    qkw = _dot(qc, ktc) * w
    v_aug = jnp.concatenate([vc, jnp.ones_like(vc)], axis=1)
    res = jnp.concatenate([sc, sc], axis=1) * _dot(qc, state.astype(BF16)) + _dot(qkw.astype(BF16), v_aug)
    h = res[:, :dv] / jnp.maximum(jnp.abs(res[:, dv:]), floor)
    e = t - 1 if lower else 0
    bl = b_col[e:e + 1, :]
    m_new = bl + jnp.maximum(m, amax_col[e:e + 1, :])
    decay = jnp.exp2(bl + m - m_new)
    kw_t = (ktc.astype(F32) * jnp.exp2(bl + a_row - m_new)).astype(BF16)
    return h, decay * state + _dot(kw_t, v_aug), m_new


def _mlstm_kernel(q_ref, kt_ref, v_ref, sg_ref, gt_ref, hn_ref, o_ref, hf_ref, hb_ref, bq_ref, mq_ref,
                  ar_ref, st_ref, *, n_chunks):
    _mlstm_gate_scans(gt_ref, (bq_ref, mq_ref), ar_ref)
    st_ref[...] = jnp.zeros(st_ref.shape, F32)

    def run(off, hd, m, lower, dst_ref):
        r = hd if lower else MLSTM_HEADS + hd
        slot = 2 * hd + (0 if lower else 1)
        rows = pl.ds(off, CHUNK)
        cols = slice(hd * MLSTM_HD, (hd + 1) * MLSTM_HD)
        h, state, m = _mlstm_chunk(
            q_ref[rows, cols], kt_ref[cols, rows], v_ref[rows, cols],
            bq_ref[rows, r:r + 1], mq_ref[rows, r:r + 1], ar_ref[r:r + 1, rows], st_ref[slot], m, lower)
        st_ref[slot] = state
        dst_ref[rows, cols] = h
        return m

    def body(s, ms):
        off_f = pl.multiple_of(s * CHUNK, CHUNK)
        off_b = pl.multiple_of(jnp.where(s == 0, 0, n_chunks - s) * CHUNK, CHUNK)
        out = []
        for hd in range(MLSTM_HEADS):
            out.append(run(off_f, hd, ms[2 * hd], True, hf_ref))
            out.append(run(off_b, hd, ms[2 * hd + 1], False, hb_ref))
        return tuple(out)

    lax.fori_loop(0, n_chunks, body, tuple(jnp.zeros((1, 1), F32) for _ in range(2 * MLSTM_HEADS)))

    for hd in range(MLSTM_HEADS):
        cols = slice(hd * MLSTM_HD, (hd + 1) * MLSTM_HD)
        hm = hf_ref[:, cols] + hb_ref[:, cols]
        ms = jnp.mean(hm * hm, axis=-1, keepdims=True)
        y = hm * lax.rsqrt(ms + EPS) * hn_ref[:, cols]
        o_ref[:, cols] = (y * sg_ref[:, cols].astype(F32)).astype(BF16)


def _mlstm(q, k_t, v, sg, gates_t, hnorm):
    b, tok, _ = q.shape
    full = pl.BlockSpec((None, tok, MIX_A), lambda i: (i, 0, 0))
    return pl.pallas_call(
        functools.partial(_mlstm_kernel, n_chunks=tok // CHUNK),
        grid=(b,),
        in_specs=[
            full, pl.BlockSpec((MIX_A, tok), lambda i: (0, i)), full, full,
            pl.BlockSpec((N_GATES, tok), lambda i: (0, i)),
            pl.BlockSpec((1, MIX_A), lambda i: (0, 0)),
        ],
        out_specs=full,
        out_shape=jax.ShapeDtypeStruct((b, tok, MIX_A), BF16),
        scratch_shapes=[
            pltpu.VMEM((tok, MIX_A), F32), pltpu.VMEM((tok, MIX_A), F32),
            pltpu.VMEM((tok, LANES), F32), pltpu.VMEM((tok, LANES), F32),
            pltpu.VMEM((2 * MLSTM_HEADS, tok), F32),
            pltpu.VMEM((2 * MLSTM_HEADS, MLSTM_HD, 2 * MLSTM_HD), F32),
        ],
        compiler_params=_cparams("parallel"),
        name="mlstm",
    )(q, k_t, v, sg, gates_t, hnorm)


def _short_conv_rows(x, w, b, ctx):
    tok = x.shape[0]
    t = lax.broadcasted_iota(jnp.int32, x.shape, 0)
    prev = jnp.where((t == 0) | (t == ctx), 0.0, pltpu.roll(x, 1, 0))
    nxt = jnp.where((t == ctx - 1) | (t == tok - 1), 0.0, pltpu.roll(x, tok - 1, 0))
    return b + prev * w[0:1, :] + x * w[1:2, :] + nxt * w[2:3, :]


def _filter_kernel(emb_ref, w1_ref, b1_ref, w2_ref, b2_ref, fr_ref, w3_ref, dec_ref, o_ref, hid_ref):
    @pl.when(pl.program_id(0) == 0)
    def _():
        fr = fr_ref[...]
        h = jnp.sin(fr * (_dot(emb_ref[...].astype(BF16), w1_ref[...]) + b1_ref[...]))
        hid_ref[...] = jnp.sin(fr * (_dot(h.astype(BF16), w2_ref[...]) + b2_ref[...])).astype(BF16)

    h = _dot(hid_ref[...], w3_ref[...]) * dec_ref[...]
    o_ref[...] = h * lax.rsqrt(jnp.sum(h * h, axis=0, keepdims=True) + EPS)


def _hyena_filters(length, w1p, b1p, w2p, b2p, frp, w3p):
    bands = (HYENA_EMB - 1) // 2
    t = np.linspace(0.0, 1.0, length)[:, None]
    wpos = 2.0 * math.pi * np.arange(length) / length
    fr = np.linspace(1e-4, bands - 1, bands)
    ang = wpos[:, None] * fr[None, :]
    emb = np.concatenate([t, np.cos(ang), -np.sin(ang)], axis=-1)
    emb = np.pad(emb, ((0, 0), (0, LANES - HYENA_EMB)))
    deltas = np.abs(np.linspace(math.log(HYENA_TARGET) / HYENA_SLOW, math.log(HYENA_TARGET) / HYENA_FAST, MIX_B))
    decay = np.exp(-t * deltas[None, :])
    ngrp = w3p.shape[1] // MIX_B
    const = lambda g: (0, 0)
    return pl.pallas_call(
        _filter_kernel,
        grid=(ngrp,),
        in_specs=[
            pl.BlockSpec((length, LANES), const), pl.BlockSpec((LANES, LANES), const),
            pl.BlockSpec((1, LANES), const), pl.BlockSpec((LANES, LANES), const),
            pl.BlockSpec((1, LANES), const), pl.BlockSpec((1, LANES), const),
            pl.BlockSpec((LANES, MIX_B), lambda g: (0, g)),
            pl.BlockSpec((length, MIX_B), const),
        ],
        out_specs=pl.BlockSpec((length, MIX_B), lambda g: (0, g)),
        out_shape=jax.ShapeDtypeStruct((length, ngrp * MIX_B), F32),
        scratch_shapes=[pltpu.VMEM((length, LANES), BF16)],
        compiler_params=_cparams("arbitrary"),
        name="hyena_filters",
    )(jnp.asarray(emb, F32), w1p, b1p, w2p, b2p, frp, w3p, jnp.asarray(decay, F32))


def _dft_matrices(length):
    period = 2 * length
    k = jnp.arange(length, dtype=jnp.int32)[:, None]
    t = jnp.arange(length, dtype=jnp.int32)[None, :]
    unit = 2.0 * math.pi / period

    def table(freqs):
        ang = ((freqs[:, None] * t) & (period - 1)).astype(F32) * unit
        return jnp.cos(ang), jnp.sin(ang)

    c1, s1 = table(jnp.arange(length // DFT_SPLIT, dtype=jnp.int32) * DFT_SPLIT)
    c0, s0 = table(jnp.arange(DFT_SPLIT, dtype=jnp.int32))
    c = (c1[:, None, :] * c0[None, :, :] - s1[:, None, :] * s0[None, :, :]).reshape(length, length)
    s = (s1[:, None, :] * c0[None, :, :] + c1[:, None, :] * s0[None, :, :]).reshape(length, length)
    alt_t = jnp.where(t % 2 == 0, 1.0, -1.0).astype(F32)
    fwd = jnp.concatenate([c, jnp.where(k == 0, alt_t, -s)], axis=0)
    coef = jnp.where(t == 0, 1.0, 2.0).astype(F32) / period
    alt_k = jnp.where(k % 2 == 0, 1.0, -1.0).astype(F32)
    inv = jnp.concatenate([c * coef, jnp.where(t == 0, alt_k / period, -s * coef)], axis=1)
    return fwd.astype(BF16), inv.astype(BF16)


def _spectrum_kernel(a_ref, hf_ref, hb_ref, o_ref):
    half = pl.program_id(0)
    hf = hf_ref[...]
    hb = hb_ref[...]
    comb = jnp.where(half == 0, hf + hb, hf - hb).astype(BF16)
    o_ref[...] = _dot(a_ref[...], comb)

    @pl.when(half == 1)
    def _():
        t = lax.broadcasted_iota(jnp.int32, hf.shape, 0)
        o_ref[0:1, :] = jnp.sum(jnp.where(t % 2 == 0, hf + hb, -(hf + hb)), axis=0, keepdims=True)


def _filter_spectrum(fwd, filt):
    period, length = fwd.shape
    orders = filt.shape[1] // (2 * MIX_B)
    return pl.pallas_call(
        _spectrum_kernel,
        grid=(2, orders),
        in_specs=[
            pl.BlockSpec((length, length), lambda hlf, o: (hlf, 0)),
            pl.BlockSpec((length, MIX_B), lambda hlf, o: (0, 2 * o)),
            pl.BlockSpec((length, MIX_B), lambda hlf, o: (0, 2 * o + 1)),
        ],
        out_specs=pl.BlockSpec((length, MIX_B), lambda hlf, o: (hlf, o)),
        out_shape=jax.ShapeDtypeStruct((period, orders * MIX_B), F32),
        compiler_params=_cparams("arbitrary", "arbitrary"),
        name="hyena_filter_spectrum",
    )(fwd, filt, filt)


HY_CH = 256
BFLY_ROWS = 128


def _spectral_product(a_ref, g_ref, z, length):
    zr = _dot(a_ref[0:length, :], z)
    zi = _dot(a_ref[length:2 * length, :], z)
    gr = g_ref[0:length, :]
    gi = g_ref[length:2 * length, :]
    first = lax.broadcasted_iota(jnp.int32, zr.shape, 0) == 0
    yr = jnp.where(first, zr * gr, zr * gr - zi * gi)
    yi = jnp.where(first, zi * gi, zr * gi + zi * gr)
    return jnp.concatenate([yr, yi], axis=0).astype(BF16)


def _radix2_conv(ze, zo, a_ref, b_ref, g_ref, tw_ref):
    n = ze.shape[0]
    e = _dot(a_ref[...], ze)
    o = _dot(a_ref[...], zo)
    rep = ze.shape[1] // LANES
    f0r, f0i, f1r, f1i = [], [], [], []
    for r0 in range(0, n, BFLY_ROWS):
        rows = slice(r0, r0 + BFLY_ROWS)
        rows_im = slice(n + r0, n + r0 + BFLY_ROWS)
        er, ei, orr, oi = e[rows, :], e[rows_im, :], o[rows, :], o[rows_im, :]
        c = jnp.concatenate([tw_ref[rows, :]] * rep, axis=1)
        s = jnp.concatenate([tw_ref[rows_im, :]] * rep, axis=1)
        gar, gai, gbr, gbi = (g_ref[j * n + r0:j * n + r0 + BFLY_ROWS, :] for j in range(4))
        tr = c * orr + s * oi
        ti = c * oi - s * orr
        zar = er + tr
        zbr = er - tr
        zai = ei + ti
        zbi = ti - ei
        if r0 == 0:
            first = lax.broadcasted_iota(jnp.int32, er.shape, 0) == 0
            zai = jnp.where(first, zbr, zai)
            zbi = jnp.where(first, -oi, zbi)
            zbr = jnp.where(first, ei, zbr)
        yar = zar * gar - zai * gai
        yai = zar * gai + zai * gar
        if r0 == 0:
            yar = jnp.where(first, zar * gar, yar)
            yai = jnp.where(first, zai * gai, yai)
        ybr = zbr * gbr - zbi * gbi
        ybi = zbr * gbi + zbi * gbr
        dr = yar - ybr
        di = yai + ybi
        p0r = 0.5 * (yar + ybr)
        p0i = 0.5 * (yai - ybi)
        p1r = 0.5 * (c * dr - s * di)
        p1i = 0.5 * (c * di + s * dr)
        if r0 == 0:
            p0r = jnp.where(first, 0.5 * (yar + yai), p0r)
            p0i = jnp.where(first, ybr, p0i)
            p1r = jnp.where(first, 0.5 * (yar - yai), p1r)
            p1i = jnp.where(first, -ybi, p1i)
        f0r.append(p0r.astype(BF16))
        f0i.append(p0i.astype(BF16))
        f1r.append(p1r.astype(BF16))
        f1i.append(p1i.astype(BF16))
    y_even = _dot(b_ref[...], jnp.concatenate(f0r + f0i, axis=0))
    y_odd = _dot(b_ref[...], jnp.concatenate(f1r + f1i, axis=0))
    return y_even, y_odd


def _hyena_order_kernel(*refs, ctx, seq, conv_z):
    if conv_z:
        (z_ref, zw_ref, zb_ref, x_ref, xw_ref, xb_ref, d_ref, ah_ref, bh_ref, ac_ref, bc_ref, gl_ref, gc_ref,
         tw_ref, o_ref, zs_ref, gs_ref, os_ref) = refs
        z = _short_conv_rows(z_ref[...], zw_ref[...], zb_ref[...], ctx)
    else:
        (z_ref, x_ref, xw_ref, xb_ref, d_ref, ah_ref, bh_ref, ac_ref, bc_ref, gl_ref, gc_ref,
         tw_ref, o_ref, zs_ref, gs_ref, os_ref) = refs
        z = z_ref[...].astype(F32)
    gate = _short_conv_rows(x_ref[...], xw_ref[...], xb_ref[...], ctx)
    d = d_ref[...]
    half = seq // 2
    n_blk = zs_ref.shape[0]

    zc = z[0:ctx, :]
    yc = _dot(bc_ref[...], _spectral_product(ac_ref, gc_ref, zc.astype(BF16), ctx))
    o_ref[0:ctx, :] = (gate[0:ctx, :] * (yc + d * zc)).astype(o_ref.dtype)

    for j in range(n_blk):
        zs_ref[j] = z[ctx:, j * LANES:(j + 1) * LANES]
        gs_ref[j] = gate[ctx:, j * LANES:(j + 1) * LANES]

    def samples(ref, parity):
        return jnp.concatenate([ref[j, pl.ds(parity, half, stride=2), :] for j in range(n_blk)], axis=1)

    ze = samples(zs_ref, 0)
    zo = samples(zs_ref, 1)
    y_even, y_odd = _radix2_conv(ze.astype(BF16), zo.astype(BF16), ah_ref, bh_ref, gl_ref, tw_ref)
    for parity, y, zp in ((0, y_even, ze), (1, y_odd, zo)):
        out = samples(gs_ref, parity) * (y + d * zp)
        for j in range(n_blk):
            os_ref[j, pl.ds(parity, half, stride=2), :] = out[:, j * LANES:(j + 1) * LANES]
    for j in range(n_blk):
        o_ref[ctx:, j * LANES:(j + 1) * LANES] = os_ref[j].astype(o_ref.dtype)


def _conv_specs(blk):
    n_ch = MIX_B // HY_CH
    return [pl.BlockSpec((3, HY_CH), lambda j, i: (0, blk * n_ch + j)),
            pl.BlockSpec((1, HY_CH), lambda j, i: (0, blk * n_ch + j))]


def _hyena_order(z, conv_z, hy, xblk, conv_x, dskip, order, mats, g_lat, g_ctx, twiddle, ctx, seq, out_dtype):
    a_half, b_half, a_ctx, b_ctx = mats
    b, tok, _ = hy.shape
    n_ch = MIX_B // HY_CH
    has_z = conv_z is not None
    const = lambda j, i: (0, 0)
    return pl.pallas_call(
        functools.partial(_hyena_order_kernel, ctx=ctx, seq=seq, conv_z=has_z),
        grid=(n_ch, b),
        in_specs=[pl.BlockSpec((None, tok, HY_CH), lambda j, i: (i, 0, j))]
        + (_conv_specs(0) if has_z else [])
        + [pl.BlockSpec((None, tok, HY_CH), lambda j, i: (i, 0, xblk * n_ch + j))]
        + _conv_specs(xblk)
        + [
            pl.BlockSpec((None, 1, HY_CH), lambda j, i: (order, 0, j)),
            _resident(a_half.shape, const), _resident(b_half.shape, const),
            _resident(a_ctx.shape, const), _resident(b_ctx.shape, const),
            pl.BlockSpec((2 * seq, HY_CH), lambda j, i: (0, order * n_ch + j)),
            pl.BlockSpec((2 * ctx, HY_CH), lambda j, i: (0, order * n_ch + j)),
            _resident(twiddle.shape, const),
        ],
        out_specs=pl.BlockSpec((None, tok, HY_CH), lambda j, i: (i, 0, j)),
        out_shape=jax.ShapeDtypeStruct((b, tok, MIX_B), out_dtype),
        scratch_shapes=[pltpu.VMEM((HY_CH // LANES, seq, LANES), F32)] * 3,
        compiler_params=_cparams("parallel", "parallel"),
        name="hyena_order",
    )(z, *(conv_z if has_z else ()), hy, *conv_x, dskip, a_half, b_half, a_ctx, b_ctx, g_lat, g_ctx, twiddle)


def _spectrum_bfly_kernel(a_ref, hf_ref, hb_ref, o_ref):
    q = pl.program_id(0)
    hf = hf_ref[...]
    hb = hb_ref[...]
    t = lax.broadcasted_iota(jnp.int32, hf.shape, 0)
    alt = jnp.where(t % 2 == 0, 1.0, -1.0)
    comb = jnp.where(q % 2 == 0, hf + hb, hf - hb) * jnp.where(q < 2, 1.0, alt)
    res = _dot(a_ref[...], comb.astype(BF16))
    o_ref[...] = jnp.where(q == 3, -res, res)

    quarter = t % 4

    @pl.when(q == 1)
    def _():
        o_ref[0:1, :] = jnp.sum(alt * (hf + hb), axis=0, keepdims=True)

    @pl.when(q == 2)
    def _():
        w = jnp.where(quarter == 0, 1.0, jnp.where(quarter == 2, -1.0, 0.0))
        o_ref[0:1, :] = jnp.sum(w * (hf + hb), axis=0, keepdims=True)

    @pl.when(q == 3)
    def _():
        w = jnp.where(quarter == 1, -1.0, jnp.where(quarter == 3, 1.0, 0.0))
        o_ref[0:1, :] = jnp.sum(w * (hf - hb), axis=0, keepdims=True)


def _filter_spectrum_bfly(fwd, filt):
    period, length = fwd.shape
    n = length // 2
    orders = filt.shape[1] // (2 * MIX_B)
    return pl.pallas_call(
        _spectrum_bfly_kernel,
        grid=(4, orders),
        in_specs=[
            pl.BlockSpec((n, length), lambda q, o: ((q % 2) * (length // n), 0)),
            pl.BlockSpec((length, MIX_B), lambda q, o: (0, 2 * o)),
            pl.BlockSpec((length, MIX_B), lambda q, o: (0, 2 * o + 1)),
        ],
        out_specs=pl.BlockSpec((n, MIX_B), lambda q, o: (q, o)),
        out_shape=jax.ShapeDtypeStruct((period, orders * MIX_B), F32),
        compiler_params=_cparams("arbitrary", "arbitrary"),
        name="hyena_filter_spectrum_bfly",
    )(fwd, filt, filt)


def _twiddles(length):
    n = length // 2
    ang = jnp.arange(n, dtype=F32) * (2.0 * math.pi / (2 * length))
    tw = jnp.concatenate([jnp.cos(ang), jnp.sin(ang)])[:, None]
    return jnp.broadcast_to(tw, (2 * n, LANES))


def _odd_in_kernel(x_ref, shc_ref, shb_ref, scc_ref, scb_ref, w_ref, qn_ref, kn_ref, q_ref, k_ref, v_ref,
                   *, n_sub, per_batch):
    lane = lax.broadcasted_iota(jnp.int32, (SUB_ROWS, LANES), 1)
    lo = lane < NA_HD
    half = D_MODEL // 2
    for sb in range(n_sub):
        rows = _sub_rows(sb)
        h = _normmod(x_ref[rows, :], _sub_mod(shc_ref, shb_ref, sb, per_batch),
                     _sub_mod(scc_ref, scb_ref, sb, per_batch)).astype(BF16)
        for part, gain_ref, scale, dst in ((0, qn_ref, NA_HD ** -0.5 * LOG2E, q_ref), (1, kn_ref, 1.0, k_ref)):
            gain = gain_ref[...]
            for c in range(2):
                base = part * D_MODEL + c * half
                acc = _dot(h, w_ref[:, base:base + half])
                for g in range(half // LANES):
                    a = acc[:, g * LANES:(g + 1) * LANES]
                    sq = a * a
                    s_lo = jnp.sum(jnp.where(lo, sq, 0.0), axis=-1, keepdims=True)
                    s_hi = jnp.sum(jnp.where(lo, 0.0, sq), axis=-1, keepdims=True)
                    inv = lax.rsqrt(jnp.where(lo, s_lo, s_hi) * (1.0 / NA_HD) + EPS)
                    y = a * inv * gain
                    if scale != 1.0:
                        y = y * scale
                    dst[rows, c * half + g * LANES:c * half + (g + 1) * LANES] = y.astype(BF16)
        for c in range(2):
            base = 2 * D_MODEL + c * half
            v_ref[rows, c * half:(c + 1) * half] = _dot(h, w_ref[:, base:base + half]).astype(BF16)


def _odd_in_proj(x2, mod5, layer, tiles, w_qkv, qn2, kn2):
    m = x2.shape[0]
    out = jax.ShapeDtypeStruct((m, D_MODEL), BF16)
    return pl.pallas_call(
        functools.partial(_odd_in_kernel, n_sub=tiles.n_sub, per_batch=tiles.per_batch),
        grid=(tiles.n,),
        in_specs=[tiles.row_spec(D_MODEL)] + tiles.mod_specs(layer, 0) + tiles.mod_specs(layer, 1) + [
            _resident((D_MODEL, 3 * D_MODEL), lambda i: (0, 0)),
            pl.BlockSpec((1, LANES), lambda i: (0, 0)), pl.BlockSpec((1, LANES), lambda i: (0, 0)),
        ],
        out_specs=[tiles.row_spec(D_MODEL)] * 3,
        out_shape=(out, out, out),
        compiler_params=_cparams("parallel"),
        name="odd_qkv_proj",
    )(x2, mod5, mod5, mod5, mod5, w_qkv, qn2, kn2)


def _pair_queries(q):
    lane = lax.broadcasted_iota(jnp.int32, q.shape, 1)
    zero = jnp.zeros_like(q)
    return jnp.concatenate([jnp.where(lane < NA_HD, q, zero), jnp.where(lane < NA_HD, zero, q)], axis=0)


def _unpair(res, n):
    lane = lax.broadcasted_iota(jnp.int32, (n, LANES), 1)
    return jnp.where(lane < NA_HD, res[0:n, :], res[n:2 * n, :])


def _natten_kernel(q_ref, k_ref, v_ref, t_ref, o_ref, va_ref, *, ctx, rows_n, win_rows):
    tok = v_ref.shape[0]
    va_ref[:, 0:LANES] = v_ref[...]
    va_ref[:, LANES:2 * LANES] = jnp.ones((tok, LANES), BF16)
    kc = k_ref[0:ctx, :]
    vc = va_ref[0:ctx, :]

    def softmax_pv(scores, values):
        m = functools.reduce(jnp.maximum, [jnp.max(s, axis=1, keepdims=True) for s in scores])
        res = None
        for s, vblk in zip(scores, values):
            part = _dot(jnp.exp2(s - m).astype(BF16), vblk)
            res = part if res is None else res + part
        return res[:, 0:LANES] * (1.0 / res[:, LANES:2 * LANES])

    w = _pair_queries(q_ref[0:ctx, :])
    o_ref[0:ctx, :] = _unpair(softmax_pv([_dot_nt(w, kc)], [vc]), ctx).astype(BF16)

    nk = win_rows * GRID_W

    def body(r, carry):
        rs = jnp.clip(r - win_rows // 2, 0, rows_n - win_rows)
        dr0 = rs - r + NA_WIN_ROWS - 1
        q_off = pl.multiple_of(ctx + r * GRID_W, GRID_W)
        k_off = pl.multiple_of(ctx + rs * GRID_W, GRID_W)
        w = _pair_queries(q_ref[pl.ds(q_off, GRID_W), :])
        bias = t_ref[dr0 & 1, :, pl.ds(pl.multiple_of((dr0 >> 1) * LANES, LANES), nk)]
        s_lat = _dot_nt(w, k_ref[pl.ds(k_off, nk), :]) + bias
        s_ctx = _dot_nt(w, kc)
        out = softmax_pv([s_lat, s_ctx], [va_ref[pl.ds(k_off, nk), :], vc])
        o_ref[pl.ds(q_off, GRID_W), :] = _unpair(out, GRID_W).astype(BF16)
        return carry

    lax.fori_loop(0, rows_n, body, 0, unroll=min(NA_UNROLL, rows_n))


def _natten(q, k, v, table, ctx, seq):
    b, tok, _ = q.shape
    rows_n = seq // GRID_W
    assert rows_n >= NA_WIN_ROWS and rows_n % min(NA_UNROLL, rows_n) == 0
    pairs = NA_HEADS // 2
    pair_spec = pl.BlockSpec((None, tok, LANES), lambda p, i: (i, 0, p))
    return pl.pallas_call(
        functools.partial(_natten_kernel, ctx=ctx, rows_n=rows_n, win_rows=NA_WIN_ROWS),
        grid=(pairs, b),
        in_specs=[
            pair_spec, pair_spec, pair_spec,
            pl.BlockSpec((None,) + table.shape[1:], lambda p, i: (p, 0, 0, 0)),
        ],
        out_specs=pair_spec,
        out_shape=jax.ShapeDtypeStruct((b, tok, D_MODEL), BF16),
        scratch_shapes=[pltpu.VMEM((tok, 2 * LANES), BF16)],
        compiler_params=_cparams("parallel", "parallel"),
        name="natten",
    )(q, k, v, table)


def _natten_bias_table(rpb):
    n_dr = 2 * NA_WIN_ROWS - 1
    n_dc = 2 * NA_WIN_COLS - 1
    cidx = np.arange(GRID_W)
    cstart = np.clip(cidx - NA_WIN_COLS // 2, 0, GRID_W - NA_WIN_COLS)
    cmask = (cidx[None, :] >= cstart[:, None]) & (cidx[None, :] < cstart[:, None] + NA_WIN_COLS)
    dc = np.clip(cidx[None, :] - cidx[:, None] + NA_WIN_COLS - 1, 0, n_dc - 1)
    onehot = ((dc[None] == np.arange(n_dc)[:, None, None]) & cmask[None]).astype(np.float32)
    tb = jnp.einsum("hrc,cqk->hrqk", rpb.astype(F32), jnp.asarray(onehot), precision=lax.Precision.HIGHEST)
    tb = tb * LOG2E + jnp.asarray(np.where(cmask, 0.0, NEG_BIG), F32)
    tb = tb.reshape(NA_HEADS // 2, 2, n_dr, GRID_W, GRID_W).transpose(0, 1, 3, 2, 4)
    tb = tb.reshape(NA_HEADS // 2, 2 * GRID_W, n_dr * GRID_W)
    width = (n_dr + 1) * GRID_W
    tb = jnp.pad(tb, ((0, 0), (0, 0), (0, width + GRID_W - n_dr * GRID_W)))
    return jnp.stack([tb[:, :, 0:width], tb[:, :, GRID_W:GRID_W + width]], axis=1)


FF_CHUNK = 1024


def _mix_mlp_kernel(*refs, widths, n_sub, per_batch, latent_only):
    x_ref = refs[0]
    g1, sh, sc, g2 = (refs[1 + 2 * j:3 + 2 * j] for j in range(4))
    in_refs = refs[9:9 + len(widths)]
    wo_ref, w1_ref, w2_ref, o_ref = refs[9 + len(widths):]

    def compute(sb):
        rows = _sub_rows(sb)
        mix = None
        off = 0
        for r, wd in zip(in_refs, widths):
            part = _dot(r[rows, :].astype(BF16), wo_ref[off:off + wd, :])
            mix = part if mix is None else mix + part
            off += wd
        x = x_ref[rows, :] + _sub_mod(*g1, sb, per_batch) * mix
        h = _normmod(x, _sub_mod(*sh, sb, per_batch), _sub_mod(*sc, sb, per_batch)).astype(BF16)
        acc = None
        for c0 in range(0, D_FF, FF_CHUNK):
            a = jnp.maximum(_dot(h, w1_ref[:, c0:c0 + FF_CHUNK]), 0.0)
            part = _dot((a * a).astype(BF16), w2_ref[c0:c0 + FF_CHUNK, :])
            acc = part if acc is None else acc + part
        o_ref[rows, :] = x + _sub_mod(*g2, sb, per_batch) * acc

    if latent_only:
        assert n_sub == 1
        pl.when(pl.program_id(0) % per_batch != 0)(functools.partial(compute, 0))
    else:
        for sb in range(n_sub):
            compute(sb)


def _mix_mlp(x2, mod5, layer, tiles, parts, w_out, w1, w2, latent_only):
    widths = tuple(p.shape[1] for p in parts)
    pb = tiles.per_batch
    if latent_only:
        out_rows = tiles.batch * (tiles.tok - tiles.ctx)
        out_spec = pl.BlockSpec((tiles.rows, D_MODEL), lambda i: (i - i // pb - jnp.where(i % pb == 0, 0, 1), 0))
    else:
        out_rows = x2.shape[0]
        out_spec = tiles.row_spec(D_MODEL)
    mods = [s for which in (2, 3, 4, 5) for s in tiles.mod_specs(layer, which)]
    return pl.pallas_call(
        functools.partial(_mix_mlp_kernel, widths=widths, n_sub=tiles.n_sub, per_batch=pb, latent_only=latent_only),
        grid=(tiles.n,),
        in_specs=[tiles.row_spec(D_MODEL)] + mods
        + [tiles.row_spec(wd) for wd in widths]
        + [_resident(w_out.shape, lambda i: (0, 0)), _resident(w1.shape, lambda i: (0, 0)),
           _resident(w2.shape, lambda i: (0, 0))],
        out_specs=out_spec,
        out_shape=jax.ShapeDtypeStruct((out_rows, D_MODEL), F32),
        input_output_aliases={} if latent_only else {0: 0},
        compiler_params=_cparams("arbitrary" if latent_only else "parallel"),
        name="mix_out_mlp",
    )(x2, *([mod5] * len(mods)), *parts, w_out, w1, w2)


def _pad_cols(a, width):
    return jnp.pad(a, ((0, 0), (0, width - a.shape[1])))


def kernel(x, c, ctx, c_ctx, w_mod, b_mod, w_mlp_in, w_mlp_out, e_w_in, e_gate_b, e_hnorm, e_conv_w, e_conv_b,
           e_f_w1, e_f_b1, e_f_w2, e_f_b2, e_f_w3, e_f_freq, e_hy_d, e_w_out, o_w_qkv, o_qn, o_kn, o_rpb, o_w_out):
    batch, seq, d = x.shape
    n_ctx = ctx.shape[1]
    depth = w_mod.shape[0]
    tok = n_ctx + seq
    assert d == D_MODEL and seq % CHUNK == 0 and n_ctx == CHUNK and seq % GRID_W == 0
    tiles = _Tiles(batch, tok, n_ctx, TILE_ROWS if tok % TILE_ROWS == 0 else SUB_ROWS)
    tiles_last = _Tiles(batch, tok, n_ctx, SUB_ROWS)

    n_samp = -(-(batch + 1) // 8) * 8
    cvec = jnp.concatenate([c, c_ctx[None, :], jnp.zeros((n_samp - batch - 1, d), F32)], axis=0)
    mod5 = _modulation(cvec, w_mod, b_mod)

    cos_t, sin_t = _rope_tables(n_ctx, seq)
    a_lat, _ = _dft_matrices(seq)
    a_ctx, b_ctx = _dft_matrices(n_ctx)
    dft_mats = _dft_matrices(seq // 2) + (a_ctx, b_ctx)
    twiddle = _twiddles(seq)

    xs = jnp.concatenate([ctx, x], axis=1).reshape(batch * tok, d)

    for l in range(depth):
        i = l // 2
        if l % 2 == 0:
            w_in = e_w_in[i]
            g0 = 4 * MIX_A
            w_pad = jnp.concatenate(
                [w_in[:, :g0], _pad_cols(w_in[:, g0:g0 + N_GATES], LANES), w_in[:, g0 + N_GATES:]], axis=1).astype(BF16)
            gate_b = _pad_cols(e_gate_b[i][None, :], LANES)
            q, k_t, v, sg, g_t, hy = _even_in_proj(xs, mod5, l, tiles, w_pad, gate_b, cos_t, sin_t)
            as3 = lambda a: a.reshape(batch, tok, a.shape[-1])
            a_out = _mlstm(as3(q), k_t, as3(v), as3(sg), g_t, e_hnorm[i][None, :])

            hy3 = as3(hy)
            conv = (e_conv_w[i], e_conv_b[i][None, :])
            pad2 = lambda a: jnp.pad(a, ((0, LANES - a.shape[0]), (0, LANES - a.shape[1])))
            w1p = pad2(e_f_w1[i]).astype(BF16)
            w2p = pad2(e_f_w2[i]).astype(BF16)
            w3p = jnp.pad(e_f_w3[i], ((0, LANES - HYENA_FFN), (0, 0))).astype(BF16)
            b1p = _pad_cols(e_f_b1[i][None, :], LANES)
            b2p = _pad_cols(e_f_b2[i][None, :], LANES)
            frp = _pad_cols(e_f_freq[i][None, :], LANES)
            g_lat = _filter_spectrum_bfly(a_lat, _hyena_filters(seq, w1p, b1p, w2p, b2p, frp, w3p))
            g_ctx = _filter_spectrum(a_ctx, _hyena_filters(n_ctx, w1p, b1p, w2p, b2p, frp, w3p))
            dskip = e_hy_d[i][:, None, :]
            n_ord = e_hy_d.shape[1]
            z, conv_z = hy3, conv
            for o in range(n_ord):
                z = _hyena_order(z, conv_z, hy3, 1 + o, conv, dskip, o, dft_mats, g_lat, g_ctx, twiddle,
                                 n_ctx, seq, F32)
                conv_z = None
            parts = [a_out.reshape(batch * tok, MIX_A), z.reshape(batch * tok, MIX_B)]
            w_out = e_w_out[i].astype(BF16)
        else:
            rep = LANES // NA_HD
            qn2 = jnp.tile(o_qn[i], rep)[None, :]
            kn2 = jnp.tile(o_kn[i], rep)[None, :]
            q, k, v = _odd_in_proj(xs, mod5, l, tiles, o_w_qkv[i].astype(BF16), qn2, kn2)
            as3 = lambda a: a.reshape(batch, tok, D_MODEL)
            table = _natten_bias_table(o_rpb[i])
            att = _natten(as3(q), as3(k), as3(v), table, n_ctx, seq)
            parts = [att.reshape(batch * tok, D_MODEL)]
            w_out = o_w_out[i].astype(BF16)
        last = l == depth - 1
        xs = _mix_mlp(xs, mod5, l, tiles_last if last else tiles, parts, w_out,
                      w_mlp_in[l].astype(BF16), w_mlp_out[l].astype(BF16), latent_only=last)

    return xs.reshape(batch, seq, d)
```

```python
import functools
import math

import numpy as np
import jax
import jax.numpy as jnp
from jax import lax
from jax.experimental import pallas as pl
from jax.experimental.pallas import tpu as pltpu

F32 = jnp.float32
BF16 = jnp.bfloat16

D_MODEL = 1024
D_FF = 4 * D_MODEL
EPS = 1e-6
ROPE_BASE = 10000.0
GRID_W = 64
MIX_A = D_MODEL // 2
MIX_B = D_MODEL - MIX_A
MLSTM_HEADS = 4
MLSTM_HD = MIX_A // MLSTM_HEADS
N_GATES = 4 * MLSTM_HEADS
HYENA_EMB = 33
HYENA_FFN = 64
HYENA_TARGET = 1e-2
HYENA_FAST = 0.3
HYENA_SLOW = 1.5
NA_HEADS = 16
NA_HD = D_MODEL // NA_HEADS
NA_WIN_ROWS = 8
NA_WIN_COLS = 16

LANES = 128
SUB_ROWS = 256
TILE_ROWS = 768
CHUNK = 256
VMEM_LIMIT = 56 * 1024 * 1024
NEG_BIG = -1e30
LOG2E = math.log2(math.e)
DFT_SPLIT = 64
NA_UNROLL = 32


def _cparams(*sem):
    return pltpu.CompilerParams(dimension_semantics=sem, vmem_limit_bytes=VMEM_LIMIT)


def _dot(a, b):
    return jnp.dot(a, b, preferred_element_type=F32)


def _dot_nt(a, b):
    return lax.dot_general(a, b, (((1,), (1,)), ((), ())), preferred_element_type=F32)


def _dot_tn(a, b):
    return lax.dot_general(a, b, (((0,), (0,)), ((), ())), preferred_element_type=F32)


def _resident(shape, index_map):
    return pl.BlockSpec(shape, index_map, pipeline_mode=pl.Buffered(1))


def _normmod(x, sh, sc):
    ms = jnp.mean(x * x, axis=-1, keepdims=True)
    return (x * lax.rsqrt(ms + EPS)) * (1.0 + sc) + sh


def _sigmoid(x):
    return 1.0 / (1.0 + jnp.exp(-x))


def _log_sigmoid(x):
    return jnp.minimum(x, 0.0) - jnp.log(1.0 + jnp.exp(-jnp.abs(x)))


def _mod_kernel(c_ref, w_ref, b_ref, o_ref):
    c = c_ref[...]
    s = (c * _sigmoid(c)).astype(BF16)
    o_ref[...] = _dot(s, w_ref[...].astype(BF16)) + b_ref[...]


def _modulation(cvec, w_mod, b_mod):
    depth, d, d6 = w_mod.shape
    ns = cvec.shape[0]
    nj = d6 // d
    out = pl.pallas_call(
        _mod_kernel,
        grid=(depth, nj),
        in_specs=[
            pl.BlockSpec((ns, d), lambda l, j: (0, 0)),
            pl.BlockSpec((None, d, d), lambda l, j: (l, 0, j)),
            pl.BlockSpec((None, 1, d), lambda l, j: (l, 0, j)),
        ],
        out_specs=pl.BlockSpec((None, ns, d), lambda l, j: (l, 0, j)),
        out_shape=jax.ShapeDtypeStruct((depth, ns, d6), F32),
        compiler_params=_cparams("arbitrary", "arbitrary"),
        name="modulation",
    )(cvec, w_mod, b_mod.reshape(depth, 1, d6))
    return out.reshape(depth, ns, nj, 1, d)


class _Tiles:
    def __init__(self, batch, tok, ctx, rows):
        assert ctx == SUB_ROWS and rows % SUB_ROWS == 0 and tok % rows == 0
        self.batch, self.tok, self.ctx, self.rows = batch, tok, ctx, rows
        self.n_sub = rows // SUB_ROWS
        self.per_batch = tok // rows
        self.n = batch * self.per_batch

    def mod_specs(self, layer, which):
        blk = (None, None, None, 1, D_MODEL)
        return [pl.BlockSpec(blk, lambda i: (layer, self.batch, which, 0, 0)),
                pl.BlockSpec(blk, lambda i: (layer, i // self.per_batch, which, 0, 0))]

    def row_spec(self, width):
        return pl.BlockSpec((self.rows, width), lambda i: (i, 0))

    def pos_spec(self, width):
        return pl.BlockSpec((self.rows, width), lambda i: (i % self.per_batch, 0))


def _sub_mod(c_ref, b_ref, sb, per_batch):
    if sb > 0:
        return b_ref[...]
    return jnp.where(pl.program_id(0) % per_batch == 0, c_ref[...], b_ref[...])


def _sub_rows(sb):
    return slice(sb * SUB_ROWS, (sb + 1) * SUB_ROWS)


def _even_in_kernel(*refs, n_sub, per_batch, split_in):
    if split_in:
        ctx_ref, lat_refs, refs = refs[0], refs[1:1 + n_sub], refs[1 + n_sub:]
    else:
        x_ref, refs = refs[0], refs[1:]
    (shc_ref, shb_ref, scc_ref, scb_ref, w_ref, gb_ref, cos_ref, sin_ref,
     q_ref, kt_ref, v_ref, sg_ref, gt_ref, hy_ref) = refs[:14]
    lane = lax.broadcasted_iota(jnp.int32, (SUB_ROWS, MLSTM_HD), 1)
    first = (lane % (MLSTM_HD // 2)) < (MLSTM_HD // 4)
    g0 = 4 * MIX_A
    h0 = g0 + LANES
    for sb in range(n_sub):
        rows = _sub_rows(sb)
        if split_in:
            x = lat_refs[sb][...]
            if sb == 0:
                x = jnp.where(pl.program_id(0) % per_batch == 0, ctx_ref[...], x)
            refs[14][rows, :] = x
        else:
            x = x_ref[rows, :]
        h = _normmod(x, _sub_mod(shc_ref, shb_ref, sb, per_batch),
                     _sub_mod(scc_ref, scb_ref, sb, per_batch)).astype(BF16)
        cos = cos_ref[rows, :]
        sin = sin_ref[rows, :]

        def rope(a):
            part = jnp.where(first, pltpu.roll(a, LANES - MLSTM_HD // 4, 1), pltpu.roll(a, MLSTM_HD // 4, 1))
            return a * cos + part * sin

        acc = _dot(h, w_ref[:, 0:MIX_A])
        for hd in range(MLSTM_HEADS):
            sl = slice(hd * MLSTM_HD, (hd + 1) * MLSTM_HD)
            q_ref[rows, sl] = rope(acc[:, sl]).astype(BF16)
        acc = _dot(h, w_ref[:, MIX_A:2 * MIX_A]) * (MLSTM_HD ** -0.5)
        for hd in range(MLSTM_HEADS):
            sl = slice(hd * MLSTM_HD, (hd + 1) * MLSTM_HD)
            kt_ref[sl, rows] = rope(acc[:, sl]).T.astype(BF16)
        v_ref[rows, :] = _dot(h, w_ref[:, 2 * MIX_A:3 * MIX_A]).astype(BF16)
        sg_ref[rows, :] = _sigmoid(_dot(h, w_ref[:, 3 * MIX_A:4 * MIX_A])).astype(BF16)
        gt_ref[:, rows] = (_dot(h, w_ref[:, g0:g0 + LANES]) + gb_ref[...]).T
        for j in range(3):
            hy_ref[rows, j * MIX_B:(j + 1) * MIX_B] = _dot(h, w_ref[:, h0 + j * MIX_B:h0 + (j + 1) * MIX_B])


def _even_in_proj(x2, mod5, layer, tiles, w_pad, gate_b_pad, cos_t, sin_t):
    split_in = isinstance(x2, tuple)
    m = tiles.batch * tiles.tok
    n_w = w_pad.shape[1]
    outs = (
        jax.ShapeDtypeStruct((m, MIX_A), BF16), jax.ShapeDtypeStruct((MIX_A, m), BF16),
        jax.ShapeDtypeStruct((m, MIX_A), BF16), jax.ShapeDtypeStruct((m, MIX_A), BF16),
        jax.ShapeDtypeStruct((LANES, m), F32), jax.ShapeDtypeStruct((m, 3 * MIX_B), F32),
    )
    out_specs = [tiles.row_spec(MIX_A), pl.BlockSpec((MIX_A, tiles.rows), lambda i: (0, i)),
                 tiles.row_spec(MIX_A), tiles.row_spec(MIX_A),
                 pl.BlockSpec((LANES, tiles.rows), lambda i: (0, i)), tiles.row_spec(3 * MIX_B)]
    if split_in:
        pb, ns = tiles.per_batch, tiles.n_sub
        lat_blocks = (tiles.tok - tiles.ctx) // SUB_ROWS

        def lat_spec(sb):
            return pl.BlockSpec((SUB_ROWS, D_MODEL), lambda i: (
                (i // pb) * lat_blocks + jnp.maximum((i % pb) * ns + sb - 1, 0), 0))

        x_specs = [pl.BlockSpec((SUB_ROWS, D_MODEL), lambda i: (i // pb, 0))] + [lat_spec(sb) for sb in range(ns)]
        x_args = (x2[0],) + (x2[1],) * ns
        outs = outs + (jax.ShapeDtypeStruct((m, D_MODEL), F32),)
        out_specs = out_specs + [tiles.row_spec(D_MODEL)]
    else:
        x_specs = [tiles.row_spec(D_MODEL)]
        x_args = (x2,)
    return pl.pallas_call(
        functools.partial(_even_in_kernel, n_sub=tiles.n_sub, per_batch=tiles.per_batch, split_in=split_in),
        grid=(tiles.n,),
        in_specs=x_specs + tiles.mod_specs(layer, 0) + tiles.mod_specs(layer, 1) + [
            _resident((D_MODEL, n_w), lambda i: (0, 0)),
            pl.BlockSpec((1, LANES), lambda i: (0, 0)),
            tiles.pos_spec(MLSTM_HD), tiles.pos_spec(MLSTM_HD),
        ],
        out_specs=out_specs,
        out_shape=outs,
        compiler_params=_cparams("parallel"),
        name="even_in_proj",
    )(*x_args, mod5, mod5, mod5, mod5, w_pad, gate_b_pad, cos_t, sin_t)


def _rope_tables(ctx, seq):
    half = MLSTM_HD // 2
    nf = half // 2
    inv = ROPE_BASE ** (-np.arange(nf, dtype=np.float64) / nf)
    t = np.arange(seq)
    rows, cols = t // GRID_W, t % GRID_W

    def one(pos):
        ang = pos[:, None].astype(np.float64) * inv[None, :]
        c = np.concatenate([np.cos(ang), np.cos(ang)], axis=-1)
        s = np.concatenate([-np.sin(ang), np.sin(ang)], axis=-1)
        return c, s

    cr, sr = one(rows)
    cc, sc = one(cols)
    cos = np.concatenate([cr, cc], axis=-1)
    sin = np.concatenate([sr, sc], axis=-1)
    cos = np.concatenate([np.ones((ctx, MLSTM_HD)), cos], axis=0)
    sin = np.concatenate([np.zeros((ctx, MLSTM_HD)), sin], axis=0)
    return jnp.asarray(cos, F32), jnp.asarray(sin, F32)


def _seg_scan(y, pos, op, reverse, axis):
    n = y.shape[axis]
    k = 1
    while k < CHUNK:
        if reverse:
            y = jnp.where(pos < CHUNK - k, op(y, pltpu.roll(y, n - k, axis)), y)
        else:
            y = jnp.where(pos >= k, op(y, pltpu.roll(y, k, axis)), y)
        k *= 2
    return y


def _mlstm_gate_scans(gt_ref, qc_refs, ar_ref):
    hh = MLSTM_HEADS
    g = gt_ref[...]
    gi = jnp.concatenate([g[0:hh], g[2 * hh:3 * hh]], axis=0) * LOG2E
    lf = _log_sigmoid(jnp.concatenate([g[hh:2 * hh], g[3 * hh:4 * hh]], axis=0)) * LOG2E
    fwd = lax.broadcasted_iota(jnp.int32, lf.shape, 0) < hh
    pos = lax.broadcasted_iota(jnp.int32, lf.shape, 1) % CHUNK
    b = jnp.where(fwd, _seg_scan(lf, pos, jnp.add, False, 1), _seg_scan(lf, pos, jnp.add, True, 1))
    a = gi - b
    amax = jnp.where(fwd, _seg_scan(a, pos, jnp.maximum, False, 1), _seg_scan(a, pos, jnp.maximum, True, 1))
    ar_ref[...] = a
    stack = jnp.concatenate([b, amax], axis=0)
    hi = stack.astype(BF16)
    rest = stack - hi.astype(F32)
    mid = rest.astype(BF16)
    lo = (rest - mid.astype(F32)).astype(BF16)
    n = stack.shape[0]
    nq = b.shape[0]
    eye = (lax.broadcasted_iota(jnp.int32, (n, LANES), 0) == lax.broadcasted_iota(jnp.int32, (n, LANES), 1))
    eye = eye.astype(F32).astype(BF16)
    cols = (_dot_tn(hi, eye) + _dot_tn(mid, eye)) + _dot_tn(lo, eye)
    for j, ref in enumerate(qc_refs):
        ref[...] = cols if j == 0 else pltpu.roll(cols, LANES - j * nq, 1)


def _mlstm_chunk(qc, ktc, vc, b_col, amax_col, a_row, state, m, lower):
    t, dv = vc.shape
    r = lax.broadcasted_iota(jnp.int32, (t, t), 0)
    c = lax.broadcasted_iota(jnp.int32, (t, t), 1)
    incl = (c <= r) if lower else (c >= r)
    mm = jnp.broadcast_to(jnp.maximum(m, amax_col), (t, dv))
    b_rows = jnp.broadcast_to(b_col, (t, dv))
    sc = jnp.exp2(m - mm)
    floor = jnp.exp2(-(b_rows + mm))
    rep = t // dv
    w = jnp.exp2(jnp.where(incl, a_row - jnp.concatenate([mm] * rep, axis=1), -jnp.inf))
    qkw = _dot(qc, ktc) * w
    v_aug = jnp.concatenate([vc, jnp.ones_like(vc)], axis=1)
    res = jnp.concatenate([sc, sc], axis=1) * _dot(qc, state.astype(BF16)) + _dot(qkw.astype(BF16), v_aug)
    h = res[:, :dv] / jnp.maximum(jnp.abs(res[:, dv:]), floor)
    e = t - 1 if lower else 0
    bl = b_col[e:e + 1, :]
    m_new = bl + jnp.maximum(m, amax_col[e:e + 1, :])
    decay = jnp.exp2(bl + m - m_new)
    kw_t = (ktc.astype(F32) * jnp.exp2(bl + a_row - m_new)).astype(BF16)
    return h, decay * state + _dot(kw_t, v_aug), m_new


def _mlstm_kernel(q_ref, kt_ref, v_ref, sg_ref, gt_ref, hn_ref, o_ref, hf_ref, hb_ref, bq_ref, mq_ref,
                  ar_ref, st_ref, *, n_chunks):
    _mlstm_gate_scans(gt_ref, (bq_ref, mq_ref), ar_ref)
    st_ref[...] = jnp.zeros(st_ref.shape, F32)

    def run(off, hd, m, lower, dst_ref):
        r = hd if lower else MLSTM_HEADS + hd
        slot = 2 * hd + (0 if lower else 1)
        rows = pl.ds(off, CHUNK)
        cols = slice(hd * MLSTM_HD, (hd + 1) * MLSTM_HD)
        h, state, m = _mlstm_chunk(
            q_ref[rows, cols], kt_ref[cols, rows], v_ref[rows, cols],
            bq_ref[rows, r:r + 1], mq_ref[rows, r:r + 1], ar_ref[r:r + 1, rows], st_ref[slot], m, lower)
        st_ref[slot] = state
        dst_ref[rows, cols] = h
        return m

    def body(s, ms):
        off_f = pl.multiple_of(s * CHUNK, CHUNK)
        off_b = pl.multiple_of(jnp.where(s == 0, 0, n_chunks - s) * CHUNK, CHUNK)
        out = []
        for hd in range(MLSTM_HEADS):
            out.append(run(off_f, hd, ms[2 * hd], True, hf_ref))
            out.append(run(off_b, hd, ms[2 * hd + 1], False, hb_ref))
        return tuple(out)

    lax.fori_loop(0, n_chunks, body, tuple(jnp.zeros((1, 1), F32) for _ in range(2 * MLSTM_HEADS)))

    for hd in range(MLSTM_HEADS):
        cols = slice(hd * MLSTM_HD, (hd + 1) * MLSTM_HD)
        hm = hf_ref[:, cols] + hb_ref[:, cols]
        ms = jnp.mean(hm * hm, axis=-1, keepdims=True)
        y = hm * lax.rsqrt(ms + EPS) * hn_ref[:, cols]
        o_ref[:, cols] = (y * sg_ref[:, cols].astype(F32)).astype(BF16)


def _mlstm(q, k_t, v, sg, gates_t, hnorm):
    b, tok, _ = q.shape
    full = pl.BlockSpec((None, tok, MIX_A), lambda i: (i, 0, 0))
    return pl.pallas_call(
        functools.partial(_mlstm_kernel, n_chunks=tok // CHUNK),
        grid=(b,),
        in_specs=[
            full, pl.BlockSpec((MIX_A, tok), lambda i: (0, i)), full, full,
            pl.BlockSpec((N_GATES, tok), lambda i: (0, i)),
            pl.BlockSpec((1, MIX_A), lambda i: (0, 0)),
        ],
        out_specs=full,
        out_shape=jax.ShapeDtypeStruct((b, tok, MIX_A), BF16),
        scratch_shapes=[
            pltpu.VMEM((tok, MIX_A), F32), pltpu.VMEM((tok, MIX_A), F32),
            pltpu.VMEM((tok, LANES), F32), pltpu.VMEM((tok, LANES), F32),
            pltpu.VMEM((2 * MLSTM_HEADS, tok), F32),
            pltpu.VMEM((2 * MLSTM_HEADS, MLSTM_HD, 2 * MLSTM_HD), F32),
        ],
        compiler_params=_cparams("parallel"),
        name="mlstm",
    )(q, k_t, v, sg, gates_t, hnorm)


def _short_conv_rows(x, w, b, ctx):
    tok = x.shape[0]
    t = lax.broadcasted_iota(jnp.int32, x.shape, 0)
    prev = jnp.where((t == 0) | (t == ctx), 0.0, pltpu.roll(x, 1, 0))
    nxt = jnp.where((t == ctx - 1) | (t == tok - 1), 0.0, pltpu.roll(x, tok - 1, 0))
    return b + prev * w[0:1, :] + x * w[1:2, :] + nxt * w[2:3, :]


def _filter_kernel(emb_ref, w1_ref, b1_ref, w2_ref, b2_ref, fr_ref, w3_ref, dec_ref, o_ref, hid_ref):
    @pl.when(pl.program_id(0) == 0)
    def _():
        fr = fr_ref[...]
        h = jnp.sin(fr * (_dot(emb_ref[...].astype(BF16), w1_ref[...]) + b1_ref[...]))
        hid_ref[...] = jnp.sin(fr * (_dot(h.astype(BF16), w2_ref[...]) + b2_ref[...])).astype(BF16)

    h = _dot(hid_ref[...], w3_ref[...]) * dec_ref[...]
    o_ref[...] = h * lax.rsqrt(jnp.sum(h * h, axis=0, keepdims=True) + EPS)


def _hyena_filters(length, w1p, b1p, w2p, b2p, frp, w3p):
    bands = (HYENA_EMB - 1) // 2
    t = np.linspace(0.0, 1.0, length)[:, None]
    wpos = 2.0 * math.pi * np.arange(length) / length
    fr = np.linspace(1e-4, bands - 1, bands)
    ang = wpos[:, None] * fr[None, :]
    emb = np.concatenate([t, np.cos(ang), -np.sin(ang)], axis=-1)
    emb = np.pad(emb, ((0, 0), (0, LANES - HYENA_EMB)))
    deltas = np.abs(np.linspace(math.log(HYENA_TARGET) / HYENA_SLOW, math.log(HYENA_TARGET) / HYENA_FAST, MIX_B))
    decay = np.exp(-t * deltas[None, :])
    ngrp = w3p.shape[1] // MIX_B
    const = lambda g: (0, 0)
    return pl.pallas_call(
        _filter_kernel,
        grid=(ngrp,),
        in_specs=[
            pl.BlockSpec((length, LANES), const), pl.BlockSpec((LANES, LANES), const),
            pl.BlockSpec((1, LANES), const), pl.BlockSpec((LANES, LANES), const),
            pl.BlockSpec((1, LANES), const), pl.BlockSpec((1, LANES), const),
            pl.BlockSpec((LANES, MIX_B), lambda g: (0, g)),
            pl.BlockSpec((length, MIX_B), const),
        ],
        out_specs=pl.BlockSpec((length, MIX_B), lambda g: (0, g)),
        out_shape=jax.ShapeDtypeStruct((length, ngrp * MIX_B), F32),
        scratch_shapes=[pltpu.VMEM((length, LANES), BF16)],
        compiler_params=_cparams("arbitrary"),
        name="hyena_filters",
    )(jnp.asarray(emb, F32), w1p, b1p, w2p, b2p, frp, w3p, jnp.asarray(decay, F32))


def _dft_matrices(length):
    period = 2 * length
    k = jnp.arange(length, dtype=jnp.int32)[:, None]
    t = jnp.arange(length, dtype=jnp.int32)[None, :]
    unit = 2.0 * math.pi / period

    def table(freqs):
        ang = ((freqs[:, None] * t) & (period - 1)).astype(F32) * unit
        return jnp.cos(ang), jnp.sin(ang)

    c1, s1 = table(jnp.arange(length // DFT_SPLIT, dtype=jnp.int32) * DFT_SPLIT)
    c0, s0 = table(jnp.arange(DFT_SPLIT, dtype=jnp.int32))
    c = (c1[:, None, :] * c0[None, :, :] - s1[:, None, :] * s0[None, :, :]).reshape(length, length)
    s = (s1[:, None, :] * c0[None, :, :] + c1[:, None, :] * s0[None, :, :]).reshape(length, length)
    alt_t = jnp.where(t % 2 == 0, 1.0, -1.0).astype(F32)
    fwd = jnp.concatenate([c, jnp.where(k == 0, alt_t, -s)], axis=0)
    coef = jnp.where(t == 0, 1.0, 2.0).astype(F32) / period
    alt_k = jnp.where(k % 2 == 0, 1.0, -1.0).astype(F32)
    inv = jnp.concatenate([c * coef, jnp.where(t == 0, alt_k / period, -s * coef)], axis=1)
    return fwd.astype(BF16), inv.astype(BF16)


def _spectrum_kernel(a_ref, hf_ref, hb_ref, o_ref):
    half = pl.program_id(0)
    hf = hf_ref[...]
    hb = hb_ref[...]
    comb = jnp.where(half == 0, hf + hb, hf - hb).astype(BF16)
    o_ref[...] = _dot(a_ref[...], comb)

    @pl.when(half == 1)
    def _():
        t = lax.broadcasted_iota(jnp.int32, hf.shape, 0)
        o_ref[0:1, :] = jnp.sum(jnp.where(t % 2 == 0, hf + hb, -(hf + hb)), axis=0, keepdims=True)


def _filter_spectrum(fwd, filt):
    period, length = fwd.shape
    orders = filt.shape[1] // (2 * MIX_B)
    return pl.pallas_call(
        _spectrum_kernel,
        grid=(2, orders),
        in_specs=[
            pl.BlockSpec((length, length), lambda hlf, o: (hlf, 0)),
            pl.BlockSpec((length, MIX_B), lambda hlf, o: (0, 2 * o)),
            pl.BlockSpec((length, MIX_B), lambda hlf, o: (0, 2 * o + 1)),
        ],
        out_specs=pl.BlockSpec((length, MIX_B), lambda hlf, o: (hlf, o)),
        out_shape=jax.ShapeDtypeStruct((period, orders * MIX_B), F32),
        compiler_params=_cparams("arbitrary", "arbitrary"),
        name="hyena_filter_spectrum",
    )(fwd, filt, filt)


HY_CH = 256
BFLY_ROWS = 128


def _spectral_product(a_ref, g_ref, z, length):
    zr = _dot(a_ref[0:length, :], z)
    zi = _dot(a_ref[length:2 * length, :], z)
    gr = g_ref[0:length, :]
    gi = g_ref[length:2 * length, :]
    first = lax.broadcasted_iota(jnp.int32, zr.shape, 0) == 0
    yr = jnp.where(first, zr * gr, zr * gr - zi * gi)
    yi = jnp.where(first, zi * gi, zr * gi + zi * gr)
    return jnp.concatenate([yr, yi], axis=0).astype(BF16)


def _radix2_conv(ze, zo, a_ref, b_ref, g_ref, tw_ref):
    n = ze.shape[0]
    e = _dot(a_ref[...], ze)
    o = _dot(a_ref[...], zo)
    rep = ze.shape[1] // LANES
    f0r, f0i, f1r, f1i = [], [], [], []
    for r0 in range(0, n, BFLY_ROWS):
        rows = slice(r0, r0 + BFLY_ROWS)
        rows_im = slice(n + r0, n + r0 + BFLY_ROWS)
        er, ei, orr, oi = e[rows, :], e[rows_im, :], o[rows, :], o[rows_im, :]
        c = jnp.concatenate([tw_ref[rows, :]] * rep, axis=1)
        s = jnp.concatenate([tw_ref[rows_im, :]] * rep, axis=1)
        gar, gai, gbr, gbi = (g_ref[j * n + r0:j * n + r0 + BFLY_ROWS, :] for j in range(4))
        tr = c * orr + s * oi
        ti = c * oi - s * orr
        zar = er + tr
        zbr = er - tr
        zai = ei + ti
        zbi = ti - ei
        if r0 == 0:
            first = lax.broadcasted_iota(jnp.int32, er.shape, 0) == 0
            zai = jnp.where(first, zbr, zai)
            zbi = jnp.where(first, -oi, zbi)
            zbr = jnp.where(first, ei, zbr)
        yar = zar * gar - zai * gai
        yai = zar * gai + zai * gar
        if r0 == 0:
            yar = jnp.where(first, zar * gar, yar)
            yai = jnp.where(first, zai * gai, yai)
        ybr = zbr * gbr - zbi * gbi
        ybi = zbr * gbi + zbi * gbr
        dr = yar - ybr
        di = yai + ybi
        p0r = 0.5 * (yar + ybr)
        p0i = 0.5 * (yai - ybi)
        p1r = 0.5 * (c * dr - s * di)
        p1i = 0.5 * (c * di + s * dr)
        if r0 == 0:
            p0r = jnp.where(first, 0.5 * (yar + yai), p0r)
            p0i = jnp.where(first, ybr, p0i)
            p1r = jnp.where(first, 0.5 * (yar - yai), p1r)
            p1i = jnp.where(first, -ybi, p1i)
        f0r.append(p0r.astype(BF16))
        f0i.append(p0i.astype(BF16))
        f1r.append(p1r.astype(BF16))
        f1i.append(p1i.astype(BF16))
    y_even = _dot(b_ref[...], jnp.concatenate(f0r + f0i, axis=0))
    y_odd = _dot(b_ref[...], jnp.concatenate(f1r + f1i, axis=0))
    return y_even, y_odd


def _hyena_order_kernel(*refs, ctx, seq, conv_z):
    if conv_z:
        (z_ref, zw_ref, zb_ref, x_ref, xw_ref, xb_ref, d_ref, ah_ref, bh_ref, ac_ref, bc_ref, gl_ref, gc_ref,
         tw_ref, o_ref, zs_ref, gs_ref, os_ref) = refs
        z = _short_conv_rows(z_ref[...], zw_ref[...], zb_ref[...], ctx)
    else:
        (z_ref, x_ref, xw_ref, xb_ref, d_ref, ah_ref, bh_ref, ac_ref, bc_ref, gl_ref, gc_ref,
         tw_ref, o_ref, zs_ref, gs_ref, os_ref) = refs
        z = z_ref[...].astype(F32)
    gate = _short_conv_rows(x_ref[...], xw_ref[...], xb_ref[...], ctx)
    d = d_ref[...]
    half = seq // 2
    n_blk = zs_ref.shape[0]

    zc = z[0:ctx, :]
    yc = _dot(bc_ref[...], _spectral_product(ac_ref, gc_ref, zc.astype(BF16), ctx))
    o_ref[0:ctx, :] = (gate[0:ctx, :] * (yc + d * zc)).astype(o_ref.dtype)

    for j in range(n_blk):
        zs_ref[j] = z[ctx:, j * LANES:(j + 1) * LANES]
        gs_ref[j] = gate[ctx:, j * LANES:(j + 1) * LANES]

    def samples(ref, parity):
        return jnp.concatenate([ref[j, pl.ds(parity, half, stride=2), :] for j in range(n_blk)], axis=1)

    ze = samples(zs_ref, 0)
    zo = samples(zs_ref, 1)
    y_even, y_odd = _radix2_conv(ze.astype(BF16), zo.astype(BF16), ah_ref, bh_ref, gl_ref, tw_ref)
    for parity, y, zp in ((0, y_even, ze), (1, y_odd, zo)):
        out = samples(gs_ref, parity) * (y + d * zp)
        for j in range(n_blk):
            os_ref[j, pl.ds(parity, half, stride=2), :] = out[:, j * LANES:(j + 1) * LANES]
    for j in range(n_blk):
        o_ref[ctx:, j * LANES:(j + 1) * LANES] = os_ref[j].astype(o_ref.dtype)


def _conv_specs(blk):
    n_ch = MIX_B // HY_CH
    return [pl.BlockSpec((3, HY_CH), lambda j, i: (0, blk * n_ch + j)),
            pl.BlockSpec((1, HY_CH), lambda j, i: (0, blk * n_ch + j))]


def _hyena_order(z, conv_z, hy, xblk, conv_x, dskip, order, mats, g_lat, g_ctx, twiddle, ctx, seq, out_dtype):
    a_half, b_half, a_ctx, b_ctx = mats
    b, tok, _ = hy.shape
    n_ch = MIX_B // HY_CH
    has_z = conv_z is not None
    const = lambda j, i: (0, 0)
    return pl.pallas_call(
        functools.partial(_hyena_order_kernel, ctx=ctx, seq=seq, conv_z=has_z),
        grid=(n_ch, b),
        in_specs=[pl.BlockSpec((None, tok, HY_CH), lambda j, i: (i, 0, j))]
        + (_conv_specs(0) if has_z else [])
        + [pl.BlockSpec((None, tok, HY_CH), lambda j, i: (i, 0, xblk * n_ch + j))]
        + _conv_specs(xblk)
        + [
            pl.BlockSpec((None, 1, HY_CH), lambda j, i: (order, 0, j)),
            _resident(a_half.shape, const), _resident(b_half.shape, const),
            _resident(a_ctx.shape, const), _resident(b_ctx.shape, const),
            pl.BlockSpec((2 * seq, HY_CH), lambda j, i: (0, order * n_ch + j)),
            pl.BlockSpec((2 * ctx, HY_CH), lambda j, i: (0, order * n_ch + j)),
            _resident(twiddle.shape, const),
        ],
        out_specs=pl.BlockSpec((None, tok, HY_CH), lambda j, i: (i, 0, j)),
        out_shape=jax.ShapeDtypeStruct((b, tok, MIX_B), out_dtype),
        scratch_shapes=[pltpu.VMEM((HY_CH // LANES, seq, LANES), F32)] * 3,
        compiler_params=_cparams("parallel", "parallel"),
        name="hyena_order",
    )(z, *(conv_z if has_z else ()), hy, *conv_x, dskip, a_half, b_half, a_ctx, b_ctx, g_lat, g_ctx, twiddle)


def _spectrum_bfly_kernel(a_ref, hf_ref, hb_ref, o_ref):
    q = pl.program_id(0)
    hf = hf_ref[...]
    hb = hb_ref[...]
    t = lax.broadcasted_iota(jnp.int32, hf.shape, 0)
    alt = jnp.where(t % 2 == 0, 1.0, -1.0)
    comb = jnp.where(q % 2 == 0, hf + hb, hf - hb) * jnp.where(q < 2, 1.0, alt)
    res = _dot(a_ref[...], comb.astype(BF16))
    o_ref[...] = jnp.where(q == 3, -res, res)

    quarter = t % 4

    @pl.when(q == 1)
    def _():
        o_ref[0:1, :] = jnp.sum(alt * (hf + hb), axis=0, keepdims=True)

    @pl.when(q == 2)
    def _():
        w = jnp.where(quarter == 0, 1.0, jnp.where(quarter == 2, -1.0, 0.0))
        o_ref[0:1, :] = jnp.sum(w * (hf + hb), axis=0, keepdims=True)

    @pl.when(q == 3)
    def _():
        w = jnp.where(quarter == 1, -1.0, jnp.where(quarter == 3, 1.0, 0.0))
        o_ref[0:1, :] = jnp.sum(w * (hf - hb), axis=0, keepdims=True)


def _filter_spectrum_bfly(fwd, filt):
    period, length = fwd.shape
    n = length // 2
    orders = filt.shape[1] // (2 * MIX_B)
    return pl.pallas_call(
        _spectrum_bfly_kernel,
        grid=(4, orders),
        in_specs=[
            pl.BlockSpec((n, length), lambda q, o: ((q % 2) * (length // n), 0)),
            pl.BlockSpec((length, MIX_B), lambda q, o: (0, 2 * o)),
            pl.BlockSpec((length, MIX_B), lambda q, o: (0, 2 * o + 1)),
        ],
        out_specs=pl.BlockSpec((n, MIX_B), lambda q, o: (q, o)),
        out_shape=jax.ShapeDtypeStruct((period, orders * MIX_B), F32),
        compiler_params=_cparams("arbitrary", "arbitrary"),
        name="hyena_filter_spectrum_bfly",
    )(fwd, filt, filt)


def _twiddles(length):
    n = length // 2
    ang = jnp.arange(n, dtype=F32) * (2.0 * math.pi / (2 * length))
    tw = jnp.concatenate([jnp.cos(ang), jnp.sin(ang)])[:, None]
    return jnp.broadcast_to(tw, (2 * n, LANES))


def _odd_in_kernel(x_ref, shc_ref, shb_ref, scc_ref, scb_ref, w_ref, qn_ref, kn_ref, q_ref, k_ref, v_ref,
                   *, n_sub, per_batch):
    lane = lax.broadcasted_iota(jnp.int32, (SUB_ROWS, LANES), 1)
    lo = lane < NA_HD
    half = D_MODEL // 2
    for sb in range(n_sub):
        rows = _sub_rows(sb)
        h = _normmod(x_ref[rows, :], _sub_mod(shc_ref, shb_ref, sb, per_batch),
                     _sub_mod(scc_ref, scb_ref, sb, per_batch)).astype(BF16)
        for part, gain_ref, scale, dst in ((0, qn_ref, NA_HD ** -0.5 * LOG2E, q_ref), (1, kn_ref, 1.0, k_ref)):
            gain = gain_ref[...]
            for c in range(2):
                base = part * D_MODEL + c * half
                acc = _dot(h, w_ref[:, base:base + half])
                for g in range(half // LANES):
                    a = acc[:, g * LANES:(g + 1) * LANES]
                    sq = a * a
                    s_lo = jnp.sum(jnp.where(lo, sq, 0.0), axis=-1, keepdims=True)
                    s_hi = jnp.sum(jnp.where(lo, 0.0, sq), axis=-1, keepdims=True)
                    inv = lax.rsqrt(jnp.where(lo, s_lo, s_hi) * (1.0 / NA_HD) + EPS)
                    y = a * inv * gain
                    if scale != 1.0:
                        y = y * scale
                    dst[rows, c * half + g * LANES:c * half + (g + 1) * LANES] = y.astype(BF16)
        for c in range(2):
            base = 2 * D_MODEL + c * half
            v_ref[rows, c * half:(c + 1) * half] = _dot(h, w_ref[:, base:base + half]).astype(BF16)


def _odd_in_proj(x2, mod5, layer, tiles, w_qkv, qn2, kn2):
    m = x2.shape[0]
    out = jax.ShapeDtypeStruct((m, D_MODEL), BF16)
    return pl.pallas_call(
        functools.partial(_odd_in_kernel, n_sub=tiles.n_sub, per_batch=tiles.per_batch),
        grid=(tiles.n,),
        in_specs=[tiles.row_spec(D_MODEL)] + tiles.mod_specs(layer, 0) + tiles.mod_specs(layer, 1) + [
            _resident((D_MODEL, 3 * D_MODEL), lambda i: (0, 0)),
            pl.BlockSpec((1, LANES), lambda i: (0, 0)), pl.BlockSpec((1, LANES), lambda i: (0, 0)),
        ],
        out_specs=[tiles.row_spec(D_MODEL)] * 3,
        out_shape=(out, out, out),
        compiler_params=_cparams("parallel"),
        name="odd_qkv_proj",
    )(x2, mod5, mod5, mod5, mod5, w_qkv, qn2, kn2)


def _pair_queries(q):
    lane = lax.broadcasted_iota(jnp.int32, q.shape, 1)
    zero = jnp.zeros_like(q)
    return jnp.concatenate([jnp.where(lane < NA_HD, q, zero), jnp.where(lane < NA_HD, zero, q)], axis=0)


def _unpair(res, n):
    lane = lax.broadcasted_iota(jnp.int32, (n, LANES), 1)
    return jnp.where(lane < NA_HD, res[0:n, :], res[n:2 * n, :])


def _natten_kernel(q_ref, k_ref, v_ref, t_ref, o_ref, va_ref, *, ctx, rows_n, win_rows):
    tok = v_ref.shape[0]
    va_ref[:, 0:LANES] = v_ref[...]
    va_ref[:, LANES:2 * LANES] = jnp.ones((tok, LANES), BF16)
    kc = k_ref[0:ctx, :]
    vc = va_ref[0:ctx, :]

    def softmax_pv(scores, values):
        m = functools.reduce(jnp.maximum, [jnp.max(s, axis=1, keepdims=True) for s in scores])
        res = None
        for s, vblk in zip(scores, values):
            part = _dot(jnp.exp2(s - m).astype(BF16), vblk)
            res = part if res is None else res + part
        return res[:, 0:LANES] * (1.0 / res[:, LANES:2 * LANES])

    w = _pair_queries(q_ref[0:ctx, :])
    o_ref[0:ctx, :] = _unpair(softmax_pv([_dot_nt(w, kc)], [vc]), ctx).astype(BF16)

    nk = win_rows * GRID_W

    def body(r, carry):
        rs = jnp.clip(r - win_rows // 2, 0, rows_n - win_rows)
        dr0 = rs - r + NA_WIN_ROWS - 1
        q_off = pl.multiple_of(ctx + r * GRID_W, GRID_W)
        k_off = pl.multiple_of(ctx + rs * GRID_W, GRID_W)
        w = _pair_queries(q_ref[pl.ds(q_off, GRID_W), :])
        bias = t_ref[dr0 & 1, :, pl.ds(pl.multiple_of((dr0 >> 1) * LANES, LANES), nk)]
        s_lat = _dot_nt(w, k_ref[pl.ds(k_off, nk), :]) + bias
        s_ctx = _dot_nt(w, kc)
        out = softmax_pv([s_lat, s_ctx], [va_ref[pl.ds(k_off, nk), :], vc])
        o_ref[pl.ds(q_off, GRID_W), :] = _unpair(out, GRID_W).astype(BF16)
        return carry

    lax.fori_loop(0, rows_n, body, 0, unroll=min(NA_UNROLL, rows_n))


def _natten(q, k, v, table, ctx, seq):
    b, tok, _ = q.shape
    rows_n = seq // GRID_W
    assert rows_n >= NA_WIN_ROWS and rows_n % min(NA_UNROLL, rows_n) == 0
    pairs = NA_HEADS // 2
    pair_spec = pl.BlockSpec((None, tok, LANES), lambda p, i: (i, 0, p))
    return pl.pallas_call(
        functools.partial(_natten_kernel, ctx=ctx, rows_n=rows_n, win_rows=NA_WIN_ROWS),
        grid=(pairs, b),
        in_specs=[
            pair_spec, pair_spec, pair_spec,
            pl.BlockSpec((None,) + table.shape[1:], lambda p, i: (p, 0, 0, 0)),
        ],
        out_specs=pair_spec,
        out_shape=jax.ShapeDtypeStruct((b, tok, D_MODEL), BF16),
        scratch_shapes=[pltpu.VMEM((tok, 2 * LANES), BF16)],
        compiler_params=_cparams("parallel", "parallel"),
        name="natten",
    )(q, k, v, table)


def _natten_bias_table(rpb):
    n_dr = 2 * NA_WIN_ROWS - 1
    n_dc = 2 * NA_WIN_COLS - 1
    cidx = np.arange(GRID_W)
    cstart = np.clip(cidx - NA_WIN_COLS // 2, 0, GRID_W - NA_WIN_COLS)
    cmask = (cidx[None, :] >= cstart[:, None]) & (cidx[None, :] < cstart[:, None] + NA_WIN_COLS)
    dc = np.clip(cidx[None, :] - cidx[:, None] + NA_WIN_COLS - 1, 0, n_dc - 1)
    onehot = ((dc[None] == np.arange(n_dc)[:, None, None]) & cmask[None]).astype(np.float32)
    tb = jnp.einsum("hrc,cqk->hrqk", rpb.astype(F32), jnp.asarray(onehot), precision=lax.Precision.HIGHEST)
    tb = tb * LOG2E + jnp.asarray(np.where(cmask, 0.0, NEG_BIG), F32)
    tb = tb.reshape(NA_HEADS // 2, 2, n_dr, GRID_W, GRID_W).transpose(0, 1, 3, 2, 4)
    tb = tb.reshape(NA_HEADS // 2, 2 * GRID_W, n_dr * GRID_W)
    width = (n_dr + 1) * GRID_W
    tb = jnp.pad(tb, ((0, 0), (0, 0), (0, width + GRID_W - n_dr * GRID_W)))
    return jnp.stack([tb[:, :, 0:width], tb[:, :, GRID_W:GRID_W + width]], axis=1)


FF_CHUNK = 1024


def _mix_mlp_kernel(*refs, widths, n_sub, per_batch, latent_only):
    x_ref = refs[0]
    g1, sh, sc, g2 = (refs[1 + 2 * j:3 + 2 * j] for j in range(4))
    in_refs = refs[9:9 + len(widths)]
    wo_ref, w1_ref, w2_ref, o_ref = refs[9 + len(widths):]

    def compute(sb):
        rows = _sub_rows(sb)
        mix = None
        off = 0
        for r, wd in zip(in_refs, widths):
            part = _dot(r[rows, :].astype(BF16), wo_ref[off:off + wd, :])
            mix = part if mix is None else mix + part
            off += wd
        x = x_ref[rows, :] + _sub_mod(*g1, sb, per_batch) * mix
        h = _normmod(x, _sub_mod(*sh, sb, per_batch), _sub_mod(*sc, sb, per_batch)).astype(BF16)
        acc = None
        for c0 in range(0, D_FF, FF_CHUNK):
            a = jnp.maximum(_dot(h, w1_ref[:, c0:c0 + FF_CHUNK]), 0.0)
            part = _dot((a * a).astype(BF16), w2_ref[c0:c0 + FF_CHUNK, :])
            acc = part if acc is None else acc + part
        o_ref[rows, :] = x + _sub_mod(*g2, sb, per_batch) * acc

    if latent_only:
        assert n_sub == 1
        pl.when(pl.program_id(0) % per_batch != 0)(functools.partial(compute, 0))
    else:
        for sb in range(n_sub):
            compute(sb)


def _mix_mlp(x2, mod5, layer, tiles, parts, w_out, w1, w2, latent_only):
    widths = tuple(p.shape[1] for p in parts)
    pb = tiles.per_batch
    if latent_only:
        out_rows = tiles.batch * (tiles.tok - tiles.ctx)
        out_spec = pl.BlockSpec((tiles.rows, D_MODEL), lambda i: (i - i // pb - jnp.where(i % pb == 0, 0, 1), 0))
    else:
        out_rows = x2.shape[0]
        out_spec = tiles.row_spec(D_MODEL)
    mods = [s for which in (2, 3, 4, 5) for s in tiles.mod_specs(layer, which)]
    return pl.pallas_call(
        functools.partial(_mix_mlp_kernel, widths=widths, n_sub=tiles.n_sub, per_batch=pb, latent_only=latent_only),
        grid=(tiles.n,),
        in_specs=[tiles.row_spec(D_MODEL)] + mods
        + [tiles.row_spec(wd) for wd in widths]
        + [_resident(w_out.shape, lambda i: (0, 0)), _resident(w1.shape, lambda i: (0, 0)),
           _resident(w2.shape, lambda i: (0, 0))],
        out_specs=out_spec,
        out_shape=jax.ShapeDtypeStruct((out_rows, D_MODEL), F32),
        input_output_aliases={} if latent_only else {0: 0},
        compiler_params=_cparams("arbitrary" if latent_only else "parallel"),
        name="mix_out_mlp",
    )(x2, *([mod5] * len(mods)), *parts, w_out, w1, w2)


def _pad_cols(a, width):
    return jnp.pad(a, ((0, 0), (0, width - a.shape[1])))


def kernel(x, c, ctx, c_ctx, w_mod, b_mod, w_mlp_in, w_mlp_out, e_w_in, e_gate_b, e_hnorm, e_conv_w, e_conv_b,
           e_f_w1, e_f_b1, e_f_w2, e_f_b2, e_f_w3, e_f_freq, e_hy_d, e_w_out, o_w_qkv, o_qn, o_kn, o_rpb, o_w_out):
    batch, seq, d = x.shape
    n_ctx = ctx.shape[1]
    depth = w_mod.shape[0]
    tok = n_ctx + seq
    assert d == D_MODEL and seq % CHUNK == 0 and n_ctx == CHUNK and seq % GRID_W == 0
    tiles = _Tiles(batch, tok, n_ctx, TILE_ROWS if tok % TILE_ROWS == 0 else SUB_ROWS)
    tiles_last = _Tiles(batch, tok, n_ctx, SUB_ROWS)

    n_samp = -(-(batch + 1) // 8) * 8
    cvec = jnp.concatenate([c, c_ctx[None, :], jnp.zeros((n_samp - batch - 1, d), F32)], axis=0)
    mod5 = _modulation(cvec, w_mod, b_mod)

    cos_t, sin_t = _rope_tables(n_ctx, seq)
    a_lat, _ = _dft_matrices(seq)
    a_ctx, b_ctx = _dft_matrices(n_ctx)
    dft_mats = _dft_matrices(seq // 2) + (a_ctx, b_ctx)
    twiddle = _twiddles(seq)

    xs = (ctx.reshape(batch * n_ctx, d), x.reshape(batch * seq, d))

    for l in range(depth):
        i = l // 2
        if l % 2 == 0:
            w_in = e_w_in[i]
            g0 = 4 * MIX_A
            w_pad = jnp.concatenate(
                [w_in[:, :g0], _pad_cols(w_in[:, g0:g0 + N_GATES], LANES), w_in[:, g0 + N_GATES:]], axis=1).astype(BF16)
            gate_b = _pad_cols(e_gate_b[i][None, :], LANES)
            q, k_t, v, sg, g_t, hy, *joint = _even_in_proj(xs, mod5, l, tiles, w_pad, gate_b, cos_t, sin_t)
            if joint:
                xs = joint[0]
            as3 = lambda a: a.reshape(batch, tok, a.shape[-1])
            a_out = _mlstm(as3(q), k_t, as3(v), as3(sg), g_t, e_hnorm[i][None, :])

            hy3 = as3(hy)
            conv = (e_conv_w[i], e_conv_b[i][None, :])
            pad2 = lambda a: jnp.pad(a, ((0, LANES - a.shape[0]), (0, LANES - a.shape[1])))
            w1p = pad2(e_f_w1[i]).astype(BF16)
            w2p = pad2(e_f_w2[i]).astype(BF16)
            w3p = jnp.pad(e_f_w3[i], ((0, LANES - HYENA_FFN), (0, 0))).astype(BF16)
            b1p = _pad_cols(e_f_b1[i][None, :], LANES)
            b2p = _pad_cols(e_f_b2[i][None, :], LANES)
            frp = _pad_cols(e_f_freq[i][None, :], LANES)
            g_lat = _filter_spectrum_bfly(a_lat, _hyena_filters(seq, w1p, b1p, w2p, b2p, frp, w3p))
            g_ctx = _filter_spectrum(a_ctx, _hyena_filters(n_ctx, w1p, b1p, w2p, b2p, frp, w3p))
            dskip = e_hy_d[i][:, None, :]
            n_ord = e_hy_d.shape[1]
            z, conv_z = hy3, conv
            for o in range(n_ord):
                z = _hyena_order(z, conv_z, hy3, 1 + o, conv, dskip, o, dft_mats, g_lat, g_ctx, twiddle,
                                 n_ctx, seq, F32)
                conv_z = None
            parts = [a_out.reshape(batch * tok, MIX_A), z.reshape(batch * tok, MIX_B)]
            w_out = e_w_out[i].astype(BF16)
        else:
            rep = LANES // NA_HD
            qn2 = jnp.tile(o_qn[i], rep)[None, :]
            kn2 = jnp.tile(o_kn[i], rep)[None, :]
            q, k, v = _odd_in_proj(xs, mod5, l, tiles, o_w_qkv[i].astype(BF16), qn2, kn2)
            as3 = lambda a: a.reshape(batch, tok, D_MODEL)
            table = _natten_bias_table(o_rpb[i])
            att = _natten(as3(q), as3(k), as3(v), table, n_ctx, seq)
            parts = [att.reshape(batch * tok, D_MODEL)]
            w_out = o_w_out[i].astype(BF16)
        last = l == depth - 1
        xs = _mix_mlp(xs, mod5, l, tiles_last if last else tiles, parts, w_out,
                      w_mlp_in[l].astype(BF16), w_mlp_out[l].astype(BF16), latent_only=last)

    return xs.reshape(batch, seq, d)
```

```python
import functools
import math

import numpy as np
import jax
import jax.numpy as jnp
from jax import lax
from jax.experimental import pallas as pl
from jax.experimental.pallas import tpu as pltpu

F32 = jnp.float32
BF16 = jnp.bfloat16

D_MODEL = 1024
D_FF = 4 * D_MODEL
EPS = 1e-6
ROPE_BASE = 10000.0
GRID_W = 64
MIX_A = D_MODEL // 2
MIX_B = D_MODEL - MIX_A
MLSTM_HEADS = 4
MLSTM_HD = MIX_A // MLSTM_HEADS
N_GATES = 4 * MLSTM_HEADS
HYENA_EMB = 33
HYENA_FFN = 64
HYENA_TARGET = 1e-2
HYENA_FAST = 0.3
HYENA_SLOW = 1.5
NA_HEADS = 16
NA_HD = D_MODEL // NA_HEADS
NA_WIN_ROWS = 8
NA_WIN_COLS = 16

LANES = 128
SUB_ROWS = 256
TILE_ROWS = 768
CHUNK = 256
VMEM_LIMIT = 56 * 1024 * 1024
NEG_BIG = -1e30
LOG2E = math.log2(math.e)
DFT_SPLIT = 64
NA_UNROLL = 32


def _cparams(*sem):
    return pltpu.CompilerParams(dimension_semantics=sem, vmem_limit_bytes=VMEM_LIMIT)


def _dot(a, b):
    return jnp.dot(a, b, preferred_element_type=F32)


def _dot_nt(a, b):
    return lax.dot_general(a, b, (((1,), (1,)), ((), ())), preferred_element_type=F32)


def _dot_tn(a, b):
    return lax.dot_general(a, b, (((0,), (0,)), ((), ())), preferred_element_type=F32)


def _resident(shape, index_map):
    return pl.BlockSpec(shape, index_map, pipeline_mode=pl.Buffered(1))


def _normmod(x, sh, sc):
    ms = jnp.mean(x * x, axis=-1, keepdims=True)
    return (x * lax.rsqrt(ms + EPS)) * (1.0 + sc) + sh


def _sigmoid(x):
    return 1.0 / (1.0 + jnp.exp(-x))


def _log_sigmoid(x):
    return jnp.minimum(x, 0.0) - jnp.log(1.0 + jnp.exp(-jnp.abs(x)))


def _mod_kernel(c_ref, w_ref, b_ref, o_ref):
    c = c_ref[...]
    s = (c * _sigmoid(c)).astype(BF16)
    o_ref[...] = _dot(s, w_ref[...].astype(BF16)) + b_ref[...]


def _modulation(cvec, w_mod, b_mod):
    depth, d, d6 = w_mod.shape
    ns = cvec.shape[0]
    nj = d6 // d
    out = pl.pallas_call(
        _mod_kernel,
        grid=(depth, nj),
        in_specs=[
            pl.BlockSpec((ns, d), lambda l, j: (0, 0)),
            pl.BlockSpec((None, d, d), lambda l, j: (l, 0, j)),
            pl.BlockSpec((None, 1, d), lambda l, j: (l, 0, j)),
        ],
        out_specs=pl.BlockSpec((None, ns, d), lambda l, j: (l, 0, j)),
        out_shape=jax.ShapeDtypeStruct((depth, ns, d6), F32),
        compiler_params=_cparams("arbitrary", "arbitrary"),
        name="modulation",
    )(cvec, w_mod, b_mod.reshape(depth, 1, d6))
    return out.reshape(depth, ns, nj, 1, d)


class _Tiles:
    def __init__(self, batch, tok, ctx, rows):
        assert ctx == SUB_ROWS and rows % SUB_ROWS == 0 and tok % rows == 0
        self.batch, self.tok, self.ctx, self.rows = batch, tok, ctx, rows
        self.n_sub = rows // SUB_ROWS
        self.per_batch = tok // rows
        self.n = batch * self.per_batch

    def mod_specs(self, layer, which):
        blk = (None, None, None, 1, D_MODEL)
        return [pl.BlockSpec(blk, lambda i: (layer, self.batch, which, 0, 0)),
                pl.BlockSpec(blk, lambda i: (layer, i // self.per_batch, which, 0, 0))]

    def row_spec(self, width):
        return pl.BlockSpec((self.rows, width), lambda i: (i, 0))

    def pos_spec(self, width):
        return pl.BlockSpec((self.rows, width), lambda i: (i % self.per_batch, 0))


def _sub_mod(c_ref, b_ref, sb, per_batch):
    if sb > 0:
        return b_ref[...]
    return jnp.where(pl.program_id(0) % per_batch == 0, c_ref[...], b_ref[...])


def _sub_rows(sb):
    return slice(sb * SUB_ROWS, (sb + 1) * SUB_ROWS)


def _even_in_kernel(*refs, n_sub, per_batch, split_in):
    if split_in:
        ctx_ref, lat_refs, refs = refs[0], refs[1:1 + n_sub], refs[1 + n_sub:]
    else:
        x_ref, refs = refs[0], refs[1:]
    (shc_ref, shb_ref, scc_ref, scb_ref, w_ref, gb_ref, cos_ref, sin_ref,
     q_ref, kt_ref, v_ref, sg_ref, gt_ref, hy_ref) = refs[:14]
    lane = lax.broadcasted_iota(jnp.int32, (SUB_ROWS, MLSTM_HD), 1)
    first = (lane % (MLSTM_HD // 2)) < (MLSTM_HD // 4)
    g0 = 4 * MIX_A
    h0 = g0 + LANES
    for sb in range(n_sub):
        rows = _sub_rows(sb)
        if split_in:
            x = lat_refs[sb][...]
            if sb == 0:
                x = jnp.where(pl.program_id(0) % per_batch == 0, ctx_ref[...], x)
            refs[14][rows, :] = x
        else:
            x = x_ref[rows, :]
        h = _normmod(x, _sub_mod(shc_ref, shb_ref, sb, per_batch),
                     _sub_mod(scc_ref, scb_ref, sb, per_batch)).astype(BF16)
        cos = cos_ref[rows, :]
        sin = sin_ref[rows, :]

        def rope(a):
            part = jnp.where(first, pltpu.roll(a, LANES - MLSTM_HD // 4, 1), pltpu.roll(a, MLSTM_HD // 4, 1))
            return a * cos + part * sin

        acc = _dot(h, w_ref[:, 0:MIX_A])
        for hd in range(MLSTM_HEADS):
            sl = slice(hd * MLSTM_HD, (hd + 1) * MLSTM_HD)
            q_ref[rows, sl] = rope(acc[:, sl]).astype(BF16)
        acc = _dot(h, w_ref[:, MIX_A:2 * MIX_A]) * (MLSTM_HD ** -0.5)
        for hd in range(MLSTM_HEADS):
            sl = slice(hd * MLSTM_HD, (hd + 1) * MLSTM_HD)
            kt_ref[sl, rows] = rope(acc[:, sl]).T.astype(BF16)
        v_ref[rows, :] = _dot(h, w_ref[:, 2 * MIX_A:3 * MIX_A]).astype(BF16)
        sg_ref[rows, :] = _sigmoid(_dot(h, w_ref[:, 3 * MIX_A:4 * MIX_A])).astype(BF16)
        gt_ref[:, rows] = (_dot(h, w_ref[:, g0:g0 + LANES]) + gb_ref[...]).T
        for j in range(3):
            hy_ref[rows, j * MIX_B:(j + 1) * MIX_B] = _dot(h, w_ref[:, h0 + j * MIX_B:h0 + (j + 1) * MIX_B])


def _even_in_proj(x2, mod5, layer, tiles, w_pad, gate_b_pad, cos_t, sin_t):
    split_in = isinstance(x2, tuple)
    m = tiles.batch * tiles.tok
    n_w = w_pad.shape[1]
    outs = (
        jax.ShapeDtypeStruct((m, MIX_A), BF16), jax.ShapeDtypeStruct((MIX_A, m), BF16),
        jax.ShapeDtypeStruct((m, MIX_A), BF16), jax.ShapeDtypeStruct((m, MIX_A), BF16),
        jax.ShapeDtypeStruct((LANES, m), F32), jax.ShapeDtypeStruct((m, 3 * MIX_B), F32),
    )
    out_specs = [tiles.row_spec(MIX_A), pl.BlockSpec((MIX_A, tiles.rows), lambda i: (0, i)),
                 tiles.row_spec(MIX_A), tiles.row_spec(MIX_A),
                 pl.BlockSpec((LANES, tiles.rows), lambda i: (0, i)), tiles.row_spec(3 * MIX_B)]
    if split_in:
        pb, ns = tiles.per_batch, tiles.n_sub
        lat_blocks = (tiles.tok - tiles.ctx) // SUB_ROWS

        def lat_spec(sb):
            return pl.BlockSpec((SUB_ROWS, D_MODEL), lambda i: (
                (i // pb) * lat_blocks + jnp.maximum((i % pb) * ns + sb - 1, 0), 0))

        x_specs = [pl.BlockSpec((SUB_ROWS, D_MODEL), lambda i: (i // pb, 0))] + [lat_spec(sb) for sb in range(ns)]
        x_args = (x2[0],) + (x2[1],) * ns
        outs = outs + (jax.ShapeDtypeStruct((m, D_MODEL), F32),)
        out_specs = out_specs + [tiles.row_spec(D_MODEL)]
    else:
        x_specs = [tiles.row_spec(D_MODEL)]
        x_args = (x2,)
    return pl.pallas_call(
        functools.partial(_even_in_kernel, n_sub=tiles.n_sub, per_batch=tiles.per_batch, split_in=split_in),
        grid=(tiles.n,),
        in_specs=x_specs + tiles.mod_specs(layer, 0) + tiles.mod_specs(layer, 1) + [
            _resident((D_MODEL, n_w), lambda i: (0, 0)),
            pl.BlockSpec((1, LANES), lambda i: (0, 0)),
            tiles.pos_spec(MLSTM_HD), tiles.pos_spec(MLSTM_HD),
        ],
        out_specs=out_specs,
        out_shape=outs,
        compiler_params=_cparams("parallel"),
        name="even_in_proj",
    )(*x_args, mod5, mod5, mod5, mod5, w_pad, gate_b_pad, cos_t, sin_t)


def _rope_tables(ctx, seq):
    half = MLSTM_HD // 2
    nf = half // 2
    inv = ROPE_BASE ** (-np.arange(nf, dtype=np.float64) / nf)
    t = np.arange(seq)
    rows, cols = t // GRID_W, t % GRID_W

    def one(pos):
        ang = pos[:, None].astype(np.float64) * inv[None, :]
        c = np.concatenate([np.cos(ang), np.cos(ang)], axis=-1)
        s = np.concatenate([-np.sin(ang), np.sin(ang)], axis=-1)
        return c, s

    cr, sr = one(rows)
    cc, sc = one(cols)
    cos = np.concatenate([cr, cc], axis=-1)
    sin = np.concatenate([sr, sc], axis=-1)
    cos = np.concatenate([np.ones((ctx, MLSTM_HD)), cos], axis=0)
    sin = np.concatenate([np.zeros((ctx, MLSTM_HD)), sin], axis=0)
    return jnp.asarray(cos, F32), jnp.asarray(sin, F32)


def _seg_scan(y, pos, op, reverse, axis):
    n = y.shape[axis]
    k = 1
    while k < CHUNK:
        if reverse:
            y = jnp.where(pos < CHUNK - k, op(y, pltpu.roll(y, n - k, axis)), y)
        else:
            y = jnp.where(pos >= k, op(y, pltpu.roll(y, k, axis)), y)
        k *= 2
    return y


def _mlstm_gate_scans(gt_ref, qc_refs, ar_ref):
    hh = MLSTM_HEADS
    g = gt_ref[...]
    gi = jnp.concatenate([g[0:hh], g[2 * hh:3 * hh]], axis=0) * LOG2E
    lf = _log_sigmoid(jnp.concatenate([g[hh:2 * hh], g[3 * hh:4 * hh]], axis=0)) * LOG2E
    fwd = lax.broadcasted_iota(jnp.int32, lf.shape, 0) < hh
    pos = lax.broadcasted_iota(jnp.int32, lf.shape, 1) % CHUNK
    b = jnp.where(fwd, _seg_scan(lf, pos, jnp.add, False, 1), _seg_scan(lf, pos, jnp.add, True, 1))
    a = gi - b
    amax = jnp.where(fwd, _seg_scan(a, pos, jnp.maximum, False, 1), _seg_scan(a, pos, jnp.maximum, True, 1))
    ar_ref[...] = a
    stack = jnp.concatenate([b, amax], axis=0)
    hi = stack.astype(BF16)
    rest = stack - hi.astype(F32)
    mid = rest.astype(BF16)
    lo = (rest - mid.astype(F32)).astype(BF16)
    n = stack.shape[0]
    nq = b.shape[0]
    eye = (lax.broadcasted_iota(jnp.int32, (n, LANES), 0) == lax.broadcasted_iota(jnp.int32, (n, LANES), 1))
    eye = eye.astype(F32).astype(BF16)
    cols = (_dot_tn(hi, eye) + _dot_tn(mid, eye)) + _dot_tn(lo, eye)
    for j, ref in enumerate(qc_refs):
        ref[...] = cols if j == 0 else pltpu.roll(cols, LANES - j * nq, 1)


def _mlstm_chunk(qc, ktc, vc, b_col, amax_col, a_row, state, m, lower):
    t, dv = vc.shape
    r = lax.broadcasted_iota(jnp.int32, (t, t), 0)
    c = lax.broadcasted_iota(jnp.int32, (t, t), 1)
    incl = (c <= r) if lower else (c >= r)
    mm = jnp.broadcast_to(jnp.maximum(m, amax_col), (t, dv))
    b_rows = jnp.broadcast_to(b_col, (t, dv))
    sc = jnp.exp2(m - mm)
    floor = jnp.exp2(-(b_rows + mm))
    rep = t // dv
    w = jnp.exp2(jnp.where(incl, a_row - jnp.concatenate([mm] * rep, axis=1), -jnp.inf))
    qkw = _dot(qc, ktc) * w
    v_aug = jnp.concatenate([vc, jnp.ones_like(vc)], axis=1)
    res = jnp.concatenate([sc, sc], axis=1) * _dot(qc, state.astype(BF16)) + _dot(qkw.astype(BF16), v_aug)
    h = res[:, :dv] / jnp.maximum(jnp.abs(res[:, dv:]), floor)
    e = t - 1 if lower else 0
    bl = b_col[e:e + 1, :]
    m_new = bl + jnp.maximum(m, amax_col[e:e + 1, :])
    decay = jnp.exp2(bl + m - m_new)
    kw_t = (ktc.astype(F32) * jnp.exp2(bl + a_row - m_new)).astype(BF16)
    return h, decay * state + _dot(kw_t, v_aug), m_new


def _mlstm_kernel(q_ref, kt_ref, v_ref, sg_ref, gt_ref, hn_ref, o_ref, hf_ref, hb_ref, bq_ref, mq_ref,
                  ar_ref, st_ref, *, n_chunks):
    _mlstm_gate_scans(gt_ref, (bq_ref, mq_ref), ar_ref)
    st_ref[...] = jnp.zeros(st_ref.shape, F32)

    def run(off, hd, m, lower, dst_ref):
        r = hd if lower else MLSTM_HEADS + hd
        slot = 2 * hd + (0 if lower else 1)
        rows = pl.ds(off, CHUNK)
        cols = slice(hd * MLSTM_HD, (hd + 1) * MLSTM_HD)
        h, state, m = _mlstm_chunk(
            q_ref[rows, cols], kt_ref[cols, rows], v_ref[rows, cols],
            bq_ref[rows, r:r + 1], mq_ref[rows, r:r + 1], ar_ref[r:r + 1, rows], st_ref[slot], m, lower)
        st_ref[slot] = state
        dst_ref[rows, cols] = h
        return m

    def body(s, ms):
        off_f = pl.multiple_of(s * CHUNK, CHUNK)
        off_b = pl.multiple_of(jnp.where(s == 0, 0, n_chunks - s) * CHUNK, CHUNK)
        out = []
        for hd in range(MLSTM_HEADS):
            out.append(run(off_f, hd, ms[2 * hd], True, hf_ref))
            out.append(run(off_b, hd, ms[2 * hd + 1], False, hb_ref))
        return tuple(out)

    lax.fori_loop(0, n_chunks, body, tuple(jnp.zeros((1, 1), F32) for _ in range(2 * MLSTM_HEADS)))

    for hd in range(MLSTM_HEADS):
        cols = slice(hd * MLSTM_HD, (hd + 1) * MLSTM_HD)
        hm = hf_ref[:, cols] + hb_ref[:, cols]
        ms = jnp.mean(hm * hm, axis=-1, keepdims=True)
        y = hm * lax.rsqrt(ms + EPS) * hn_ref[:, cols]
        o_ref[:, cols] = (y * sg_ref[:, cols].astype(F32)).astype(BF16)


def _mlstm(q, k_t, v, sg, gates_t, hnorm):
    b, tok, _ = q.shape
    full = pl.BlockSpec((None, tok, MIX_A), lambda i: (i, 0, 0))
    return pl.pallas_call(
        functools.partial(_mlstm_kernel, n_chunks=tok // CHUNK),
        grid=(b,),
        in_specs=[
            full, pl.BlockSpec((MIX_A, tok), lambda i: (0, i)), full, full,
            pl.BlockSpec((N_GATES, tok), lambda i: (0, i)),
            pl.BlockSpec((1, MIX_A), lambda i: (0, 0)),
        ],
        out_specs=full,
        out_shape=jax.ShapeDtypeStruct((b, tok, MIX_A), BF16),
        scratch_shapes=[
            pltpu.VMEM((tok, MIX_A), F32), pltpu.VMEM((tok, MIX_A), F32),
            pltpu.VMEM((tok, LANES), F32), pltpu.VMEM((tok, LANES), F32),
            pltpu.VMEM((2 * MLSTM_HEADS, tok), F32),
            pltpu.VMEM((2 * MLSTM_HEADS, MLSTM_HD, 2 * MLSTM_HD), F32),
        ],
        compiler_params=_cparams("parallel"),
        name="mlstm",
    )(q, k_t, v, sg, gates_t, hnorm)


def _short_conv_rows(x, w, b, ctx):
    tok = x.shape[0]
    t = lax.broadcasted_iota(jnp.int32, x.shape, 0)
    prev = jnp.where((t == 0) | (t == ctx), 0.0, pltpu.roll(x, 1, 0))
    nxt = jnp.where((t == ctx - 1) | (t == tok - 1), 0.0, pltpu.roll(x, tok - 1, 0))
    return b + prev * w[0:1, :] + x * w[1:2, :] + nxt * w[2:3, :]


def _filter_kernel(emb_ref, w1_ref, b1_ref, w2_ref, b2_ref, fr_ref, w3_ref, dec_ref, o_ref, hid_ref):
    @pl.when(pl.program_id(0) == 0)
    def _():
        fr = fr_ref[...]
        h = jnp.sin(fr * (_dot(emb_ref[...].astype(BF16), w1_ref[...]) + b1_ref[...]))
        hid_ref[...] = jnp.sin(fr * (_dot(h.astype(BF16), w2_ref[...]) + b2_ref[...])).astype(BF16)

    h = _dot(hid_ref[...], w3_ref[...]) * dec_ref[...]
    o_ref[...] = h * lax.rsqrt(jnp.sum(h * h, axis=0, keepdims=True) + EPS)


def _hyena_filters(length, w1p, b1p, w2p, b2p, frp, w3p):
    bands = (HYENA_EMB - 1) // 2
    t = np.linspace(0.0, 1.0, length)[:, None]
    wpos = 2.0 * math.pi * np.arange(length) / length
    fr = np.linspace(1e-4, bands - 1, bands)
    ang = wpos[:, None] * fr[None, :]
    emb = np.concatenate([t, np.cos(ang), -np.sin(ang)], axis=-1)
    emb = np.pad(emb, ((0, 0), (0, LANES - HYENA_EMB)))
    deltas = np.abs(np.linspace(math.log(HYENA_TARGET) / HYENA_SLOW, math.log(HYENA_TARGET) / HYENA_FAST, MIX_B))
    decay = np.exp(-t * deltas[None, :])
    ngrp = w3p.shape[1] // MIX_B
    const = lambda g: (0, 0)
    return pl.pallas_call(
        _filter_kernel,
        grid=(ngrp,),
        in_specs=[
            pl.BlockSpec((length, LANES), const), pl.BlockSpec((LANES, LANES), const),
            pl.BlockSpec((1, LANES), const), pl.BlockSpec((LANES, LANES), const),
            pl.BlockSpec((1, LANES), const), pl.BlockSpec((1, LANES), const),
            pl.BlockSpec((LANES, MIX_B), lambda g: (0, g)),
            pl.BlockSpec((length, MIX_B), const),
        ],
        out_specs=pl.BlockSpec((length, MIX_B), lambda g: (0, g)),
        out_shape=jax.ShapeDtypeStruct((length, ngrp * MIX_B), F32),
        scratch_shapes=[pltpu.VMEM((length, LANES), BF16)],
        compiler_params=_cparams("arbitrary"),
        name="hyena_filters",
    )(jnp.asarray(emb, F32), w1p, b1p, w2p, b2p, frp, w3p, jnp.asarray(decay, F32))


def _dft_matrices(length):
    period = 2 * length
    k = jnp.arange(length, dtype=jnp.int32)[:, None]
    t = jnp.arange(length, dtype=jnp.int32)[None, :]
    unit = 2.0 * math.pi / period

    def table(freqs):
        ang = ((freqs[:, None] * t) & (period - 1)).astype(F32) * unit
        return jnp.cos(ang), jnp.sin(ang)

    c1, s1 = table(jnp.arange(length // DFT_SPLIT, dtype=jnp.int32) * DFT_SPLIT)
    c0, s0 = table(jnp.arange(DFT_SPLIT, dtype=jnp.int32))
    c = (c1[:, None, :] * c0[None, :, :] - s1[:, None, :] * s0[None, :, :]).reshape(length, length)
    s = (s1[:, None, :] * c0[None, :, :] + c1[:, None, :] * s0[None, :, :]).reshape(length, length)
    alt_t = jnp.where(t % 2 == 0, 1.0, -1.0).astype(F32)
    fwd = jnp.concatenate([c, jnp.where(k == 0, alt_t, -s)], axis=0)
    coef = jnp.where(t == 0, 1.0, 2.0).astype(F32) / period
    alt_k = jnp.where(k % 2 == 0, 1.0, -1.0).astype(F32)
    inv = jnp.concatenate([c * coef, jnp.where(t == 0, alt_k / period, -s * coef)], axis=1)
    return fwd.astype(BF16), inv.astype(BF16)


def _spectrum_kernel(a_ref, hf_ref, hb_ref, o_ref):
    half = pl.program_id(0)
    hf = hf_ref[...]
    hb = hb_ref[...]
    comb = jnp.where(half == 0, hf + hb, hf - hb).astype(BF16)
    o_ref[...] = _dot(a_ref[...], comb)

    @pl.when(half == 1)
    def _():
        t = lax.broadcasted_iota(jnp.int32, hf.shape, 0)
        o_ref[0:1, :] = jnp.sum(jnp.where(t % 2 == 0, hf + hb, -(hf + hb)), axis=0, keepdims=True)


def _filter_spectrum(fwd, filt):
    period, length = fwd.shape
    orders = filt.shape[1] // (2 * MIX_B)
    return pl.pallas_call(
        _spectrum_kernel,
        grid=(2, orders),
        in_specs=[
            pl.BlockSpec((length, length), lambda hlf, o: (hlf, 0)),
            pl.BlockSpec((length, MIX_B), lambda hlf, o: (0, 2 * o)),
            pl.BlockSpec((length, MIX_B), lambda hlf, o: (0, 2 * o + 1)),
        ],
        out_specs=pl.BlockSpec((length, MIX_B), lambda hlf, o: (hlf, o)),
        out_shape=jax.ShapeDtypeStruct((period, orders * MIX_B), F32),
        compiler_params=_cparams("arbitrary", "arbitrary"),
        name="hyena_filter_spectrum",
    )(fwd, filt, filt)


HY_CH = 256
BFLY_ROWS = 128


def _spectral_product(a_ref, g_ref, z, length):
    zr = _dot(a_ref[0:length, :], z)
    zi = _dot(a_ref[length:2 * length, :], z)
    gr = g_ref[0:length, :]
    gi = g_ref[length:2 * length, :]
    first = lax.broadcasted_iota(jnp.int32, zr.shape, 0) == 0
    yr = jnp.where(first, zr * gr, zr * gr - zi * gi)
    yi = jnp.where(first, zi * gi, zr * gi + zi * gr)
    return jnp.concatenate([yr, yi], axis=0).astype(BF16)


def _radix2_conv(ze, zo, a_ref, b_ref, g_ref, tw_ref):
    n = ze.shape[0]
    e = _dot(a_ref[...], ze)
    o = _dot(a_ref[...], zo)
    rep = ze.shape[1] // LANES
    f0r, f0i, f1r, f1i = [], [], [], []
    for r0 in range(0, n, BFLY_ROWS):
        rows = slice(r0, r0 + BFLY_ROWS)
        rows_im = slice(n + r0, n + r0 + BFLY_ROWS)
        er, ei, orr, oi = e[rows, :], e[rows_im, :], o[rows, :], o[rows_im, :]
        c = jnp.concatenate([tw_ref[rows, :]] * rep, axis=1)
        s = jnp.concatenate([tw_ref[rows_im, :]] * rep, axis=1)
        gar, gai, gbr, gbi = (g_ref[j * n + r0:j * n + r0 + BFLY_ROWS, :] for j in range(4))
        tr = c * orr + s * oi
        ti = c * oi - s * orr
        zar = er + tr
        zbr = er - tr
        zai = ei + ti
        zbi = ti - ei
        if r0 == 0:
            first = lax.broadcasted_iota(jnp.int32, er.shape, 0) == 0
            zai = jnp.where(first, zbr, zai)
            zbi = jnp.where(first, -oi, zbi)
            zbr = jnp.where(first, ei, zbr)
        yar = zar * gar - zai * gai
        yai = zar * gai + zai * gar
        if r0 == 0:
            yar = jnp.where(first, zar * gar, yar)
            yai = jnp.where(first, zai * gai, yai)
        ybr = zbr * gbr - zbi * gbi
        ybi = zbr * gbi + zbi * gbr
        dr = yar - ybr
        di = yai + ybi
        p0r = 0.5 * (yar + ybr)
        p0i = 0.5 * (yai - ybi)
        p1r = 0.5 * (c * dr - s * di)
        p1i = 0.5 * (c * di + s * dr)
        if r0 == 0:
            p0r = jnp.where(first, 0.5 * (yar + yai), p0r)
            p0i = jnp.where(first, ybr, p0i)
            p1r = jnp.where(first, 0.5 * (yar - yai), p1r)
            p1i = jnp.where(first, -ybi, p1i)
        f0r.append(p0r.astype(BF16))
        f0i.append(p0i.astype(BF16))
        f1r.append(p1r.astype(BF16))
        f1i.append(p1i.astype(BF16))
    y_even = _dot(b_ref[...], jnp.concatenate(f0r + f0i, axis=0))
    y_odd = _dot(b_ref[...], jnp.concatenate(f1r + f1i, axis=0))
    return y_even, y_odd


def _hyena_order_kernel(*refs, ctx, seq, conv_z):
    if conv_z:
        (z_ref, zw_ref, zb_ref, x_ref, xw_ref, xb_ref, d_ref, ah_ref, bh_ref, ac_ref, bc_ref, gl_ref, gc_ref,
         tw_ref, o_ref, zs_ref, gs_ref, os_ref) = refs
        z = _short_conv_rows(z_ref[...], zw_ref[...], zb_ref[...], ctx)
    else:
        (z_ref, x_ref, xw_ref, xb_ref, d_ref, ah_ref, bh_ref, ac_ref, bc_ref, gl_ref, gc_ref,
         tw_ref, o_ref, zs_ref, gs_ref, os_ref) = refs
        z = z_ref[...].astype(F32)
    gate = _short_conv_rows(x_ref[...], xw_ref[...], xb_ref[...], ctx)
    d = d_ref[...]
    half = seq // 2
    n_blk = zs_ref.shape[0]

    zc = z[0:ctx, :]
    yc = _dot(bc_ref[...], _spectral_product(ac_ref, gc_ref, zc.astype(BF16), ctx))
    o_ref[0:ctx, :] = (gate[0:ctx, :] * (yc + d * zc)).astype(o_ref.dtype)

    for j in range(n_blk):
        zs_ref[j] = z[ctx:, j * LANES:(j + 1) * LANES]
        gs_ref[j] = gate[ctx:, j * LANES:(j + 1) * LANES]

    def samples(ref, parity):
        return jnp.concatenate([ref[j, pl.ds(parity, half, stride=2), :] for j in range(n_blk)], axis=1)

    ze = samples(zs_ref, 0)
    zo = samples(zs_ref, 1)
    y_even, y_odd = _radix2_conv(ze.astype(BF16), zo.astype(BF16), ah_ref, bh_ref, gl_ref, tw_ref)
    for parity, y, zp in ((0, y_even, ze), (1, y_odd, zo)):
        out = samples(gs_ref, parity) * (y + d * zp)
        for j in range(n_blk):
            os_ref[j, pl.ds(parity, half, stride=2), :] = out[:, j * LANES:(j + 1) * LANES]
    for j in range(n_blk):
        o_ref[ctx:, j * LANES:(j + 1) * LANES] = os_ref[j].astype(o_ref.dtype)


def _conv_specs(blk):
    n_ch = MIX_B // HY_CH
    return [pl.BlockSpec((3, HY_CH), lambda j, i: (0, blk * n_ch + j)),
            pl.BlockSpec((1, HY_CH), lambda j, i: (0, blk * n_ch + j))]


def _hyena_order(z, conv_z, hy, xblk, conv_x, dskip, order, mats, g_lat, g_ctx, twiddle, ctx, seq, out_dtype):
    a_half, b_half, a_ctx, b_ctx = mats
    b, tok, _ = hy.shape
    n_ch = MIX_B // HY_CH
    has_z = conv_z is not None
    const = lambda j, i: (0, 0)
    return pl.pallas_call(
        functools.partial(_hyena_order_kernel, ctx=ctx, seq=seq, conv_z=has_z),
        grid=(n_ch, b),
        in_specs=[pl.BlockSpec((None, tok, HY_CH), lambda j, i: (i, 0, j))]
        + (_conv_specs(0) if has_z else [])
        + [pl.BlockSpec((None, tok, HY_CH), lambda j, i: (i, 0, xblk * n_ch + j))]
        + _conv_specs(xblk)
        + [
            pl.BlockSpec((None, 1, HY_CH), lambda j, i: (order, 0, j)),
            _resident(a_half.shape, const), _resident(b_half.shape, const),
            _resident(a_ctx.shape, const), _resident(b_ctx.shape, const),
            pl.BlockSpec((2 * seq, HY_CH), lambda j, i: (0, order * n_ch + j)),
            pl.BlockSpec((2 * ctx, HY_CH), lambda j, i: (0, order * n_ch + j)),
            _resident(twiddle.shape, const),
        ],
        out_specs=pl.BlockSpec((None, tok, HY_CH), lambda j, i: (i, 0, j)),
        out_shape=jax.ShapeDtypeStruct((b, tok, MIX_B), out_dtype),
        scratch_shapes=[pltpu.VMEM((HY_CH // LANES, seq, LANES), F32)] * 3,
        compiler_params=_cparams("parallel", "parallel"),
        name="hyena_order",
    )(z, *(conv_z if has_z else ()), hy, *conv_x, dskip, a_half, b_half, a_ctx, b_ctx, g_lat, g_ctx, twiddle)


def _spectrum_bfly_kernel(a_ref, hf_ref, hb_ref, o_ref):
    p = pl.program_id(1)
    q = (p % 2) * 2 + p // 2
    hf = hf_ref[...]
    hb = hb_ref[...]
    t = lax.broadcasted_iota(jnp.int32, hf.shape, 0)
    alt = jnp.where(t % 2 == 0, 1.0, -1.0)
    comb = jnp.where(q % 2 == 0, hf + hb, hf - hb) * jnp.where(q < 2, 1.0, alt)
    res = _dot(a_ref[...], comb.astype(BF16))
    o_ref[...] = jnp.where(q == 3, -res, res)

    quarter = t % 4

    @pl.when(q == 1)
    def _():
        o_ref[0:1, :] = jnp.sum(alt * (hf + hb), axis=0, keepdims=True)

    @pl.when(q == 2)
    def _():
        w = jnp.where(quarter == 0, 1.0, jnp.where(quarter == 2, -1.0, 0.0))
        o_ref[0:1, :] = jnp.sum(w * (hf + hb), axis=0, keepdims=True)

    @pl.when(q == 3)
    def _():
        w = jnp.where(quarter == 1, -1.0, jnp.where(quarter == 3, 1.0, 0.0))
        o_ref[0:1, :] = jnp.sum(w * (hf - hb), axis=0, keepdims=True)


def _filter_spectrum_bfly(fwd, filt):
    period, length = fwd.shape
    n = length // 2
    orders = filt.shape[1] // (2 * MIX_B)
    return pl.pallas_call(
        _spectrum_bfly_kernel,
        grid=(orders, 4),
        in_specs=[
            pl.BlockSpec((n, length), lambda o, p: ((p // 2) * (length // n), 0)),
            pl.BlockSpec((length, MIX_B), lambda o, p: (0, 2 * o)),
            pl.BlockSpec((length, MIX_B), lambda o, p: (0, 2 * o + 1)),
        ],
        out_specs=pl.BlockSpec((n, MIX_B), lambda o, p: ((p % 2) * 2 + p // 2, o)),
        out_shape=jax.ShapeDtypeStruct((period, orders * MIX_B), F32),
        compiler_params=_cparams("arbitrary", "arbitrary"),
        name="hyena_filter_spectrum_bfly",
    )(fwd, filt, filt)


def _twiddles(length):
    n = length // 2
    ang = jnp.arange(n, dtype=F32) * (2.0 * math.pi / (2 * length))
    tw = jnp.concatenate([jnp.cos(ang), jnp.sin(ang)])[:, None]
    return jnp.broadcast_to(tw, (2 * n, LANES))


def _odd_in_kernel(x_ref, shc_ref, shb_ref, scc_ref, scb_ref, w_ref, qn_ref, kn_ref, q_ref, k_ref, v_ref,
                   *, n_sub, per_batch):
    lane = lax.broadcasted_iota(jnp.int32, (SUB_ROWS, LANES), 1)
    lo = lane < NA_HD
    half = D_MODEL // 2
    for sb in range(n_sub):
        rows = _sub_rows(sb)
        h = _normmod(x_ref[rows, :], _sub_mod(shc_ref, shb_ref, sb, per_batch),
                     _sub_mod(scc_ref, scb_ref, sb, per_batch)).astype(BF16)
        for part, gain_ref, scale, dst in ((0, qn_ref, NA_HD ** -0.5 * LOG2E, q_ref), (1, kn_ref, 1.0, k_ref)):
            gain = gain_ref[...]
            for c in range(2):
                base = part * D_MODEL + c * half
                acc = _dot(h, w_ref[:, base:base + half])
                for g in range(half // LANES):
                    a = acc[:, g * LANES:(g + 1) * LANES]
                    sq = a * a
                    s_lo = jnp.sum(jnp.where(lo, sq, 0.0), axis=-1, keepdims=True)
                    s_hi = jnp.sum(jnp.where(lo, 0.0, sq), axis=-1, keepdims=True)
                    inv = lax.rsqrt(jnp.where(lo, s_lo, s_hi) * (1.0 / NA_HD) + EPS)
                    y = a * inv * gain
                    if scale != 1.0:
                        y = y * scale
                    dst[rows, c * half + g * LANES:c * half + (g + 1) * LANES] = y.astype(BF16)
        for c in range(2):
            base = 2 * D_MODEL + c * half
            v_ref[rows, c * half:(c + 1) * half] = _dot(h, w_ref[:, base:base + half]).astype(BF16)


def _odd_in_proj(x2, mod5, layer, tiles, w_qkv, qn2, kn2):
    m = x2.shape[0]
    out = jax.ShapeDtypeStruct((m, D_MODEL), BF16)
    return pl.pallas_call(
        functools.partial(_odd_in_kernel, n_sub=tiles.n_sub, per_batch=tiles.per_batch),
        grid=(tiles.n,),
        in_specs=[tiles.row_spec(D_MODEL)] + tiles.mod_specs(layer, 0) + tiles.mod_specs(layer, 1) + [
            _resident((D_MODEL, 3 * D_MODEL), lambda i: (0, 0)),
            pl.BlockSpec((1, LANES), lambda i: (0, 0)), pl.BlockSpec((1, LANES), lambda i: (0, 0)),
        ],
        out_specs=[tiles.row_spec(D_MODEL)] * 3,
        out_shape=(out, out, out),
        compiler_params=_cparams("parallel"),
        name="odd_qkv_proj",
    )(x2, mod5, mod5, mod5, mod5, w_qkv, qn2, kn2)


def _pair_queries(q):
    lane = lax.broadcasted_iota(jnp.int32, q.shape, 1)
    zero = jnp.zeros_like(q)
    return jnp.concatenate([jnp.where(lane < NA_HD, q, zero), jnp.where(lane < NA_HD, zero, q)], axis=0)


def _unpair(res, n):
    lane = lax.broadcasted_iota(jnp.int32, (n, LANES), 1)
    return jnp.where(lane < NA_HD, res[0:n, :], res[n:2 * n, :])


def _natten_kernel(q_ref, k_ref, v_ref, t_ref, o_ref, va_ref, *, ctx, rows_n, win_rows, ctx_out):
    tok = v_ref.shape[0]
    va_ref[:, 0:LANES] = v_ref[...]
    va_ref[:, LANES:2 * LANES] = jnp.ones((tok, LANES), BF16)
    kc = k_ref[0:ctx, :]
    vc = va_ref[0:ctx, :]

    def softmax_pv(scores, values):
        m = functools.reduce(jnp.maximum, [jnp.max(s, axis=1, keepdims=True) for s in scores])
        res = None
        for s, vblk in zip(scores, values):
            part = _dot(jnp.exp2(s - m).astype(BF16), vblk)
            res = part if res is None else res + part
        return res[:, 0:LANES] * (1.0 / res[:, LANES:2 * LANES])

    if ctx_out:
        w = _pair_queries(q_ref[0:ctx, :])
        o_ref[0:ctx, :] = _unpair(softmax_pv([_dot_nt(w, kc)], [vc]), ctx).astype(BF16)
    else:
        o_ref[0:ctx, :] = jnp.zeros((ctx, LANES), BF16)

    nk = win_rows * GRID_W

    def body(r, carry):
        rs = jnp.clip(r - win_rows // 2, 0, rows_n - win_rows)
        dr0 = rs - r + NA_WIN_ROWS - 1
        q_off = pl.multiple_of(ctx + r * GRID_W, GRID_W)
        k_off = pl.multiple_of(ctx + rs * GRID_W, GRID_W)
        w = _pair_queries(q_ref[pl.ds(q_off, GRID_W), :])
        bias = t_ref[dr0 & 1, :, pl.ds(pl.multiple_of((dr0 >> 1) * LANES, LANES), nk)]
        s_lat = _dot_nt(w, k_ref[pl.ds(k_off, nk), :]) + bias
        s_ctx = _dot_nt(w, kc)
        out = softmax_pv([s_lat, s_ctx], [va_ref[pl.ds(k_off, nk), :], vc])
        o_ref[pl.ds(q_off, GRID_W), :] = _unpair(out, GRID_W).astype(BF16)
        return carry

    lax.fori_loop(0, rows_n, body, 0, unroll=min(NA_UNROLL, rows_n))


def _natten(q, k, v, table, ctx, seq, ctx_out):
    b, tok, _ = q.shape
    rows_n = seq // GRID_W
    assert rows_n >= NA_WIN_ROWS and rows_n % min(NA_UNROLL, rows_n) == 0
    pairs = NA_HEADS // 2
    pair_spec = pl.BlockSpec((None, tok, LANES), lambda p, i: (i, 0, p))
    return pl.pallas_call(
        functools.partial(_natten_kernel, ctx=ctx, rows_n=rows_n, win_rows=NA_WIN_ROWS, ctx_out=ctx_out),
        grid=(pairs, b),
        in_specs=[
            pair_spec, pair_spec, pair_spec,
            pl.BlockSpec((None,) + table.shape[1:], lambda p, i: (p, 0, 0, 0)),
        ],
        out_specs=pair_spec,
        out_shape=jax.ShapeDtypeStruct((b, tok, D_MODEL), BF16),
        scratch_shapes=[pltpu.VMEM((tok, 2 * LANES), BF16)],
        compiler_params=_cparams("parallel", "parallel"),
        name="natten",
    )(q, k, v, table)


def _natten_bias_table(rpb):
    n_dr = 2 * NA_WIN_ROWS - 1
    n_dc = 2 * NA_WIN_COLS - 1
    cidx = np.arange(GRID_W)
    cstart = np.clip(cidx - NA_WIN_COLS // 2, 0, GRID_W - NA_WIN_COLS)
    cmask = (cidx[None, :] >= cstart[:, None]) & (cidx[None, :] < cstart[:, None] + NA_WIN_COLS)
    dc = np.clip(cidx[None, :] - cidx[:, None] + NA_WIN_COLS - 1, 0, n_dc - 1)
    onehot = ((dc[None] == np.arange(n_dc)[:, None, None]) & cmask[None]).astype(np.float32)
    tb = jnp.einsum("hrc,cqk->hrqk", rpb.astype(F32), jnp.asarray(onehot), precision=lax.Precision.HIGHEST)
    tb = tb * LOG2E + jnp.asarray(np.where(cmask, 0.0, NEG_BIG), F32)
    tb = tb.reshape(NA_HEADS // 2, 2, n_dr, GRID_W, GRID_W).transpose(0, 1, 3, 2, 4)
    tb = tb.reshape(NA_HEADS // 2, 2 * GRID_W, n_dr * GRID_W)
    width = (n_dr + 1) * GRID_W
    tb = jnp.pad(tb, ((0, 0), (0, 0), (0, width + GRID_W - n_dr * GRID_W)))
    return jnp.stack([tb[:, :, 0:width], tb[:, :, GRID_W:GRID_W + width]], axis=1)


FF_CHUNK = 1024


def _mix_mlp_kernel(*refs, widths, n_sub, per_batch, latent_only):
    x_ref = refs[0]
    g1, sh, sc, g2 = (refs[1 + 2 * j:3 + 2 * j] for j in range(4))
    in_refs = refs[9:9 + len(widths)]
    wo_ref, w1_ref, w2_ref, o_ref = refs[9 + len(widths):]

    def compute(sb):
        rows = _sub_rows(sb)
        mix = None
        off = 0
        for r, wd in zip(in_refs, widths):
            part = _dot(r[rows, :].astype(BF16), wo_ref[off:off + wd, :])
            mix = part if mix is None else mix + part
            off += wd
        x = x_ref[rows, :] + _sub_mod(*g1, sb, per_batch) * mix
        h = _normmod(x, _sub_mod(*sh, sb, per_batch), _sub_mod(*sc, sb, per_batch)).astype(BF16)
        acc = None
        for c0 in range(0, D_FF, FF_CHUNK):
            a = jnp.maximum(_dot(h, w1_ref[:, c0:c0 + FF_CHUNK]), 0.0)
            part = _dot((a * a).astype(BF16), w2_ref[c0:c0 + FF_CHUNK, :])
            acc = part if acc is None else acc + part
        o_ref[rows, :] = x + _sub_mod(*g2, sb, per_batch) * acc

    if latent_only:
        assert n_sub == 1
        pl.when(pl.program_id(0) % per_batch != 0)(functools.partial(compute, 0))
    else:
        for sb in range(n_sub):
            compute(sb)


def _mix_mlp(x2, mod5, layer, tiles, parts, w_out, w1, w2, latent_only):
    widths = tuple(p.shape[1] for p in parts)
    pb = tiles.per_batch
    if latent_only:
        out_rows = tiles.batch * (tiles.tok - tiles.ctx)
        out_spec = pl.BlockSpec((tiles.rows, D_MODEL), lambda i: (i - i // pb - jnp.where(i % pb == 0, 0, 1), 0))
    else:
        out_rows = x2.shape[0]
        out_spec = tiles.row_spec(D_MODEL)
    mods = [s for which in (2, 3, 4, 5) for s in tiles.mod_specs(layer, which)]
    return pl.pallas_call(
        functools.partial(_mix_mlp_kernel, widths=widths, n_sub=tiles.n_sub, per_batch=pb, latent_only=latent_only),
        grid=(tiles.n,),
        in_specs=[tiles.row_spec(D_MODEL)] + mods
        + [tiles.row_spec(wd) for wd in widths]
        + [_resident(w_out.shape, lambda i: (0, 0)), _resident(w1.shape, lambda i: (0, 0)),
           _resident(w2.shape, lambda i: (0, 0))],
        out_specs=out_spec,
        out_shape=jax.ShapeDtypeStruct((out_rows, D_MODEL), F32),
        input_output_aliases={} if latent_only else {0: 0},
        compiler_params=_cparams("arbitrary" if latent_only else "parallel"),
        name="mix_out_mlp",
    )(x2, *([mod5] * len(mods)), *parts, w_out, w1, w2)


def _pad_cols(a, width):
    return jnp.pad(a, ((0, 0), (0, width - a.shape[1])))


def kernel(x, c, ctx, c_ctx, w_mod, b_mod, w_mlp_in, w_mlp_out, e_w_in, e_gate_b, e_hnorm, e_conv_w, e_conv_b,
           e_f_w1, e_f_b1, e_f_w2, e_f_b2, e_f_w3, e_f_freq, e_hy_d, e_w_out, o_w_qkv, o_qn, o_kn, o_rpb, o_w_out):
    batch, seq, d = x.shape
    n_ctx = ctx.shape[1]
    depth = w_mod.shape[0]
    tok = n_ctx + seq
    assert d == D_MODEL and seq % CHUNK == 0 and n_ctx == CHUNK and seq % GRID_W == 0
    tiles = _Tiles(batch, tok, n_ctx, TILE_ROWS if tok % TILE_ROWS == 0 else SUB_ROWS)
    tiles_last = _Tiles(batch, tok, n_ctx, SUB_ROWS)

    n_samp = -(-(batch + 1) // 8) * 8
    cvec = jnp.concatenate([c, c_ctx[None, :], jnp.zeros((n_samp - batch - 1, d), F32)], axis=0)
    mod5 = _modulation(cvec, w_mod, b_mod)

    cos_t, sin_t = _rope_tables(n_ctx, seq)
    a_lat, _ = _dft_matrices(seq)
    a_ctx, b_ctx = _dft_matrices(n_ctx)
    dft_mats = _dft_matrices(seq // 2) + (a_ctx, b_ctx)
    twiddle = _twiddles(seq)

    xs = (ctx.reshape(batch * n_ctx, d), x.reshape(batch * seq, d))

    for l in range(depth):
        i = l // 2
        if l % 2 == 0:
            w_in = e_w_in[i]
            g0 = 4 * MIX_A
            w_pad = jnp.concatenate(
                [w_in[:, :g0], _pad_cols(w_in[:, g0:g0 + N_GATES], LANES), w_in[:, g0 + N_GATES:]], axis=1).astype(BF16)
            gate_b = _pad_cols(e_gate_b[i][None, :], LANES)
            q, k_t, v, sg, g_t, hy, *joint = _even_in_proj(xs, mod5, l, tiles, w_pad, gate_b, cos_t, sin_t)
            if joint:
                xs = joint[0]
            as3 = lambda a: a.reshape(batch, tok, a.shape[-1])
            a_out = _mlstm(as3(q), k_t, as3(v), as3(sg), g_t, e_hnorm[i][None, :])

            hy3 = as3(hy)
            conv = (e_conv_w[i], e_conv_b[i][None, :])
            pad2 = lambda a: jnp.pad(a, ((0, LANES - a.shape[0]), (0, LANES - a.shape[1])))
            w1p = pad2(e_f_w1[i]).astype(BF16)
            w2p = pad2(e_f_w2[i]).astype(BF16)
            w3p = jnp.pad(e_f_w3[i], ((0, LANES - HYENA_FFN), (0, 0))).astype(BF16)
            b1p = _pad_cols(e_f_b1[i][None, :], LANES)
            b2p = _pad_cols(e_f_b2[i][None, :], LANES)
            frp = _pad_cols(e_f_freq[i][None, :], LANES)
            g_lat = _filter_spectrum_bfly(a_lat, _hyena_filters(seq, w1p, b1p, w2p, b2p, frp, w3p))
            g_ctx = _filter_spectrum(a_ctx, _hyena_filters(n_ctx, w1p, b1p, w2p, b2p, frp, w3p))
            dskip = e_hy_d[i][:, None, :]
            n_ord = e_hy_d.shape[1]
            z, conv_z = hy3, conv
            for o in range(n_ord):
                z = _hyena_order(z, conv_z, hy3, 1 + o, conv, dskip, o, dft_mats, g_lat, g_ctx, twiddle,
                                 n_ctx, seq, F32)
                conv_z = None
            parts = [a_out.reshape(batch * tok, MIX_A), z.reshape(batch * tok, MIX_B)]
            w_out = e_w_out[i].astype(BF16)
        else:
            rep = LANES // NA_HD
            qn2 = jnp.tile(o_qn[i], rep)[None, :]
            kn2 = jnp.tile(o_kn[i], rep)[None, :]
            q, k, v = _odd_in_proj(xs, mod5, l, tiles, o_w_qkv[i].astype(BF16), qn2, kn2)
            as3 = lambda a: a.reshape(batch, tok, D_MODEL)
            table = _natten_bias_table(o_rpb[i])
            att = _natten(as3(q), as3(k), as3(v), table, n_ctx, seq, ctx_out=(l != depth - 1))
            parts = [att.reshape(batch * tok, D_MODEL)]
            w_out = o_w_out[i].astype(BF16)
        last = l == depth - 1
        xs = _mix_mlp(xs, mod5, l, tiles_last if last else tiles, parts, w_out,
                      w_mlp_in[l].astype(BF16), w_mlp_out[l].astype(BF16), latent_only=last)

    return xs.reshape(batch, seq, d)
```

```python
import functools
import math

import numpy as np
import jax
import jax.numpy as jnp
from jax import lax
from jax.experimental import pallas as pl
from jax.experimental.pallas import tpu as pltpu

F32 = jnp.float32
BF16 = jnp.bfloat16

D_MODEL = 1024
D_FF = 4 * D_MODEL
EPS = 1e-6
ROPE_BASE = 10000.0
GRID_W = 64
MIX_A = D_MODEL // 2
MIX_B = D_MODEL - MIX_A
MLSTM_HEADS = 4
MLSTM_HD = MIX_A // MLSTM_HEADS
N_GATES = 4 * MLSTM_HEADS
HYENA_EMB = 33
HYENA_FFN = 64
HYENA_TARGET = 1e-2
HYENA_FAST = 0.3
HYENA_SLOW = 1.5
NA_HEADS = 16
NA_HD = D_MODEL // NA_HEADS
NA_WIN_ROWS = 8
NA_WIN_COLS = 16

LANES = 128
SUB_ROWS = 256
TILE_ROWS = 768
CHUNK = 256
VMEM_LIMIT = 56 * 1024 * 1024
NEG_BIG = -1e30
LOG2E = math.log2(math.e)
DFT_SPLIT = 64
NA_UNROLL = 32


def _cparams(*sem):
    return pltpu.CompilerParams(dimension_semantics=sem, vmem_limit_bytes=VMEM_LIMIT)


def _dot(a, b):
    return jnp.dot(a, b, preferred_element_type=F32)


def _dot_nt(a, b):
    return lax.dot_general(a, b, (((1,), (1,)), ((), ())), preferred_element_type=F32)


def _dot_tn(a, b):
    return lax.dot_general(a, b, (((0,), (0,)), ((), ())), preferred_element_type=F32)


def _resident(shape, index_map):
    return pl.BlockSpec(shape, index_map, pipeline_mode=pl.Buffered(1))


def _normmod(x, sh, sc):
    ms = jnp.mean(x * x, axis=-1, keepdims=True)
    return (x * lax.rsqrt(ms + EPS)) * (1.0 + sc) + sh


def _sigmoid(x):
    return 1.0 / (1.0 + jnp.exp(-x))


def _log_sigmoid(x):
    return jnp.minimum(x, 0.0) - jnp.log(1.0 + jnp.exp(-jnp.abs(x)))


def _mod_kernel(c_ref, w_ref, b_ref, o_ref):
    c = c_ref[...]
    s = (c * _sigmoid(c)).astype(BF16)
    o_ref[...] = _dot(s, w_ref[...].astype(BF16)) + b_ref[...]


def _modulation(cvec, w_mod, b_mod):
    depth, d, d6 = w_mod.shape
    ns = cvec.shape[0]
    nj = d6 // d
    out = pl.pallas_call(
        _mod_kernel,
        grid=(depth, nj),
        in_specs=[
            pl.BlockSpec((ns, d), lambda l, j: (0, 0)),
            pl.BlockSpec((None, d, d), lambda l, j: (l, 0, j)),
            pl.BlockSpec((None, 1, d), lambda l, j: (l, 0, j)),
        ],
        out_specs=pl.BlockSpec((None, ns, d), lambda l, j: (l, 0, j)),
        out_shape=jax.ShapeDtypeStruct((depth, ns, d6), F32),
        compiler_params=_cparams("arbitrary", "arbitrary"),
        name="modulation",
    )(cvec, w_mod, b_mod.reshape(depth, 1, d6))
    return out.reshape(depth, ns, nj, 1, d)


class _Tiles:
    def __init__(self, batch, tok, ctx, rows):
        assert ctx == SUB_ROWS and rows % SUB_ROWS == 0 and tok % rows == 0
        self.batch, self.tok, self.ctx, self.rows = batch, tok, ctx, rows
        self.n_sub = rows // SUB_ROWS
        self.per_batch = tok // rows
        self.n = batch * self.per_batch

    def mod_specs(self, layer, which):
        blk = (None, None, None, 1, D_MODEL)
        return [pl.BlockSpec(blk, lambda i: (layer, self.batch, which, 0, 0)),
                pl.BlockSpec(blk, lambda i: (layer, i // self.per_batch, which, 0, 0))]

    def row_spec(self, width):
        return pl.BlockSpec((self.rows, width), lambda i: (i, 0))

    def pos_spec(self, width):
        return pl.BlockSpec((self.rows, width), lambda i: (i % self.per_batch, 0))


def _sub_mod(c_ref, b_ref, sb, per_batch):
    if sb > 0:
        return b_ref[...]
    return jnp.where(pl.program_id(0) % per_batch == 0, c_ref[...], b_ref[...])


def _sub_rows(sb):
    return slice(sb * SUB_ROWS, (sb + 1) * SUB_ROWS)


def _even_in_kernel(*refs, n_sub, per_batch, split_in):
    if split_in:
        ctx_ref, lat_refs, refs = refs[0], refs[1:1 + n_sub], refs[1 + n_sub:]
    else:
        x_ref, refs = refs[0], refs[1:]
    (shc_ref, shb_ref, scc_ref, scb_ref, w_ref, gb_ref, cos_ref, sin_ref,
     q_ref, kt_ref, v_ref, sg_ref, gt_ref, hy_ref) = refs[:14]
    lane = lax.broadcasted_iota(jnp.int32, (SUB_ROWS, MLSTM_HD), 1)
    first = (lane % (MLSTM_HD // 2)) < (MLSTM_HD // 4)
    g0 = 4 * MIX_A
    h0 = g0 + LANES
    for sb in range(n_sub):
        rows = _sub_rows(sb)
        if split_in:
            x = lat_refs[sb][...]
            if sb == 0:
                x = jnp.where(pl.program_id(0) % per_batch == 0, ctx_ref[...], x)
            refs[14][rows, :] = x
        else:
            x = x_ref[rows, :]
        h = _normmod(x, _sub_mod(shc_ref, shb_ref, sb, per_batch),
                     _sub_mod(scc_ref, scb_ref, sb, per_batch)).astype(BF16)
        cos = cos_ref[rows, :]
        sin = sin_ref[rows, :]

        def rope(a):
            part = jnp.where(first, pltpu.roll(a, LANES - MLSTM_HD // 4, 1), pltpu.roll(a, MLSTM_HD // 4, 1))
            return a * cos + part * sin

        acc = _dot(h, w_ref[:, 0:MIX_A])
        for hd in range(MLSTM_HEADS):
            sl = slice(hd * MLSTM_HD, (hd + 1) * MLSTM_HD)
            q_ref[rows, sl] = rope(acc[:, sl]).astype(BF16)
        acc = _dot(h, w_ref[:, MIX_A:2 * MIX_A]) * (MLSTM_HD ** -0.5)
        for hd in range(MLSTM_HEADS):
            sl = slice(hd * MLSTM_HD, (hd + 1) * MLSTM_HD)
            kt_ref[sl, rows] = rope(acc[:, sl]).T.astype(BF16)
        v_ref[rows, :] = _dot(h, w_ref[:, 2 * MIX_A:3 * MIX_A]).astype(BF16)
        sg_ref[rows, :] = _sigmoid(_dot(h, w_ref[:, 3 * MIX_A:4 * MIX_A])).astype(BF16)
        gt_ref[:, rows] = (_dot(h, w_ref[:, g0:g0 + LANES]) + gb_ref[...]).T
        for j in range(3):
            hy_ref[rows, j * MIX_B:(j + 1) * MIX_B] = _dot(h, w_ref[:, h0 + j * MIX_B:h0 + (j + 1) * MIX_B])


def _even_in_proj(x2, mod5, layer, tiles, w_pad, gate_b_pad, cos_t, sin_t):
    split_in = isinstance(x2, tuple)
    m = tiles.batch * tiles.tok
    n_w = w_pad.shape[1]
    outs = (
        jax.ShapeDtypeStruct((m, MIX_A), BF16), jax.ShapeDtypeStruct((MIX_A, m), BF16),
        jax.ShapeDtypeStruct((m, MIX_A), BF16), jax.ShapeDtypeStruct((m, MIX_A), BF16),
        jax.ShapeDtypeStruct((LANES, m), F32), jax.ShapeDtypeStruct((m, 3 * MIX_B), F32),
    )
    out_specs = [tiles.row_spec(MIX_A), pl.BlockSpec((MIX_A, tiles.rows), lambda i: (0, i)),
                 tiles.row_spec(MIX_A), tiles.row_spec(MIX_A),
                 pl.BlockSpec((LANES, tiles.rows), lambda i: (0, i)), tiles.row_spec(3 * MIX_B)]
    if split_in:
        pb, ns = tiles.per_batch, tiles.n_sub
        lat_blocks = (tiles.tok - tiles.ctx) // SUB_ROWS

        def lat_spec(sb):
            return pl.BlockSpec((SUB_ROWS, D_MODEL), lambda i: (
                (i // pb) * lat_blocks + jnp.maximum((i % pb) * ns + sb - 1, 0), 0))

        x_specs = [pl.BlockSpec((SUB_ROWS, D_MODEL), lambda i: (i // pb, 0))] + [lat_spec(sb) for sb in range(ns)]
        x_args = (x2[0],) + (x2[1],) * ns
        outs = outs + (jax.ShapeDtypeStruct((m, D_MODEL), F32),)
        out_specs = out_specs + [tiles.row_spec(D_MODEL)]
    else:
        x_specs = [tiles.row_spec(D_MODEL)]
        x_args = (x2,)
    return pl.pallas_call(
        functools.partial(_even_in_kernel, n_sub=tiles.n_sub, per_batch=tiles.per_batch, split_in=split_in),
        grid=(tiles.n,),
        in_specs=x_specs + tiles.mod_specs(layer, 0) + tiles.mod_specs(layer, 1) + [
            _resident((D_MODEL, n_w), lambda i: (0, 0)),
            pl.BlockSpec((1, LANES), lambda i: (0, 0)),
            tiles.pos_spec(MLSTM_HD), tiles.pos_spec(MLSTM_HD),
        ],
        out_specs=out_specs,
        out_shape=outs,
        compiler_params=_cparams("parallel"),
        name="even_in_proj",
    )(*x_args, mod5, mod5, mod5, mod5, w_pad, gate_b_pad, cos_t, sin_t)


def _rope_tables(ctx, seq):
    half = MLSTM_HD // 2
    nf = half // 2
    inv = ROPE_BASE ** (-np.arange(nf, dtype=np.float64) / nf)
    t = np.arange(seq)
    rows, cols = t // GRID_W, t % GRID_W

    def one(pos):
        ang = pos[:, None].astype(np.float64) * inv[None, :]
        c = np.concatenate([np.cos(ang), np.cos(ang)], axis=-1)
        s = np.concatenate([-np.sin(ang), np.sin(ang)], axis=-1)
        return c, s

    cr, sr = one(rows)
    cc, sc = one(cols)
    cos = np.concatenate([cr, cc], axis=-1)
    sin = np.concatenate([sr, sc], axis=-1)
    cos = np.concatenate([np.ones((ctx, MLSTM_HD)), cos], axis=0)
    sin = np.concatenate([np.zeros((ctx, MLSTM_HD)), sin], axis=0)
    return jnp.asarray(cos, F32), jnp.asarray(sin, F32)


def _seg_scan(y, pos, op, reverse, axis):
    n = y.shape[axis]
    k = 1
    while k < CHUNK:
        if reverse:
            y = jnp.where(pos < CHUNK - k, op(y, pltpu.roll(y, n - k, axis)), y)
        else:
            y = jnp.where(pos >= k, op(y, pltpu.roll(y, k, axis)), y)
        k *= 2
    return y


def _mlstm_gate_scans(gt_ref, qc_refs, ar_ref):
    hh = MLSTM_HEADS
    g = gt_ref[...]
    gi = jnp.concatenate([g[0:hh], g[2 * hh:3 * hh]], axis=0) * LOG2E
    lf = _log_sigmoid(jnp.concatenate([g[hh:2 * hh], g[3 * hh:4 * hh]], axis=0)) * LOG2E
    fwd = lax.broadcasted_iota(jnp.int32, lf.shape, 0) < hh
    pos = lax.broadcasted_iota(jnp.int32, lf.shape, 1) % CHUNK
    b = jnp.where(fwd, _seg_scan(lf, pos, jnp.add, False, 1), _seg_scan(lf, pos, jnp.add, True, 1))
    a = gi - b
    amax = jnp.where(fwd, _seg_scan(a, pos, jnp.maximum, False, 1), _seg_scan(a, pos, jnp.maximum, True, 1))
    ar_ref[...] = a
    stack = jnp.concatenate([b, amax], axis=0)
    hi = stack.astype(BF16)
    rest = stack - hi.astype(F32)
    mid = rest.astype(BF16)
    lo = (rest - mid.astype(F32)).astype(BF16)
    n = stack.shape[0]
    nq = b.shape[0]
    eye = (lax.broadcasted_iota(jnp.int32, (n, LANES), 0) == lax.broadcasted_iota(jnp.int32, (n, LANES), 1))
    eye = eye.astype(F32).astype(BF16)
    cols = (_dot_tn(hi, eye) + _dot_tn(mid, eye)) + _dot_tn(lo, eye)
    for j, ref in enumerate(qc_refs):
        ref[...] = cols if j == 0 else pltpu.roll(cols, LANES - j * nq, 1)


def _mlstm_chunk(qc, ktc, vc, b_col, amax_col, a_row, state, m, lower):
    t, dv = vc.shape
    r = lax.broadcasted_iota(jnp.int32, (t, t), 0)
    c = lax.broadcasted_iota(jnp.int32, (t, t), 1)
    incl = (c <= r) if lower else (c >= r)
    mm = jnp.broadcast_to(jnp.maximum(m, amax_col), (t, dv))
    b_rows = jnp.broadcast_to(b_col, (t, dv))
    sc = jnp.exp2(m - mm)
    floor = jnp.exp2(-(b_rows + mm))
    rep = t // dv
    w = jnp.exp2(jnp.where(incl, a_row - jnp.concatenate([mm] * rep, axis=1), -jnp.inf))
    qkw = _dot(qc, ktc) * w
    v_aug = jnp.concatenate([vc, jnp.ones_like(vc)], axis=1)
    res = jnp.concatenate([sc, sc], axis=1) * _dot(qc, state.astype(BF16)) + _dot(qkw.astype(BF16), v_aug)
    h = res[:, :dv] / jnp.maximum(jnp.abs(res[:, dv:]), floor)
    e = t - 1 if lower else 0
    bl = b_col[e:e + 1, :]
    m_new = bl + jnp.maximum(m, amax_col[e:e + 1, :])
    decay = jnp.exp2(bl + m - m_new)
    kw_t = (ktc.astype(F32) * jnp.exp2(bl + a_row - m_new)).astype(BF16)
    return h, decay * state + _dot(kw_t, v_aug), m_new


def _mlstm_kernel(q_ref, kt_ref, v_ref, sg_ref, gt_ref, hn_ref, o_ref, hf_ref, hb_ref, bq_ref, mq_ref,
                  ar_ref, st_ref, *, n_chunks):
    _mlstm_gate_scans(gt_ref, (bq_ref, mq_ref), ar_ref)
    st_ref[...] = jnp.zeros(st_ref.shape, F32)

    def run(off, hd, m, lower, dst_ref):
        r = hd if lower else MLSTM_HEADS + hd
        slot = 2 * hd + (0 if lower else 1)
        rows = pl.ds(off, CHUNK)
        cols = slice(hd * MLSTM_HD, (hd + 1) * MLSTM_HD)
        h, state, m = _mlstm_chunk(
            q_ref[rows, cols], kt_ref[cols, rows], v_ref[rows, cols],
            bq_ref[rows, r:r + 1], mq_ref[rows, r:r + 1], ar_ref[r:r + 1, rows], st_ref[slot], m, lower)
        st_ref[slot] = state
        dst_ref[rows, cols] = h
        return m

    def body(s, ms):
        off_f = pl.multiple_of(s * CHUNK, CHUNK)
        off_b = pl.multiple_of(jnp.where(s == 0, 0, n_chunks - s) * CHUNK, CHUNK)
        out = []
        for hd in range(MLSTM_HEADS):
            out.append(run(off_f, hd, ms[2 * hd], True, hf_ref))
            out.append(run(off_b, hd, ms[2 * hd + 1], False, hb_ref))
        return tuple(out)

    lax.fori_loop(0, n_chunks, body, tuple(jnp.zeros((1, 1), F32) for _ in range(2 * MLSTM_HEADS)))

    for hd in range(MLSTM_HEADS):
        cols = slice(hd * MLSTM_HD, (hd + 1) * MLSTM_HD)
        hm = hf_ref[:, cols] + hb_ref[:, cols]
        ms = jnp.mean(hm * hm, axis=-1, keepdims=True)
        y = hm * lax.rsqrt(ms + EPS) * hn_ref[:, cols]
        o_ref[:, cols] = (y * sg_ref[:, cols].astype(F32)).astype(BF16)


def _mlstm(q, k_t, v, sg, gates_t, hnorm):
    b, tok, _ = q.shape
    full = pl.BlockSpec((None, tok, MIX_A), lambda i: (i, 0, 0))
    return pl.pallas_call(
        functools.partial(_mlstm_kernel, n_chunks=tok // CHUNK),
        grid=(b,),
        in_specs=[
            full, pl.BlockSpec((MIX_A, tok), lambda i: (0, i)), full, full,
            pl.BlockSpec((N_GATES, tok), lambda i: (0, i)),
            pl.BlockSpec((1, MIX_A), lambda i: (0, 0)),
        ],
        out_specs=full,
        out_shape=jax.ShapeDtypeStruct((b, tok, MIX_A), BF16),
        scratch_shapes=[
            pltpu.VMEM((tok, MIX_A), F32), pltpu.VMEM((tok, MIX_A), F32),
            pltpu.VMEM((tok, LANES), F32), pltpu.VMEM((tok, LANES), F32),
            pltpu.VMEM((2 * MLSTM_HEADS, tok), F32),
            pltpu.VMEM((2 * MLSTM_HEADS, MLSTM_HD, 2 * MLSTM_HD), F32),
        ],
        compiler_params=_cparams("parallel"),
        name="mlstm",
    )(q, k_t, v, sg, gates_t, hnorm)


def _short_conv_rows(x, w, b, ctx):
    tok = x.shape[0]
    t = lax.broadcasted_iota(jnp.int32, x.shape, 0)
    prev = jnp.where((t == 0) | (t == ctx), 0.0, pltpu.roll(x, 1, 0))
    nxt = jnp.where((t == ctx - 1) | (t == tok - 1), 0.0, pltpu.roll(x, tok - 1, 0))
    return b + prev * w[0:1, :] + x * w[1:2, :] + nxt * w[2:3, :]


def _filter_kernel(emb_ref, w1_ref, b1_ref, w2_ref, b2_ref, fr_ref, w3_ref, dec_ref, o_ref, hid_ref):
    @pl.when(pl.program_id(0) == 0)
    def _():
        fr = fr_ref[...]
        h = jnp.sin(fr * (_dot(emb_ref[...].astype(BF16), w1_ref[...]) + b1_ref[...]))
        hid_ref[...] = jnp.sin(fr * (_dot(h.astype(BF16), w2_ref[...]) + b2_ref[...])).astype(BF16)

    h = _dot(hid_ref[...], w3_ref[...]) * dec_ref[...]
    o_ref[...] = h * lax.rsqrt(jnp.sum(h * h, axis=0, keepdims=True) + EPS)


def _hyena_filters(length, w1p, b1p, w2p, b2p, frp, w3p):
    bands = (HYENA_EMB - 1) // 2
    t = np.linspace(0.0, 1.0, length)[:, None]
    wpos = 2.0 * math.pi * np.arange(length) / length
    fr = np.linspace(1e-4, bands - 1, bands)
    ang = wpos[:, None] * fr[None, :]
    emb = np.concatenate([t, np.cos(ang), -np.sin(ang)], axis=-1)
    emb = np.pad(emb, ((0, 0), (0, LANES - HYENA_EMB)))
    deltas = np.abs(np.linspace(math.log(HYENA_TARGET) / HYENA_SLOW, math.log(HYENA_TARGET) / HYENA_FAST, MIX_B))
    decay = np.exp(-t * deltas[None, :])
    ngrp = w3p.shape[1] // MIX_B
    const = lambda g: (0, 0)
    return pl.pallas_call(
        _filter_kernel,
        grid=(ngrp,),
        in_specs=[
            pl.BlockSpec((length, LANES), const), pl.BlockSpec((LANES, LANES), const),
            pl.BlockSpec((1, LANES), const), pl.BlockSpec((LANES, LANES), const),
            pl.BlockSpec((1, LANES), const), pl.BlockSpec((1, LANES), const),
            pl.BlockSpec((LANES, MIX_B), lambda g: (0, g)),
            pl.BlockSpec((length, MIX_B), const),
        ],
        out_specs=pl.BlockSpec((length, MIX_B), lambda g: (0, g)),
        out_shape=jax.ShapeDtypeStruct((length, ngrp * MIX_B), F32),
        scratch_shapes=[pltpu.VMEM((length, LANES), BF16)],
        compiler_params=_cparams("arbitrary"),
        name="hyena_filters",
    )(jnp.asarray(emb, F32), w1p, b1p, w2p, b2p, frp, w3p, jnp.asarray(decay, F32))


def _dft_matrices(length):
    period = 2 * length
    k = jnp.arange(length, dtype=jnp.int32)[:, None]
    t = jnp.arange(length, dtype=jnp.int32)[None, :]
    unit = 2.0 * math.pi / period

    def table(freqs):
        ang = ((freqs[:, None] * t) & (period - 1)).astype(F32) * unit
        return jnp.cos(ang), jnp.sin(ang)

    c1, s1 = table(jnp.arange(length // DFT_SPLIT, dtype=jnp.int32) * DFT_SPLIT)
    c0, s0 = table(jnp.arange(DFT_SPLIT, dtype=jnp.int32))
    c = (c1[:, None, :] * c0[None, :, :] - s1[:, None, :] * s0[None, :, :]).reshape(length, length)
    s = (s1[:, None, :] * c0[None, :, :] + c1[:, None, :] * s0[None, :, :]).reshape(length, length)
    alt_t = jnp.where(t % 2 == 0, 1.0, -1.0).astype(F32)
    fwd = jnp.concatenate([c, jnp.where(k == 0, alt_t, -s)], axis=0)
    coef = jnp.where(t == 0, 1.0, 2.0).astype(F32) / period
    alt_k = jnp.where(k % 2 == 0, 1.0, -1.0).astype(F32)
    inv = jnp.concatenate([c * coef, jnp.where(t == 0, alt_k / period, -s * coef)], axis=1)
    return fwd.astype(BF16), inv.astype(BF16)


def _spectrum_kernel(a_ref, hf_ref, hb_ref, o_ref):
    half = pl.program_id(0)
    hf = hf_ref[...]
    hb = hb_ref[...]
    comb = jnp.where(half == 0, hf + hb, hf - hb).astype(BF16)
    o_ref[...] = _dot(a_ref[...], comb)

    @pl.when(half == 1)
    def _():
        t = lax.broadcasted_iota(jnp.int32, hf.shape, 0)
        o_ref[0:1, :] = jnp.sum(jnp.where(t % 2 == 0, hf + hb, -(hf + hb)), axis=0, keepdims=True)


def _filter_spectrum(fwd, filt):
    period, length = fwd.shape
    orders = filt.shape[1] // (2 * MIX_B)
    return pl.pallas_call(
        _spectrum_kernel,
        grid=(2, orders),
        in_specs=[
            pl.BlockSpec((length, length), lambda hlf, o: (hlf, 0)),
            pl.BlockSpec((length, MIX_B), lambda hlf, o: (0, 2 * o)),
            pl.BlockSpec((length, MIX_B), lambda hlf, o: (0, 2 * o + 1)),
        ],
        out_specs=pl.BlockSpec((length, MIX_B), lambda hlf, o: (hlf, o)),
        out_shape=jax.ShapeDtypeStruct((period, orders * MIX_B), F32),
        compiler_params=_cparams("arbitrary", "arbitrary"),
        name="hyena_filter_spectrum",
    )(fwd, filt, filt)


HY_CH = 256
BFLY_ROWS = 256


def _spectral_product(a_ref, g_ref, z, length):
    zr = _dot(a_ref[0:length, :], z)
    zi = _dot(a_ref[length:2 * length, :], z)
    gr = g_ref[0:length, :]
    gi = g_ref[length:2 * length, :]
    first = lax.broadcasted_iota(jnp.int32, zr.shape, 0) == 0
    yr = jnp.where(first, zr * gr, zr * gr - zi * gi)
    yi = jnp.where(first, zi * gi, zr * gi + zi * gr)
    return jnp.concatenate([yr, yi], axis=0).astype(BF16)


def _radix2_conv(ze, zo, a_ref, b_ref, g_ref, tw_ref):
    n = ze.shape[0]
    rep = ze.shape[1] // LANES
    f0r, f0i, f1r, f1i = [], [], [], []
    for r0 in range(0, n, BFLY_ROWS):
        rows = slice(r0, r0 + BFLY_ROWS)
        rows_im = slice(n + r0, n + r0 + BFLY_ROWS)
        er, ei = _dot(a_ref[rows, :], ze), _dot(a_ref[rows_im, :], ze)
        orr, oi = _dot(a_ref[rows, :], zo), _dot(a_ref[rows_im, :], zo)
        c = jnp.concatenate([tw_ref[rows, :]] * rep, axis=1)
        s = jnp.concatenate([tw_ref[rows_im, :]] * rep, axis=1)
        gar, gai, gbr, gbi = (g_ref[j * n + r0:j * n + r0 + BFLY_ROWS, :] for j in range(4))
        tr = c * orr + s * oi
        ti = c * oi - s * orr
        zar = er + tr
        zbr = er - tr
        zai = ei + ti
        zbi = ti - ei
        if r0 == 0:
            first = lax.broadcasted_iota(jnp.int32, er.shape, 0) == 0
            zai = jnp.where(first, zbr, zai)
            zbi = jnp.where(first, -oi, zbi)
            zbr = jnp.where(first, ei, zbr)
        yar = zar * gar - zai * gai
        yai = zar * gai + zai * gar
        if r0 == 0:
            yar = jnp.where(first, zar * gar, yar)
            yai = jnp.where(first, zai * gai, yai)
        ybr = zbr * gbr - zbi * gbi
        ybi = zbr * gbi + zbi * gbr
        dr = yar - ybr
        di = yai + ybi
        p0r = 0.5 * (yar + ybr)
        p0i = 0.5 * (yai - ybi)
        p1r = 0.5 * (c * dr - s * di)
        p1i = 0.5 * (c * di + s * dr)
        if r0 == 0:
            p0r = jnp.where(first, 0.5 * (yar + yai), p0r)
            p0i = jnp.where(first, ybr, p0i)
            p1r = jnp.where(first, 0.5 * (yar - yai), p1r)
            p1i = jnp.where(first, -ybi, p1i)
        f0r.append(p0r.astype(BF16))
        f0i.append(p0i.astype(BF16))
        f1r.append(p1r.astype(BF16))
        f1i.append(p1i.astype(BF16))
    y_even = _dot(b_ref[...], jnp.concatenate(f0r + f0i, axis=0))
    y_odd = _dot(b_ref[...], jnp.concatenate(f1r + f1i, axis=0))
    return y_even, y_odd


def _hyena_order_kernel(*refs, ctx, seq, conv_z):
    if conv_z:
        (z_ref, zw_ref, zb_ref, x_ref, xw_ref, xb_ref, d_ref, ah_ref, bh_ref, ac_ref, bc_ref, gl_ref, gc_ref,
         tw_ref, o_ref, zs_ref, gs_ref, os_ref) = refs
        z = _short_conv_rows(z_ref[...], zw_ref[...], zb_ref[...], ctx)
    else:
        (z_ref, x_ref, xw_ref, xb_ref, d_ref, ah_ref, bh_ref, ac_ref, bc_ref, gl_ref, gc_ref,
         tw_ref, o_ref, zs_ref, gs_ref, os_ref) = refs
        z = z_ref[...].astype(F32)
    gate = _short_conv_rows(x_ref[...], xw_ref[...], xb_ref[...], ctx)
    d = d_ref[...]
    half = seq // 2
    n_blk = zs_ref.shape[0]

    zc = z[0:ctx, :]
    yc = _dot(bc_ref[...], _spectral_product(ac_ref, gc_ref, zc.astype(BF16), ctx))
    o_ref[0:ctx, :] = (gate[0:ctx, :] * (yc + d * zc)).astype(o_ref.dtype)

    for j in range(n_blk):
        zs_ref[j] = z[ctx:, j * LANES:(j + 1) * LANES]
        gs_ref[j] = gate[ctx:, j * LANES:(j + 1) * LANES]

    def samples(ref, parity):
        return jnp.concatenate([ref[j, pl.ds(parity, half, stride=2), :] for j in range(n_blk)], axis=1)

    ze = samples(zs_ref, 0)
    zo = samples(zs_ref, 1)
    y_even, y_odd = _radix2_conv(ze.astype(BF16), zo.astype(BF16), ah_ref, bh_ref, gl_ref, tw_ref)
    for parity, y, zp in ((0, y_even, ze), (1, y_odd, zo)):
        out = samples(gs_ref, parity) * (y + d * zp)
        for j in range(n_blk):
            os_ref[j, pl.ds(parity, half, stride=2), :] = out[:, j * LANES:(j + 1) * LANES]
    for j in range(n_blk):
        o_ref[ctx:, j * LANES:(j + 1) * LANES] = os_ref[j].astype(o_ref.dtype)


def _conv_specs(blk):
    n_ch = MIX_B // HY_CH
    return [pl.BlockSpec((3, HY_CH), lambda j, i: (0, blk * n_ch + j)),
            pl.BlockSpec((1, HY_CH), lambda j, i: (0, blk * n_ch + j))]


def _hyena_order(z, conv_z, hy, xblk, conv_x, dskip, order, mats, g_lat, g_ctx, twiddle, ctx, seq, out_dtype):
    a_half, b_half, a_ctx, b_ctx = mats
    b, tok, _ = hy.shape
    n_ch = MIX_B // HY_CH
    has_z = conv_z is not None
    const = lambda j, i: (0, 0)
    return pl.pallas_call(
        functools.partial(_hyena_order_kernel, ctx=ctx, seq=seq, conv_z=has_z),
        grid=(n_ch, b),
        in_specs=[pl.BlockSpec((None, tok, HY_CH), lambda j, i: (i, 0, j))]
        + (_conv_specs(0) if has_z else [])
        + [pl.BlockSpec((None, tok, HY_CH), lambda j, i: (i, 0, xblk * n_ch + j))]
        + _conv_specs(xblk)
        + [
            pl.BlockSpec((None, 1, HY_CH), lambda j, i: (order, 0, j)),
            _resident(a_half.shape, const), _resident(b_half.shape, const),
            _resident(a_ctx.shape, const), _resident(b_ctx.shape, const),
            pl.BlockSpec((2 * seq, HY_CH), lambda j, i: (0, order * n_ch + j)),
            pl.BlockSpec((2 * ctx, HY_CH), lambda j, i: (0, order * n_ch + j)),
            _resident(twiddle.shape, const),
        ],
        out_specs=pl.BlockSpec((None, tok, HY_CH), lambda j, i: (i, 0, j)),
        out_shape=jax.ShapeDtypeStruct((b, tok, MIX_B), out_dtype),
        scratch_shapes=[pltpu.VMEM((HY_CH // LANES, seq, LANES), F32)] * 3,
        compiler_params=_cparams("parallel", "parallel"),
        name="hyena_order",
    )(z, *(conv_z if has_z else ()), hy, *conv_x, dskip, a_half, b_half, a_ctx, b_ctx, g_lat, g_ctx, twiddle)


def _spectrum_bfly_kernel(a_ref, hf_ref, hb_ref, o_ref):
    p = pl.program_id(1)
    q = (p % 2) * 2 + p // 2
    hf = hf_ref[...]
    hb = hb_ref[...]
    t = lax.broadcasted_iota(jnp.int32, hf.shape, 0)
    alt = jnp.where(t % 2 == 0, 1.0, -1.0)
    comb = jnp.where(q % 2 == 0, hf + hb, hf - hb) * jnp.where(q < 2, 1.0, alt)
    res = _dot(a_ref[...], comb.astype(BF16))
    o_ref[...] = jnp.where(q == 3, -res, res)

    quarter = t % 4

    @pl.when(q == 1)
    def _():
        o_ref[0:1, :] = jnp.sum(alt * (hf + hb), axis=0, keepdims=True)

    @pl.when(q == 2)
    def _():
        w = jnp.where(quarter == 0, 1.0, jnp.where(quarter == 2, -1.0, 0.0))
        o_ref[0:1, :] = jnp.sum(w * (hf + hb), axis=0, keepdims=True)

    @pl.when(q == 3)
    def _():
        w = jnp.where(quarter == 1, -1.0, jnp.where(quarter == 3, 1.0, 0.0))
        o_ref[0:1, :] = jnp.sum(w * (hf - hb), axis=0, keepdims=True)


def _filter_spectrum_bfly(fwd, filt):
    period, length = fwd.shape
    n = length // 2
    orders = filt.shape[1] // (2 * MIX_B)
    return pl.pallas_call(
        _spectrum_bfly_kernel,
        grid=(orders, 4),
        in_specs=[
            pl.BlockSpec((n, length), lambda o, p: ((p // 2) * (length // n), 0)),
            pl.BlockSpec((length, MIX_B), lambda o, p: (0, 2 * o)),
            pl.BlockSpec((length, MIX_B), lambda o, p: (0, 2 * o + 1)),
        ],
        out_specs=pl.BlockSpec((n, MIX_B), lambda o, p: ((p % 2) * 2 + p // 2, o)),
        out_shape=jax.ShapeDtypeStruct((period, orders * MIX_B), F32),
        compiler_params=_cparams("arbitrary", "arbitrary"),
        name="hyena_filter_spectrum_bfly",
    )(fwd, filt, filt)


def _twiddles(length):
    n = length // 2
    ang = jnp.arange(n, dtype=F32) * (2.0 * math.pi / (2 * length))
    tw = jnp.concatenate([jnp.cos(ang), jnp.sin(ang)])[:, None]
    return jnp.broadcast_to(tw, (2 * n, LANES))


def _odd_in_kernel(x_ref, shc_ref, shb_ref, scc_ref, scb_ref, w_ref, qn_ref, kn_ref, q_ref, k_ref, v_ref,
                   *, n_sub, per_batch):
    lane = lax.broadcasted_iota(jnp.int32, (SUB_ROWS, LANES), 1)
    lo = lane < NA_HD
    half = D_MODEL // 2
    for sb in range(n_sub):
        rows = _sub_rows(sb)
        h = _normmod(x_ref[rows, :], _sub_mod(shc_ref, shb_ref, sb, per_batch),
                     _sub_mod(scc_ref, scb_ref, sb, per_batch)).astype(BF16)
        for part, gain_ref, scale, dst in ((0, qn_ref, NA_HD ** -0.5 * LOG2E, q_ref), (1, kn_ref, 1.0, k_ref)):
            gain = gain_ref[...]
            for c in range(2):
                base = part * D_MODEL + c * half
                acc = _dot(h, w_ref[:, base:base + half])
                for g in range(half // LANES):
                    a = acc[:, g * LANES:(g + 1) * LANES]
                    sq = a * a
                    s_lo = jnp.sum(jnp.where(lo, sq, 0.0), axis=-1, keepdims=True)
                    s_hi = jnp.sum(jnp.where(lo, 0.0, sq), axis=-1, keepdims=True)
                    inv = lax.rsqrt(jnp.where(lo, s_lo, s_hi) * (1.0 / NA_HD) + EPS)
                    y = a * inv * gain
                    if scale != 1.0:
                        y = y * scale
                    dst[rows, c * half + g * LANES:c * half + (g + 1) * LANES] = y.astype(BF16)
        for c in range(2):
            base = 2 * D_MODEL + c * half
            v_ref[rows, c * half:(c + 1) * half] = _dot(h, w_ref[:, base:base + half]).astype(BF16)


def _odd_in_proj(x2, mod5, layer, tiles, w_qkv, qn2, kn2):
    m = x2.shape[0]
    out = jax.ShapeDtypeStruct((m, D_MODEL), BF16)
    return pl.pallas_call(
        functools.partial(_odd_in_kernel, n_sub=tiles.n_sub, per_batch=tiles.per_batch),
        grid=(tiles.n,),
        in_specs=[tiles.row_spec(D_MODEL)] + tiles.mod_specs(layer, 0) + tiles.mod_specs(layer, 1) + [
            _resident((D_MODEL, 3 * D_MODEL), lambda i: (0, 0)),
            pl.BlockSpec((1, LANES), lambda i: (0, 0)), pl.BlockSpec((1, LANES), lambda i: (0, 0)),
        ],
        out_specs=[tiles.row_spec(D_MODEL)] * 3,
        out_shape=(out, out, out),
        compiler_params=_cparams("parallel"),
        name="odd_qkv_proj",
    )(x2, mod5, mod5, mod5, mod5, w_qkv, qn2, kn2)


def _pair_queries(q):
    lane = lax.broadcasted_iota(jnp.int32, q.shape, 1)
    zero = jnp.zeros_like(q)
    return jnp.concatenate([jnp.where(lane < NA_HD, q, zero), jnp.where(lane < NA_HD, zero, q)], axis=0)


def _unpair(res, n):
    lane = lax.broadcasted_iota(jnp.int32, (n, LANES), 1)
    return jnp.where(lane < NA_HD, res[0:n, :], res[n:2 * n, :])


def _natten_kernel(q_ref, k_ref, v_ref, t_ref, o_ref, va_ref, *, ctx, rows_n, win_rows, ctx_out):
    tok = v_ref.shape[0]
    va_ref[:, 0:LANES] = v_ref[...]
    va_ref[:, LANES:2 * LANES] = jnp.ones((tok, LANES), BF16)
    kc = k_ref[0:ctx, :]
    vc = va_ref[0:ctx, :]

    def softmax_pv(scores, values):
        m = functools.reduce(jnp.maximum, [jnp.max(s, axis=1, keepdims=True) for s in scores])
        res = None
        for s, vblk in zip(scores, values):
            part = _dot(jnp.exp2(s - m).astype(BF16), vblk)
            res = part if res is None else res + part
        return res[:, 0:LANES] * (1.0 / res[:, LANES:2 * LANES])

    if ctx_out:
        w = _pair_queries(q_ref[0:ctx, :])
        o_ref[0:ctx, :] = _unpair(softmax_pv([_dot_nt(w, kc)], [vc]), ctx).astype(BF16)
    else:
        o_ref[0:ctx, :] = jnp.zeros((ctx, LANES), BF16)

    nk = win_rows * GRID_W

    def body(r, carry):
        rs = jnp.clip(r - win_rows // 2, 0, rows_n - win_rows)
        dr0 = rs - r + NA_WIN_ROWS - 1
        q_off = pl.multiple_of(ctx + r * GRID_W, GRID_W)
        k_off = pl.multiple_of(ctx + rs * GRID_W, GRID_W)
        w = _pair_queries(q_ref[pl.ds(q_off, GRID_W), :])
        bias = t_ref[dr0 & 1, :, pl.ds(pl.multiple_of((dr0 >> 1) * LANES, LANES), nk)]
        s_lat = _dot_nt(w, k_ref[pl.ds(k_off, nk), :]) + bias
        s_ctx = _dot_nt(w, kc)
        out = softmax_pv([s_lat, s_ctx], [va_ref[pl.ds(k_off, nk), :], vc])
        o_ref[pl.ds(q_off, GRID_W), :] = _unpair(out, GRID_W).astype(BF16)
        return carry

    lax.fori_loop(0, rows_n, body, 0, unroll=min(NA_UNROLL, rows_n))


def _natten(q, k, v, table, ctx, seq, ctx_out):
    b, tok, _ = q.shape
    rows_n = seq // GRID_W
    assert rows_n >= NA_WIN_ROWS and rows_n % min(NA_UNROLL, rows_n) == 0
    pairs = NA_HEADS // 2
    pair_spec = pl.BlockSpec((None, tok, LANES), lambda p, i: (i, 0, p))
    return pl.pallas_call(
        functools.partial(_natten_kernel, ctx=ctx, rows_n=rows_n, win_rows=NA_WIN_ROWS, ctx_out=ctx_out),
        grid=(pairs, b),
        in_specs=[
            pair_spec, pair_spec, pair_spec,
            pl.BlockSpec((None,) + table.shape[1:], lambda p, i: (p, 0, 0, 0)),
        ],
        out_specs=pair_spec,
        out_shape=jax.ShapeDtypeStruct((b, tok, D_MODEL), BF16),
        scratch_shapes=[pltpu.VMEM((tok, 2 * LANES), BF16)],
        compiler_params=_cparams("parallel", "parallel"),
        name="natten",
    )(q, k, v, table)


def _natten_bias_table(rpb):
    n_dr = 2 * NA_WIN_ROWS - 1
    n_dc = 2 * NA_WIN_COLS - 1
    cidx = np.arange(GRID_W)
    cstart = np.clip(cidx - NA_WIN_COLS // 2, 0, GRID_W - NA_WIN_COLS)
    cmask = (cidx[None, :] >= cstart[:, None]) & (cidx[None, :] < cstart[:, None] + NA_WIN_COLS)
    dc = np.clip(cidx[None, :] - cidx[:, None] + NA_WIN_COLS - 1, 0, n_dc - 1)
    onehot = ((dc[None] == np.arange(n_dc)[:, None, None]) & cmask[None]).astype(np.float32)
    tb = jnp.einsum("hrc,cqk->hrqk", rpb.astype(F32), jnp.asarray(onehot), precision=lax.Precision.HIGHEST)
    tb = tb * LOG2E + jnp.asarray(np.where(cmask, 0.0, NEG_BIG), F32)
    tb = tb.reshape(NA_HEADS // 2, 2, n_dr, GRID_W, GRID_W).transpose(0, 1, 3, 2, 4)
    tb = tb.reshape(NA_HEADS // 2, 2 * GRID_W, n_dr * GRID_W)
    width = (n_dr + 1) * GRID_W
    tb = jnp.pad(tb, ((0, 0), (0, 0), (0, width + GRID_W - n_dr * GRID_W)))
    return jnp.stack([tb[:, :, 0:width], tb[:, :, GRID_W:GRID_W + width]], axis=1)


FF_CHUNK = 1024


def _mix_mlp_kernel(*refs, widths, n_sub, per_batch, latent_only):
    x_ref = refs[0]
    g1, sh, sc, g2 = (refs[1 + 2 * j:3 + 2 * j] for j in range(4))
    in_refs = refs[9:9 + len(widths)]
    wo_ref, w1_ref, w2_ref, o_ref = refs[9 + len(widths):]

    def compute(sb):
        rows = _sub_rows(sb)
        mix = None
        off = 0
        for r, wd in zip(in_refs, widths):
            part = _dot(r[rows, :].astype(BF16), wo_ref[off:off + wd, :])
            mix = part if mix is None else mix + part
            off += wd
        x = x_ref[rows, :] + _sub_mod(*g1, sb, per_batch) * mix
        h = _normmod(x, _sub_mod(*sh, sb, per_batch), _sub_mod(*sc, sb, per_batch)).astype(BF16)
        acc = None
        for c0 in range(0, D_FF, FF_CHUNK):
            a = jnp.maximum(_dot(h, w1_ref[:, c0:c0 + FF_CHUNK]), 0.0)
            part = _dot((a * a).astype(BF16), w2_ref[c0:c0 + FF_CHUNK, :])
            acc = part if acc is None else acc + part
        o_ref[rows, :] = x + _sub_mod(*g2, sb, per_batch) * acc

    if latent_only:
        assert n_sub == 1
        pl.when(pl.program_id(0) % per_batch != 0)(functools.partial(compute, 0))
    else:
        for sb in range(n_sub):
            compute(sb)


def _mix_mlp(x2, mod5, layer, tiles, parts, w_out, w1, w2, latent_only):
    widths = tuple(p.shape[1] for p in parts)
    pb = tiles.per_batch
    if latent_only:
        out_rows = tiles.batch * (tiles.tok - tiles.ctx)
        out_spec = pl.BlockSpec((tiles.rows, D_MODEL), lambda i: (i - i // pb - jnp.where(i % pb == 0, 0, 1), 0))
    else:
        out_rows = x2.shape[0]
        out_spec = tiles.row_spec(D_MODEL)
    mods = [s for which in (2, 3, 4, 5) for s in tiles.mod_specs(layer, which)]
    return pl.pallas_call(
        functools.partial(_mix_mlp_kernel, widths=widths, n_sub=tiles.n_sub, per_batch=pb, latent_only=latent_only),
        grid=(tiles.n,),
        in_specs=[tiles.row_spec(D_MODEL)] + mods
        + [tiles.row_spec(wd) for wd in widths]
        + [_resident(w_out.shape, lambda i: (0, 0)), _resident(w1.shape, lambda i: (0, 0)),
           _resident(w2.shape, lambda i: (0, 0))],
        out_specs=out_spec,
        out_shape=jax.ShapeDtypeStruct((out_rows, D_MODEL), F32),
        input_output_aliases={} if latent_only else {0: 0},
        compiler_params=_cparams("arbitrary" if latent_only else "parallel"),
        name="mix_out_mlp",
    )(x2, *([mod5] * len(mods)), *parts, w_out, w1, w2)


def _pad_cols(a, width):
    return jnp.pad(a, ((0, 0), (0, width - a.shape[1])))


def kernel(x, c, ctx, c_ctx, w_mod, b_mod, w_mlp_in, w_mlp_out, e_w_in, e_gate_b, e_hnorm, e_conv_w, e_conv_b,
           e_f_w1, e_f_b1, e_f_w2, e_f_b2, e_f_w3, e_f_freq, e_hy_d, e_w_out, o_w_qkv, o_qn, o_kn, o_rpb, o_w_out):
    batch, seq, d = x.shape
    n_ctx = ctx.shape[1]
    depth = w_mod.shape[0]
    tok = n_ctx + seq
    assert d == D_MODEL and seq % CHUNK == 0 and n_ctx == CHUNK and seq % GRID_W == 0
    tiles = _Tiles(batch, tok, n_ctx, TILE_ROWS if tok % TILE_ROWS == 0 else SUB_ROWS)
    tiles_last = _Tiles(batch, tok, n_ctx, SUB_ROWS)

    n_samp = -(-(batch + 1) // 8) * 8
    cvec = jnp.concatenate([c, c_ctx[None, :], jnp.zeros((n_samp - batch - 1, d), F32)], axis=0)
    mod5 = _modulation(cvec, w_mod, b_mod)

    cos_t, sin_t = _rope_tables(n_ctx, seq)
    a_lat, _ = _dft_matrices(seq)
    a_ctx, b_ctx = _dft_matrices(n_ctx)
    dft_mats = _dft_matrices(seq // 2) + (a_ctx, b_ctx)
    twiddle = _twiddles(seq)

    xs = (ctx.reshape(batch * n_ctx, d), x.reshape(batch * seq, d))

    for l in range(depth):
        i = l // 2
        if l % 2 == 0:
            w_in = e_w_in[i]
            g0 = 4 * MIX_A
            w_pad = jnp.concatenate(
                [w_in[:, :g0], _pad_cols(w_in[:, g0:g0 + N_GATES], LANES), w_in[:, g0 + N_GATES:]], axis=1).astype(BF16)
            gate_b = _pad_cols(e_gate_b[i][None, :], LANES)
            q, k_t, v, sg, g_t, hy, *joint = _even_in_proj(xs, mod5, l, tiles, w_pad, gate_b, cos_t, sin_t)
            if joint:
                xs = joint[0]
            as3 = lambda a: a.reshape(batch, tok, a.shape[-1])
            a_out = _mlstm(as3(q), k_t, as3(v), as3(sg), g_t, e_hnorm[i][None, :])

            hy3 = as3(hy)
            conv = (e_conv_w[i], e_conv_b[i][None, :])
            pad2 = lambda a: jnp.pad(a, ((0, LANES - a.shape[0]), (0, LANES - a.shape[1])))
            w1p = pad2(e_f_w1[i]).astype(BF16)
            w2p = pad2(e_f_w2[i]).astype(BF16)
            w3p = jnp.pad(e_f_w3[i], ((0, LANES - HYENA_FFN), (0, 0))).astype(BF16)
            b1p = _pad_cols(e_f_b1[i][None, :], LANES)
            b2p = _pad_cols(e_f_b2[i][None, :], LANES)
            frp = _pad_cols(e_f_freq[i][None, :], LANES)
            g_lat = _filter_spectrum_bfly(a_lat, _hyena_filters(seq, w1p, b1p, w2p, b2p, frp, w3p))
            g_ctx = _filter_spectrum(a_ctx, _hyena_filters(n_ctx, w1p, b1p, w2p, b2p, frp, w3p))
            dskip = e_hy_d[i][:, None, :]
            n_ord = e_hy_d.shape[1]
            z, conv_z = hy3, conv
            for o in range(n_ord):
                z = _hyena_order(z, conv_z, hy3, 1 + o, conv, dskip, o, dft_mats, g_lat, g_ctx, twiddle,
                                 n_ctx, seq, F32)
                conv_z = None
            parts = [a_out.reshape(batch * tok, MIX_A), z.reshape(batch * tok, MIX_B)]
            w_out = e_w_out[i].astype(BF16)
        else:
            rep = LANES // NA_HD
            qn2 = jnp.tile(o_qn[i], rep)[None, :]
            kn2 = jnp.tile(o_kn[i], rep)[None, :]
            q, k, v = _odd_in_proj(xs, mod5, l, tiles, o_w_qkv[i].astype(BF16), qn2, kn2)
            as3 = lambda a: a.reshape(batch, tok, D_MODEL)
            table = _natten_bias_table(o_rpb[i])
            att = _natten(as3(q), as3(k), as3(v), table, n_ctx, seq, ctx_out=(l != depth - 1))
            parts = [att.reshape(batch * tok, D_MODEL)]
            w_out = o_w_out[i].astype(BF16)
        last = l == depth - 1
        xs = _mix_mlp(xs, mod5, l, tiles_last if last else tiles, parts, w_out,
                      w_mlp_in[l].astype(BF16), w_mlp_out[l].astype(BF16), latent_only=last)

    return xs.reshape(batch, seq, d)
```
